```python
import jax, jax.numpy as jnp
from jax import lax
import numpy as np

D_MODEL = 1024
BATCH = 2
SEQ = 8192
DEPTH = 1

N_META = 16
HEAD_DIM = 64
RMS_EPS = 1e-6
NEG_INF = -1e30
A_HEADS = 8
A_KV_HEADS = 2
A_WIDTH = A_HEADS * HEAD_DIM
A_KV_WIDTH = A_KV_HEADS * HEAD_DIM
WINDOW = 128
BLOCK = 128
ROT_DIM = HEAD_DIM // 4
ROPE_THETA = 500000.0
B_HEADS = 8
B_WIDTH = B_HEADS * HEAD_DIM
GRID_W = 64
NA_KH_MAX = 8
NA_KW = 16
SPLIT_SIZES = (A_WIDTH, A_KV_WIDTH, A_KV_WIDTH, A_WIDTH,
               B_WIDTH, B_WIDTH, B_WIDTH, B_WIDTH,
               D_MODEL, D_MODEL)
IN_COLS = sum(SPLIT_SIZES)

kernel_name = "hybrid_window_gqa_natten_gated_encoder"


def _rmsnorm(x, gain):
    x32 = x.astype(jnp.float32)
    y = x32 * lax.rsqrt(jnp.mean(x32 * x32, axis=-1, keepdims=True) + RMS_EPS)
    return y.astype(x.dtype) * gain


def _partial_rope(x, pos):
    half = ROT_DIM // 2
    inv_freq = ROPE_THETA ** (-jnp.arange(half, dtype=jnp.float32) / half)
    ang = pos[:, None] * inv_freq[None, :]
    cos = jnp.cos(ang)[None, :, None, :].astype(x.dtype)
    sin = jnp.sin(ang)[None, :, None, :].astype(x.dtype)
    x1 = x[..., :half]
    x2 = x[..., half:ROT_DIM]
    return jnp.concatenate([x1 * cos - x2 * sin, x2 * cos + x1 * sin, x[..., ROT_DIM:]], axis=-1)


def _softmax_with_sink(s, sink):
    s = jnp.concatenate([s, jnp.broadcast_to(sink, s.shape[:-1] + (1,))], axis=-1)
    return jax.nn.softmax(s, axis=-1)[..., :-1]


def _window_gqa(q, k, v, sink):
    bsz, L, _, dh = q.shape
    S = L - N_META
    nb = S // BLOCK
    G = A_HEADS // A_KV_HEADS
    scale = dh ** -0.5
    q = q.reshape(bsz, L, A_KV_HEADS, G, dh)
    qm, qr = q[:, :N_META], q[:, N_META:]
    km, kr = k[:, :N_META], k[:, N_META:]
    vm, vr = v[:, :N_META], v[:, N_META:]
    sink_f = sink.astype(jnp.float32).reshape(A_KV_HEADS, G)

    qb = qr.reshape(bsz, nb, BLOCK, A_KV_HEADS, G, dh)

    def band(t):
        tp = jnp.pad(t, ((0, 0), (BLOCK, BLOCK), (0, 0), (0, 0)))
        tp = tp.reshape(bsz, nb + 2, BLOCK, A_KV_HEADS, dh)
        return jnp.concatenate([tp[:, :-2], tp[:, 1:-1], tp[:, 2:]], axis=2)

    kband, vband = band(kr), band(vr)
    s_band = jnp.einsum('bnqkgd,bnjkd->bkgnqj', qb, kband).astype(jnp.float32) * scale
    qi = jnp.arange(BLOCK)[:, None]
    kj = jnp.arange(3 * BLOCK)[None, :]
    tk = (jnp.arange(nb)[:, None, None] - 1) * BLOCK + kj[None]
    band_mask = (jnp.abs(kj - BLOCK - qi)[None] <= WINDOW) & (tk >= 0) & (tk < S)
    s_band = jnp.where(band_mask, s_band, NEG_INF)
    s_meta = jnp.einsum('bnqkgd,bmkd->bkgnqm', qb, km).astype(jnp.float32) * scale
    p = _softmax_with_sink(jnp.concatenate([s_meta, s_band], axis=-1),
                           sink_f[None, :, :, None, None, None]).astype(v.dtype)
    o_r = (jnp.einsum('bkgnqm,bmkd->bnqkgd', p[..., :N_META], vm)
           + jnp.einsum('bkgnqj,bnjkd->bnqkgd', p[..., N_META:], vband))
    o_r = o_r.reshape(bsz, S, A_HEADS, dh)

    kmq = jnp.concatenate([km, kr[:, :BLOCK]], axis=1)
    vmq = jnp.concatenate([vm, vr[:, :BLOCK]], axis=1)
    s_mq = jnp.einsum('bqkgd,bjkd->bkgqj', qm, kmq).astype(jnp.float32) * scale
    kpos = jnp.arange(N_META + BLOCK)[None, :]
    qpos = jnp.arange(N_META)[:, None]
    s_mq = jnp.where(jnp.abs(kpos - qpos) <= WINDOW, s_mq, NEG_INF)
    p_mq = _softmax_with_sink(s_mq, sink_f[None, :, :, None, None]).astype(v.dtype)
    o_m = jnp.einsum('bkgqj,bjkd->bqkgd', p_mq, vmq).reshape(bsz, N_META, A_HEADS, dh)
    return jnp.concatenate([o_m, o_r], axis=1)


def _neighbourhood_attn(q, k, v, rpb):
    bsz, L, _, dh = q.shape
    S = L - N_META
    rows = S // GRID_W
    kh = min(NA_KH_MAX, rows)
    scale = dh ** -0.5
    qm, km, vm = q[:, :N_META], k[:, :N_META], v[:, :N_META]
    qg = q[:, N_META:].reshape(bsz, rows, GRID_W, B_HEADS, dh)
    kg = k[:, N_META:].reshape(bsz, rows, GRID_W, B_HEADS, dh)
    vg = v[:, N_META:].reshape(bsz, rows, GRID_W, B_HEADS, dh)

    col = jnp.arange(GRID_W)
    col_start = jnp.clip(col - NA_KW // 2, 0, GRID_W - NA_KW)
    col_idx = col_start[:, None] + jnp.arange(NA_KW)[None, :]
    dc_idx = col_idx - col[:, None] + (NA_KW - 1)

    def row_fn(r):
        r_start = jnp.clip(r - kh // 2, 0, rows - kh)
        q_r = lax.dynamic_index_in_dim(qg, r, axis=1, keepdims=False)
        k_rows = lax.dynamic_slice_in_dim(kg, r_start, kh, axis=1)
        v_rows = lax.dynamic_slice_in_dim(vg, r_start, kh, axis=1)
        k_win = k_rows[:, :, col_idx]
        v_win = v_rows[:, :, col_idx]
        s_loc = jnp.einsum('bqhd,brqwhd->bhqrw', q_r, k_win).astype(jnp.float32) * scale
        dr_idx = r_start + jnp.arange(kh) - r + (NA_KH_MAX - 1)
        bias = rpb[:, dr_idx[None, :, None], dc_idx[:, None, :]]
        s_loc = (s_loc + bias[None].astype(jnp.float32)).reshape(bsz, B_HEADS, GRID_W, kh * NA_KW)
        s_met = jnp.einsum('bqhd,bmhd->bhqm', q_r, km).astype(jnp.float32) * scale
        p = jax.nn.softmax(jnp.concatenate([s_met, s_loc], axis=-1), axis=-1).astype(v.dtype)
        p_loc = p[..., N_META:].reshape(bsz, B_HEADS, GRID_W, kh, NA_KW)
        return (jnp.einsum('bhqm,bmhd->bqhd', p[..., :N_META], vm)
                + jnp.einsum('bhqrw,brqwhd->bqhd', p_loc, v_win))

    o_g = lax.map(row_fn, jnp.arange(rows))
    o_r = jnp.transpose(o_g, (1, 0, 2, 3, 4)).reshape(bsz, S, B_HEADS, dh)
    s_mm = jnp.einsum('bqhd,bmhd->bhqm', qm, km).astype(jnp.float32) * scale
    p_mm = jax.nn.softmax(s_mm, axis=-1).astype(v.dtype)
    o_m = jnp.einsum('bhqm,bmhd->bqhd', p_mm, vm)
    return jnp.concatenate([o_m, o_r], axis=1)


def setup_inputs(seed: int = 0) -> dict:
    key = jax.random.key(seed)
    ks = jax.random.split(key, 10)
    f32 = jnp.float32
    x = jax.random.normal(ks[0], (BATCH, SEQ, D_MODEL), f32)
    meta_tokens = jax.random.normal(ks[1], (N_META, D_MODEL), f32)
    norm_gain = 1.0 + 0.01 * jax.random.normal(ks[2], (DEPTH, D_MODEL), f32)
    w_in = jax.random.normal(ks[3], (DEPTH, D_MODEL, IN_COLS), f32) * D_MODEL ** -0.5
    sink_logits = 0.5 * jax.random.normal(ks[4], (DEPTH, A_HEADS), f32)
    rel_pos_bias = 0.02 * jax.random.normal(ks[5], (DEPTH, B_HEADS, 2 * NA_KH_MAX - 1, 2 * NA_KW - 1), f32)
    w_proj_a = jax.random.normal(ks[6], (DEPTH, A_WIDTH, D_MODEL), f32) * A_WIDTH ** -0.5
    w_proj_b = jax.random.normal(ks[7], (DEPTH, B_WIDTH, D_MODEL), f32) * B_WIDTH ** -0.5
    w_out = jax.random.normal(ks[8], (DEPTH, D_MODEL, D_MODEL), f32) * D_MODEL ** -0.5
    final_norm_gain = 1.0 + 0.01 * jax.random.normal(ks[9], (D_MODEL,), f32)
    return {"x": x, "meta_tokens": meta_tokens, "norm_gain": norm_gain, "w_in": w_in,
            "sink_logits": sink_logits, "rel_pos_bias": rel_pos_bias, "w_proj_a": w_proj_a,
            "w_proj_b": w_proj_b, "w_out": w_out, "final_norm_gain": final_norm_gain}


def reference(x, meta_tokens, norm_gain, w_in, sink_logits, rel_pos_bias, w_proj_a, w_proj_b,
              w_out, final_norm_gain):
    bsz, S, _ = x.shape
    L = S + N_META
    h = jnp.concatenate([jnp.broadcast_to(meta_tokens[None].astype(x.dtype), (bsz, N_META, D_MODEL)), x], axis=1)
    pos = jnp.arange(L, dtype=jnp.float32)
    split_points = list(np.cumsum(SPLIT_SIZES)[:-1])
    for l in range(DEPTH):
        n = _rmsnorm(h, norm_gain[l])
        proj = jnp.einsum('bld,dc->blc', n, w_in[l])
        qa, ka, va, za, qb, kb, vb, zb, ga, gb = jnp.split(proj, split_points, axis=-1)
        qa = _partial_rope(qa.reshape(bsz, L, A_HEADS, HEAD_DIM), pos)
        ka = _partial_rope(ka.reshape(bsz, L, A_KV_HEADS, HEAD_DIM), pos)
        va = va.reshape(bsz, L, A_KV_HEADS, HEAD_DIM)
        oa = _window_gqa(qa, ka, va, sink_logits[l]).reshape(bsz, L, A_WIDTH) * jax.nn.silu(za)
        ob = _neighbourhood_attn(qb.reshape(bsz, L, B_HEADS, HEAD_DIM),
                                 kb.reshape(bsz, L, B_HEADS, HEAD_DIM),
                                 vb.reshape(bsz, L, B_HEADS, HEAD_DIM),
                                 rel_pos_bias[l]).reshape(bsz, L, B_WIDTH) * jax.nn.silu(zb)
        merged = (jax.nn.sigmoid(ga) * jnp.einsum('blc,cd->bld', oa, w_proj_a[l])
                  + jax.nn.sigmoid(gb) * jnp.einsum('blc,cd->bld', ob, w_proj_b[l]))
        h = h + jnp.einsum('bld,de->ble', merged, w_out[l])
    return _rmsnorm(h, final_norm_gain)[:, N_META:]
```

```python
import functools

import numpy as np
import jax
import jax.numpy as jnp
from jax import lax
from jax.experimental import pallas as pl
from jax.experimental.pallas import tpu as pltpu

D_MODEL = 1024
N_META = 16
HEAD_DIM = 64
RMS_EPS = 1e-6
NEG_INF = -1e30
A_HEADS = 8
A_KV_HEADS = 2
A_GROUP = A_HEADS // A_KV_HEADS
A_WIDTH = A_HEADS * HEAD_DIM
A_KV_WIDTH = A_KV_HEADS * HEAD_DIM
WINDOW = 128
BLOCK = 128
ROT_DIM = HEAD_DIM // 4
ROT_HALF = ROT_DIM // 2
ROPE_THETA = 500000.0
B_HEADS = 8
B_WIDTH = B_HEADS * HEAD_DIM
GRID_W = 64
NA_KH = 8
NA_KW = 16
SPLIT_SIZES = (A_WIDTH, A_KV_WIDTH, A_KV_WIDTH, A_WIDTH, B_WIDTH, B_WIDTH, B_WIDTH, B_WIDTH,
               D_MODEL, D_MODEL)

LANES = 128
CHUNK = 512
ROWS_PER_CHUNK = CHUNK // GRID_W
BLOCKS_PER_CHUNK = CHUNK // BLOCK
META_PAD = LANES
B_KEYS = NA_KH * GRID_W

KV_KA = 0
KV_VA = KV_KA + A_KV_HEADS * LANES
KV_KB = KV_VA + A_KV_HEADS * LANES
KV_VB = KV_KB + B_WIDTH
KV_COLS = KV_VB + B_WIDTH

R_QA = 0
R_QB = R_QA + A_WIDTH
R_ZA = R_QB + B_WIDTH
R_ZB = R_ZA + A_WIDTH
R_GA = R_ZB + B_WIDTH
R_GB = R_GA + D_MODEL
R_COLS = R_GB + D_MODEL

VMEM_LIMIT_BYTES = 58 * 1024 * 1024

_NT = (((1,), (1,)), ((), ()))


def _rmsnorm(x, gain):
    return x * lax.rsqrt(jnp.mean(x * x, axis=-1, keepdims=True) + RMS_EPS) * gain


def _rope_tile(t, c, sa, sb):
    return t * c + pltpu.roll(t, LANES - ROT_HALF, 1) * sa + pltpu.roll(t, ROT_HALF, 1) * sb


def _kv_body(x_ref, gain_ref, w_ref, c_ref, sa_ref, sb_ref, kv_ref):
    n = _rmsnorm(x_ref[0], gain_ref[...]).astype(jnp.bfloat16)
    kv = jnp.dot(n, w_ref[...], preferred_element_type=jnp.float32)
    c, sa, sb = c_ref[...], sa_ref[...], sb_ref[...]
    for t in range(A_KV_HEADS):
        lo = KV_KA + t * LANES
        kv_ref[0, :, lo:lo + LANES] = _rope_tile(kv[:, lo:lo + LANES], c, sa, sb).astype(jnp.bfloat16)
    kv_ref[0, :, KV_VA:] = kv[:, KV_VA:].astype(jnp.bfloat16)


def _kv_call(x, gain, w_kv, rope, rows):
    bsz, seq, _ = x.shape
    const = lambda b, i: (0, 0)
    return pl.pallas_call(
        _kv_body,
        grid=(bsz, seq // rows),
        in_specs=[
            pl.BlockSpec((1, rows, D_MODEL), lambda b, i: (b, i, 0)),
            pl.BlockSpec((1, D_MODEL), const),
            pl.BlockSpec((D_MODEL, KV_COLS), const, pipeline_mode=pl.Buffered(1)),
            pl.BlockSpec((rows, LANES), lambda b, i: (i, 0)),
            pl.BlockSpec((rows, LANES), lambda b, i: (i, 0)),
            pl.BlockSpec((rows, LANES), lambda b, i: (i, 0)),
        ],
        out_specs=pl.BlockSpec((1, rows, KV_COLS), lambda b, i: (b, i, 0)),
        out_shape=jax.ShapeDtypeStruct((bsz, seq, KV_COLS), jnp.bfloat16),
        compiler_params=pltpu.CompilerParams(
            dimension_semantics=("parallel", "parallel"), vmem_limit_bytes=VMEM_LIMIT_BYTES),
        name="kv_proj",
    )(x, gain, w_kv, *rope)


def _layer_body(sink_ref, x_ref, kvp_ref, kvc_ref, kvn_ref, kvm_ref, c_ref, sa_ref, sb_ref,
                gain_ref, fgain_ref, wr_ref, wpa_ref, wpb_ref, wout_ref, abias_ref, bbias_ref,
                out_ref, n_ref, win_ref, qa_ref, qb_ref, oa_ref, ob_ref, *, n_chunks, n_rows):
    i = pl.program_id(1)
    f32, bf16 = jnp.float32, jnp.bfloat16
    scale = HEAD_DIM ** -0.5

    x = x_ref[0]
    n_ref[...] = _rmsnorm(x, gain_ref[...]).astype(bf16)

    win_ref[0:CHUNK] = kvp_ref[0]
    win_ref[CHUNK:2 * CHUNK] = kvc_ref[0]
    win_ref[2 * CHUNK:3 * CHUNK] = kvn_ref[0]

    c, sa, sb = c_ref[...], sa_ref[...], sb_ref[...]
    qa = jnp.dot(n_ref[...], wr_ref[:, R_QA:R_QA + A_WIDTH], preferred_element_type=f32)
    for t in range(A_WIDTH // LANES):
        sl = slice(t * LANES, (t + 1) * LANES)
        qa_ref[:, sl] = (_rope_tile(qa[:, sl], c, sa, sb) * scale).astype(bf16)
    qb = jnp.dot(n_ref[...], wr_ref[:, R_QB:R_QB + B_WIDTH], preferred_element_type=f32)
    qb_ref[...] = (qb * scale).astype(bf16)

    lane = lax.broadcasted_iota(jnp.int32, (1, LANES), 1)
    lo_half = lane < HEAD_DIM
    zero_pad = jnp.zeros((META_PAD - N_META, LANES), bf16)

    def meta_tile(col):
        return jnp.concatenate([kvm_ref[0, :, col:col + LANES], zero_pad], axis=0)

    def split_heads(q):
        zeros = jnp.zeros_like(q)
        return jnp.concatenate([jnp.where(lo_half, q, zeros), jnp.where(lo_half, zeros, q)], axis=0)

    for g in range(A_KV_HEADS):
        kmeta = meta_tile(KV_KA + g * LANES)
        vmeta = meta_tile(KV_VA + g * LANES)
        for j in range(BLOCKS_PER_CHUNK):
            w0 = CHUNK + j * BLOCK - BLOCK
            kcat = jnp.concatenate(
                [kmeta, win_ref[w0:w0 + 3 * BLOCK, KV_KA + g * LANES:KV_KA + (g + 1) * LANES]], axis=0)
            vcat = jnp.concatenate(
                [vmeta, win_ref[w0:w0 + 3 * BLOCK, KV_VA + g * LANES:KV_VA + (g + 1) * LANES]], axis=0)
            rows = slice(j * BLOCK, (j + 1) * BLOCK)
            t0 = (A_GROUP // 2) * g
            qs = jnp.concatenate(
                [split_heads(qa_ref[rows, (t0 + t) * LANES:(t0 + t + 1) * LANES])
                 for t in range(A_GROUP // 2)], axis=0)
            s = lax.dot_general(qs, kcat, _NT, preferred_element_type=f32)
            first = jnp.logical_and(i == 0, j == 0)
            last = jnp.logical_and(i == n_chunks - 1, j == BLOCKS_PER_CHUNK - 1)
            variant = jnp.where(first, 1, jnp.where(last, 2, 0))
            ab = abias_ref[variant]
            ps, ls = [], []
            for hh in range(A_GROUP):
                sink = sink_ref[A_GROUP * g + hh]
                sh = s[hh * BLOCK:(hh + 1) * BLOCK] + ab
                m = jnp.maximum(jnp.max(sh, axis=-1, keepdims=True), sink)
                p = jnp.exp(sh - m)
                ls.append(jnp.sum(p, axis=-1, keepdims=True) + jnp.exp(sink - m))
                ps.append(p.astype(bf16))
            o = jnp.dot(jnp.concatenate(ps, axis=0), vcat, preferred_element_type=f32)
            os_ = [o[hh * BLOCK:(hh + 1) * BLOCK] / ls[hh] for hh in range(A_GROUP)]
            for t in range(A_GROUP // 2):
                oa_ref[rows, (t0 + t) * LANES:(t0 + t + 1) * LANES] = jnp.where(
                    lo_half, os_[2 * t], os_[2 * t + 1])

    meta_bias = jnp.where(lane < N_META, 0.0, NEG_INF).astype(f32)
    kmetas = [meta_tile(KV_KB + p * LANES) for p in range(B_HEADS // 2)]
    vmetas = [meta_tile(KV_VB + p * LANES) for p in range(B_HEADS // 2)]

    def row_step(rr, carry):
        r = i * ROWS_PER_CHUNK + rr
        r_start = jnp.clip(r - NA_KH // 2, 0, n_rows - NA_KH)
        shift = r - r_start
        woff = pl.multiple_of((r_start - (i - 1) * ROWS_PER_CHUNK) * GRID_W, GRID_W)
        q0 = pl.multiple_of(rr * GRID_W, GRID_W)
        for p in range(B_HEADS // 2):
            kcat = jnp.concatenate(
                [win_ref[pl.ds(woff, B_KEYS), KV_KB + p * LANES:KV_KB + (p + 1) * LANES], kmetas[p]],
                axis=0)
            vcat = jnp.concatenate(
                [win_ref[pl.ds(woff, B_KEYS), KV_VB + p * LANES:KV_VB + (p + 1) * LANES], vmetas[p]],
                axis=0)
            qs = split_heads(qb_ref[pl.ds(q0, GRID_W), p * LANES:(p + 1) * LANES])
            s = lax.dot_general(qs, kcat, _NT, preferred_element_type=f32)
            s_loc = s[:, :B_KEYS] + bbias_ref[shift, 2 * p:2 * p + 2].reshape(2 * GRID_W, B_KEYS)
            s_met = s[:, B_KEYS:] + meta_bias
            m = jnp.maximum(jnp.max(s_loc, axis=-1, keepdims=True),
                            jnp.max(s_met, axis=-1, keepdims=True))
            pr = jnp.concatenate([jnp.exp(s_loc - m), jnp.exp(s_met - m)], axis=1)
            l = jnp.sum(pr, axis=-1, keepdims=True)
            o = jnp.dot(pr.astype(bf16), vcat, preferred_element_type=f32) / l
            ob_ref[pl.ds(q0, GRID_W), p * LANES:(p + 1) * LANES] = jnp.where(
                lo_half, o[:GRID_W], o[GRID_W:])
        return carry

    lax.fori_loop(0, ROWS_PER_CHUNK, row_step, 0)

    def gated_branch(o_ref, z_col, w_ref, g_col):
        z = jnp.dot(n_ref[...], wr_ref[:, z_col:z_col + A_WIDTH], preferred_element_type=f32)
        a = (o_ref[...] * (z * jax.nn.sigmoid(z))).astype(bf16)
        y = jnp.dot(a, w_ref[...], preferred_element_type=f32)
        gate = jnp.dot(n_ref[...], wr_ref[:, g_col:g_col + D_MODEL], preferred_element_type=f32)
        return jax.nn.sigmoid(gate) * y

    merged = gated_branch(oa_ref, R_ZA, wpa_ref, R_GA) + gated_branch(ob_ref, R_ZB, wpb_ref, R_GB)
    h = x + jnp.dot(merged.astype(bf16), wout_ref[...], preferred_element_type=f32)
    out_ref[0] = _rmsnorm(h, fgain_ref[...])


def _layer_call(sink, x, kv, kv_meta, rope, gain, fgain, w_rest, w_pa, w_pb, w_out, abias, bbias):
    bsz, seq, _ = x.shape
    n_chunks = seq // CHUNK
    const2 = lambda b, i: (0, 0)
    const3 = lambda b, i: (0, 0, 0)
    const4 = lambda b, i: (0, 0, 0, 0)
    resident = dict(pipeline_mode=pl.Buffered(1))
    body = functools.partial(_layer_body, n_chunks=n_chunks, n_rows=seq // GRID_W)
    return pl.pallas_call(
        body,
        grid=(bsz, n_chunks),
        in_specs=[
            pl.BlockSpec(memory_space=pltpu.SMEM),
            pl.BlockSpec((1, CHUNK, D_MODEL), lambda b, i: (b, i, 0)),
            pl.BlockSpec((1, CHUNK, KV_COLS), lambda b, i: (b, jnp.maximum(i - 1, 0), 0)),
            pl.BlockSpec((1, CHUNK, KV_COLS), lambda b, i: (b, i, 0)),
            pl.BlockSpec((1, CHUNK, KV_COLS), lambda b, i: (b, jnp.minimum(i + 1, n_chunks - 1), 0)),
            pl.BlockSpec((1, N_META, KV_COLS), const3, **resident),
            pl.BlockSpec((CHUNK, LANES), lambda b, i: (i, 0)),
            pl.BlockSpec((CHUNK, LANES), lambda b, i: (i, 0)),
            pl.BlockSpec((CHUNK, LANES), lambda b, i: (i, 0)),
            pl.BlockSpec((1, D_MODEL), const2, **resident),
            pl.BlockSpec((1, D_MODEL), const2, **resident),
            pl.BlockSpec((D_MODEL, R_COLS), const2, **resident),
            pl.BlockSpec((A_WIDTH, D_MODEL), const2, **resident),
            pl.BlockSpec((B_WIDTH, D_MODEL), const2, **resident),
            pl.BlockSpec((D_MODEL, D_MODEL), const2, **resident),
            pl.BlockSpec((3, BLOCK, META_PAD + 3 * BLOCK), const3, **resident),
            pl.BlockSpec((NA_KH, B_HEADS, GRID_W, B_KEYS), const4, **resident),
        ],
        out_specs=pl.BlockSpec((1, CHUNK, D_MODEL), lambda b, i: (b, i, 0)),
        out_shape=jax.ShapeDtypeStruct((bsz, seq, D_MODEL), jnp.float32),
        scratch_shapes=[
            pltpu.VMEM((CHUNK, D_MODEL), jnp.bfloat16),
            pltpu.VMEM((3 * CHUNK, KV_COLS), jnp.bfloat16),
            pltpu.VMEM((CHUNK, A_WIDTH), jnp.bfloat16),
            pltpu.VMEM((CHUNK, B_WIDTH), jnp.bfloat16),
            pltpu.VMEM((CHUNK, A_WIDTH), jnp.float32),
            pltpu.VMEM((CHUNK, B_WIDTH), jnp.float32),
        ],
        compiler_params=pltpu.CompilerParams(
            dimension_semantics=("parallel", "arbitrary"), vmem_limit_bytes=VMEM_LIMIT_BYTES),
        name="hybrid_layer",
    )(sink, x, kv, kv, kv, kv_meta, *rope, gain, fgain, w_rest, w_pa, w_pb, w_out, abias, bbias)


def _rope_tables(pos):
    inv_freq = ROPE_THETA ** (-jnp.arange(ROT_HALF, dtype=jnp.float32) / ROT_HALF)
    ang = pos[:, None] * inv_freq[None, :]
    cos, sin = jnp.cos(ang), jnp.sin(ang)
    ones = jnp.ones((pos.shape[0], HEAD_DIM - ROT_DIM), jnp.float32)
    zeros = jnp.zeros_like(ones)
    zh = jnp.zeros_like(sin)
    c = jnp.concatenate([cos, cos, ones], axis=1)
    sa = jnp.concatenate([-sin, zh, zeros], axis=1)
    sb = jnp.concatenate([zh, sin, zeros], axis=1)
    return tuple(jnp.tile(t, (1, LANES // HEAD_DIM)) for t in (c, sa, sb))


def _band_bias():
    qi = np.arange(BLOCK)[:, None]
    col = np.arange(META_PAD + 3 * BLOCK)[None, :]
    kj = col - META_PAD
    band = (col >= META_PAD) & (np.abs(kj - BLOCK - qi) <= WINDOW)
    meta = np.broadcast_to(col < N_META, band.shape)
    variants = [meta | band, meta | (band & (kj >= BLOCK)), meta | (band & (kj < 2 * BLOCK))]
    return jnp.asarray(np.where(np.stack(variants), 0.0, NEG_INF).astype(np.float32))


def _na_bias(rpb):
    per_col = []
    for cq in range(GRID_W):
        start = min(max(cq - NA_KW // 2, 0), GRID_W - NA_KW)
        lo = start - cq + NA_KW - 1
        per_col.append(jnp.pad(rpb[:, :, lo:lo + NA_KW], ((0, 0), (0, 0), (start, GRID_W - NA_KW - start)),
                               constant_values=NEG_INF))
    t15 = jnp.stack(per_col, axis=1)
    return jnp.stack(
        [t15[:, :, NA_KH - 1 - k:2 * NA_KH - 1 - k, :].reshape(B_HEADS, GRID_W, B_KEYS)
         for k in range(NA_KH)], axis=0)


def kernel(x, meta_tokens, norm_gain, w_in, sink_logits, rel_pos_bias, w_proj_a, w_proj_b, w_out,
           final_norm_gain):
    bsz, seq, _ = x.shape
    assert seq % CHUNK == 0 and seq // GRID_W >= NA_KH and norm_gain.shape[0] == 1
    bf16 = jnp.bfloat16
    w = w_in[0]
    qa, ka, va, za, qb, kb, vb, zb, ga, gb = jnp.split(w, list(np.cumsum(SPLIT_SIZES)[:-1]), axis=1)
    dup = lambda t: [t[:, h * HEAD_DIM:(h + 1) * HEAD_DIM] for h in range(A_KV_HEADS) for _ in range(2)]
    w_kv = jnp.concatenate(dup(ka) + dup(va) + [kb, vb], axis=1).astype(bf16)
    w_rest = jnp.concatenate([qa, qb, za, zb, ga, gb], axis=1).astype(bf16)

    pos = jnp.arange(N_META + seq, dtype=jnp.float32)
    rope_meta = _rope_tables(pos[:N_META])
    rope_real = _rope_tables(pos[N_META:])

    gain = norm_gain[0][None]
    kv_meta = _kv_call(meta_tokens[None], gain, w_kv, rope_meta, N_META)
    kv = _kv_call(x, gain, w_kv, rope_real, CHUNK)
    return _layer_call(sink_logits[0], x, kv, kv_meta, rope_real, gain, final_norm_gain[None],
                       w_rest, w_proj_a[0].astype(bf16), w_proj_b[0].astype(bf16),
                       w_out[0].astype(bf16), _band_bias(), _na_bias(rel_pos_bias[0]))
```

```python
import functools

import numpy as np
import jax
import jax.numpy as jnp
from jax import lax
from jax.experimental import pallas as pl
from jax.experimental.pallas import tpu as pltpu

D_MODEL = 1024
N_META = 16
HEAD_DIM = 64
RMS_EPS = 1e-6
NEG_INF = -1e30
A_HEADS = 8
A_KV_HEADS = 2
A_GROUP = A_HEADS // A_KV_HEADS
A_WIDTH = A_HEADS * HEAD_DIM
A_KV_WIDTH = A_KV_HEADS * HEAD_DIM
WINDOW = 128
BLOCK = 128
ROT_DIM = HEAD_DIM // 4
ROT_HALF = ROT_DIM // 2
ROPE_THETA = 500000.0
B_HEADS = 8
B_WIDTH = B_HEADS * HEAD_DIM
GRID_W = 64
NA_KH = 8
NA_KW = 16
SPLIT_SIZES = (A_WIDTH, A_KV_WIDTH, A_KV_WIDTH, A_WIDTH, B_WIDTH, B_WIDTH, B_WIDTH, B_WIDTH,
               D_MODEL, D_MODEL)

LANES = 128
CHUNK = 512
ROWS_PER_CHUNK = CHUNK // GRID_W
BLOCKS_PER_CHUNK = CHUNK // BLOCK
HALO = CHUNK // 2
assert HALO >= BLOCK and HALO >= (NA_KH // 2) * GRID_W
META_PAD = LANES
B_KEYS = NA_KH * GRID_W
B_ROWS_PER_STEP = 2

KV_KA = 0
KV_VA = KV_KA + A_KV_HEADS * LANES
KV_KB = KV_VA + A_KV_HEADS * LANES
KV_VB = KV_KB + B_WIDTH
KV_COLS = KV_VB + B_WIDTH

R_QA = 0
R_QB = R_QA + A_WIDTH
R_ZA = R_QB + B_WIDTH
R_ZB = R_ZA + A_WIDTH
R_GA = R_ZB + B_WIDTH
R_GB = R_GA + D_MODEL
R_COLS = R_GB + D_MODEL

VMEM_LIMIT_BYTES = 58 * 1024 * 1024

_NT = (((1,), (1,)), ((), ()))


def _rmsnorm(x, gain):
    return x * lax.rsqrt(jnp.mean(x * x, axis=-1, keepdims=True) + RMS_EPS) * gain


def _rope_tile(t, c, sa, sb):
    return t * c + pltpu.roll(t, LANES - ROT_HALF, 1) * sa + pltpu.roll(t, ROT_HALF, 1) * sb


def _kv_body(x_ref, gain_ref, w_ref, c_ref, sa_ref, sb_ref, kv_ref):
    n = _rmsnorm(x_ref[0], gain_ref[...]).astype(jnp.bfloat16)
    kv = jnp.dot(n, w_ref[...], preferred_element_type=jnp.float32)
    c, sa, sb = c_ref[...], sa_ref[...], sb_ref[...]
    for t in range(A_KV_HEADS):
        lo = KV_KA + t * LANES
        kv_ref[0, :, lo:lo + LANES] = _rope_tile(kv[:, lo:lo + LANES], c, sa, sb).astype(jnp.bfloat16)
    kv_ref[0, :, KV_VA:] = kv[:, KV_VA:].astype(jnp.bfloat16)


def _kv_call(x, gain, w_kv, rope, rows):
    bsz, seq, _ = x.shape
    const = lambda b, i: (0, 0)
    return pl.pallas_call(
        _kv_body,
        grid=(bsz, seq // rows),
        in_specs=[
            pl.BlockSpec((1, rows, D_MODEL), lambda b, i: (b, i, 0)),
            pl.BlockSpec((1, D_MODEL), const),
            pl.BlockSpec((D_MODEL, KV_COLS), const, pipeline_mode=pl.Buffered(1)),
            pl.BlockSpec((rows, LANES), lambda b, i: (i, 0)),
            pl.BlockSpec((rows, LANES), lambda b, i: (i, 0)),
            pl.BlockSpec((rows, LANES), lambda b, i: (i, 0)),
        ],
        out_specs=pl.BlockSpec((1, rows, KV_COLS), lambda b, i: (b, i, 0)),
        out_shape=jax.ShapeDtypeStruct((bsz, seq, KV_COLS), jnp.bfloat16),
        compiler_params=pltpu.CompilerParams(
            dimension_semantics=("parallel", "parallel"), vmem_limit_bytes=VMEM_LIMIT_BYTES),
        name="kv_proj",
    )(x, gain, w_kv, *rope)


def _layer_body(sink_ref, x_ref, kvp_ref, kvc_ref, kvn_ref, kvm_ref, c_ref, sa_ref, sb_ref,
                gain_ref, fgain_ref, wr_ref, wpa_ref, wpb_ref, wout_ref, abias_ref, bbias_ref,
                out_ref, n_ref, win_ref, qa_ref, qb_ref, oa_ref, ob_ref, *, n_chunks, n_rows):
    i = pl.program_id(1)
    f32, bf16 = jnp.float32, jnp.bfloat16
    scale = HEAD_DIM ** -0.5

    x = x_ref[0]
    n_ref[...] = _rmsnorm(x, gain_ref[...]).astype(bf16)

    win_ref[0:HALO] = kvp_ref[0]
    win_ref[HALO:HALO + CHUNK] = kvc_ref[0]
    win_ref[HALO + CHUNK:CHUNK + 2 * HALO] = kvn_ref[0]

    c, sa, sb = c_ref[...], sa_ref[...], sb_ref[...]
    qa = jnp.dot(n_ref[...], wr_ref[:, R_QA:R_QA + A_WIDTH], preferred_element_type=f32)
    for t in range(A_WIDTH // LANES):
        sl = slice(t * LANES, (t + 1) * LANES)
        qa_ref[:, sl] = (_rope_tile(qa[:, sl], c, sa, sb) * scale).astype(bf16)
    qb = jnp.dot(n_ref[...], wr_ref[:, R_QB:R_QB + B_WIDTH], preferred_element_type=f32)
    qb_ref[...] = (qb * scale).astype(bf16)

    lane = lax.broadcasted_iota(jnp.int32, (1, LANES), 1)
    lo_half = lane < HEAD_DIM
    zero_pad = jnp.zeros((META_PAD - N_META, LANES), bf16)

    def meta_tile(col):
        return jnp.concatenate([kvm_ref[0, :, col:col + LANES], zero_pad], axis=0)

    def split_heads(q):
        zeros = jnp.zeros_like(q)
        return jnp.concatenate([jnp.where(lo_half, q, zeros), jnp.where(lo_half, zeros, q)], axis=0)

    def pipelined(items, scores, finish):
        pending = scores(items[0])
        for t, item in enumerate(items):
            nxt = scores(items[t + 1]) if t + 1 < len(items) else None
            finish(item, pending)
            pending = nxt

    kmetas_a = [meta_tile(KV_KA + g * LANES) for g in range(A_KV_HEADS)]
    vmetas_a = [meta_tile(KV_VA + g * LANES) for g in range(A_KV_HEADS)]

    def a_window(col, g, j):
        w0 = HALO + j * BLOCK - BLOCK
        return win_ref[w0:w0 + 3 * BLOCK, col + g * LANES:col + (g + 1) * LANES]

    def a_scores(item):
        g, j = item
        kcat = jnp.concatenate([kmetas_a[g], a_window(KV_KA, g, j)], axis=0)
        t0 = (A_GROUP // 2) * g
        qs = jnp.concatenate(
            [split_heads(qa_ref[j * BLOCK:(j + 1) * BLOCK, (t0 + t) * LANES:(t0 + t + 1) * LANES])
             for t in range(A_GROUP // 2)], axis=0)
        return lax.dot_general(qs, kcat, _NT, preferred_element_type=f32)

    def a_finish(item, s):
        g, j = item
        vcat = jnp.concatenate([vmetas_a[g], a_window(KV_VA, g, j)], axis=0)
        first = jnp.logical_and(i == 0, j == 0)
        last = jnp.logical_and(i == n_chunks - 1, j == BLOCKS_PER_CHUNK - 1)
        ab = abias_ref[jnp.where(first, 1, jnp.where(last, 2, 0))]
        ps, ls = [], []
        for hh in range(A_GROUP):
            sink = sink_ref[A_GROUP * g + hh]
            sh = s[hh * BLOCK:(hh + 1) * BLOCK] + ab
            m = jnp.maximum(jnp.max(sh, axis=-1, keepdims=True), sink)
            p = jnp.exp(sh - m)
            ls.append(jnp.sum(p, axis=-1, keepdims=True) + jnp.exp(sink - m))
            ps.append(p.astype(bf16))
        o = jnp.dot(jnp.concatenate(ps, axis=0), vcat, preferred_element_type=f32)
        os_ = [o[hh * BLOCK:(hh + 1) * BLOCK] / ls[hh] for hh in range(A_GROUP)]
        t0 = (A_GROUP // 2) * g
        for t in range(A_GROUP // 2):
            oa_ref[j * BLOCK:(j + 1) * BLOCK, (t0 + t) * LANES:(t0 + t + 1) * LANES] = jnp.where(
                lo_half, os_[2 * t], os_[2 * t + 1])

    pipelined([(g, j) for g in range(A_KV_HEADS) for j in range(BLOCKS_PER_CHUNK)],
              a_scores, a_finish)

    meta_bias = jnp.where(lane < N_META, 0.0, NEG_INF).astype(f32)
    kmetas = [meta_tile(KV_KB + p * LANES) for p in range(B_HEADS // 2)]
    vmetas = [meta_tile(KV_VB + p * LANES) for p in range(B_HEADS // 2)]

    def row_step(step, carry):
        def geometry(u):
            rr = step * B_ROWS_PER_STEP + u
            r = i * ROWS_PER_CHUNK + rr
            r_start = jnp.clip(r - NA_KH // 2, 0, n_rows - NA_KH)
            shift = r - r_start
            woff = pl.multiple_of((r_start - i * ROWS_PER_CHUNK) * GRID_W + HALO, GRID_W)
            return pl.multiple_of(rr * GRID_W, GRID_W), woff, shift

        geo = [geometry(u) for u in range(B_ROWS_PER_STEP)]

        def b_scores(item):
            u, p = item
            q0, woff, _ = geo[u]
            kcat = jnp.concatenate(
                [win_ref[pl.ds(woff, B_KEYS), KV_KB + p * LANES:KV_KB + (p + 1) * LANES], kmetas[p]],
                axis=0)
            qs = split_heads(qb_ref[pl.ds(q0, GRID_W), p * LANES:(p + 1) * LANES])
            return lax.dot_general(qs, kcat, _NT, preferred_element_type=f32)

        def b_finish(item, s):
            u, p = item
            q0, woff, shift = geo[u]
            vcat = jnp.concatenate(
                [win_ref[pl.ds(woff, B_KEYS), KV_VB + p * LANES:KV_VB + (p + 1) * LANES], vmetas[p]],
                axis=0)
            s_loc = s[:, :B_KEYS] + bbias_ref[shift, 2 * p:2 * p + 2].reshape(2 * GRID_W, B_KEYS)
            s_met = s[:, B_KEYS:] + meta_bias
            m = jnp.maximum(jnp.max(s_loc, axis=-1, keepdims=True),
                            jnp.max(s_met, axis=-1, keepdims=True))
            pr = jnp.concatenate([jnp.exp(s_loc - m), jnp.exp(s_met - m)], axis=1)
            l = jnp.sum(pr, axis=-1, keepdims=True)
            o = jnp.dot(pr.astype(bf16), vcat, preferred_element_type=f32) / l
            ob_ref[pl.ds(q0, GRID_W), p * LANES:(p + 1) * LANES] = jnp.where(
                lo_half, o[:GRID_W], o[GRID_W:])

        pipelined([(u, p) for u in range(B_ROWS_PER_STEP) for p in range(B_HEADS // 2)],
                  b_scores, b_finish)
        return carry

    lax.fori_loop(0, ROWS_PER_CHUNK // B_ROWS_PER_STEP, row_step, 0)

    def gated_branch(o_ref, z_col, w_ref, g_col):
        z = jnp.dot(n_ref[...], wr_ref[:, z_col:z_col + A_WIDTH], preferred_element_type=f32)
        a = (o_ref[...] * (z * jax.nn.sigmoid(z))).astype(bf16)
        y = jnp.dot(a, w_ref[...], preferred_element_type=f32)
        gate = jnp.dot(n_ref[...], wr_ref[:, g_col:g_col + D_MODEL], preferred_element_type=f32)
        return jax.nn.sigmoid(gate) * y

    merged = gated_branch(oa_ref, R_ZA, wpa_ref, R_GA) + gated_branch(ob_ref, R_ZB, wpb_ref, R_GB)
    h = x + jnp.dot(merged.astype(bf16), wout_ref[...], preferred_element_type=f32)
    out_ref[0] = _rmsnorm(h, fgain_ref[...])


def _layer_call(sink, x, kv, kv_meta, rope, gain, fgain, w_rest, w_pa, w_pb, w_out, abias, bbias):
    bsz, seq, _ = x.shape
    n_chunks = seq // CHUNK
    halo_per = CHUNK // HALO
    const2 = lambda b, i: (0, 0)
    const3 = lambda b, i: (0, 0, 0)
    const4 = lambda b, i: (0, 0, 0, 0)
    resident = dict(pipeline_mode=pl.Buffered(1))
    body = functools.partial(_layer_body, n_chunks=n_chunks, n_rows=seq // GRID_W)
    return pl.pallas_call(
        body,
        grid=(bsz, n_chunks),
        in_specs=[
            pl.BlockSpec(memory_space=pltpu.SMEM),
            pl.BlockSpec((1, CHUNK, D_MODEL), lambda b, i: (b, i, 0)),
            pl.BlockSpec((1, HALO, KV_COLS), lambda b, i: (b, jnp.maximum(i * halo_per - 1, 0), 0)),
            pl.BlockSpec((1, CHUNK, KV_COLS), lambda b, i: (b, i, 0)),
            pl.BlockSpec((1, HALO, KV_COLS),
                         lambda b, i: (b, jnp.minimum((i + 1) * halo_per, n_chunks * halo_per - 1), 0)),
            pl.BlockSpec((1, N_META, KV_COLS), const3, **resident),
            pl.BlockSpec((CHUNK, LANES), lambda b, i: (i, 0)),
            pl.BlockSpec((CHUNK, LANES), lambda b, i: (i, 0)),
            pl.BlockSpec((CHUNK, LANES), lambda b, i: (i, 0)),
            pl.BlockSpec((1, D_MODEL), const2, **resident),
            pl.BlockSpec((1, D_MODEL), const2, **resident),
            pl.BlockSpec((D_MODEL, R_COLS), const2, **resident),
            pl.BlockSpec((A_WIDTH, D_MODEL), const2, **resident),
            pl.BlockSpec((B_WIDTH, D_MODEL), const2, **resident),
            pl.BlockSpec((D_MODEL, D_MODEL), const2, **resident),
            pl.BlockSpec((3, BLOCK, META_PAD + 3 * BLOCK), const3, **resident),
            pl.BlockSpec((NA_KH, B_HEADS, GRID_W, B_KEYS), const4, **resident),
        ],
        out_specs=pl.BlockSpec((1, CHUNK, D_MODEL), lambda b, i: (b, i, 0)),
        out_shape=jax.ShapeDtypeStruct((bsz, seq, D_MODEL), jnp.float32),
        scratch_shapes=[
            pltpu.VMEM((CHUNK, D_MODEL), jnp.bfloat16),
            pltpu.VMEM((CHUNK + 2 * HALO, KV_COLS), jnp.bfloat16),
            pltpu.VMEM((CHUNK, A_WIDTH), jnp.bfloat16),
            pltpu.VMEM((CHUNK, B_WIDTH), jnp.bfloat16),
            pltpu.VMEM((CHUNK, A_WIDTH), jnp.float32),
            pltpu.VMEM((CHUNK, B_WIDTH), jnp.float32),
        ],
        compiler_params=pltpu.CompilerParams(
            dimension_semantics=("parallel", "arbitrary"), vmem_limit_bytes=VMEM_LIMIT_BYTES),
        name="hybrid_layer",
    )(sink, x, kv, kv, kv, kv_meta, *rope, gain, fgain, w_rest, w_pa, w_pb, w_out, abias, bbias)


def _rope_tables(pos):
    inv_freq = ROPE_THETA ** (-jnp.arange(ROT_HALF, dtype=jnp.float32) / ROT_HALF)
    ang = pos[:, None] * inv_freq[None, :]
    cos, sin = jnp.cos(ang), jnp.sin(ang)
    ones = jnp.ones((pos.shape[0], HEAD_DIM - ROT_DIM), jnp.float32)
    zeros = jnp.zeros_like(ones)
    zh = jnp.zeros_like(sin)
    c = jnp.concatenate([cos, cos, ones], axis=1)
    sa = jnp.concatenate([-sin, zh, zeros], axis=1)
    sb = jnp.concatenate([zh, sin, zeros], axis=1)
    return tuple(jnp.tile(t, (1, LANES // HEAD_DIM)) for t in (c, sa, sb))


def _band_bias():
    qi = np.arange(BLOCK)[:, None]
    col = np.arange(META_PAD + 3 * BLOCK)[None, :]
    kj = col - META_PAD
    band = (col >= META_PAD) & (np.abs(kj - BLOCK - qi) <= WINDOW)
    meta = np.broadcast_to(col < N_META, band.shape)
    variants = [meta | band, meta | (band & (kj >= BLOCK)), meta | (band & (kj < 2 * BLOCK))]
    return jnp.asarray(np.where(np.stack(variants), 0.0, NEG_INF).astype(np.float32))


def _na_bias(rpb):
    per_col = []
    for cq in range(GRID_W):
        start = min(max(cq - NA_KW // 2, 0), GRID_W - NA_KW)
        lo = start - cq + NA_KW - 1
        per_col.append(jnp.pad(rpb[:, :, lo:lo + NA_KW], ((0, 0), (0, 0), (start, GRID_W - NA_KW - start)),
                               constant_values=NEG_INF))
    t15 = jnp.stack(per_col, axis=1)
    return jnp.stack(
        [t15[:, :, NA_KH - 1 - k:2 * NA_KH - 1 - k, :].reshape(B_HEADS, GRID_W, B_KEYS)
         for k in range(NA_KH)], axis=0)


def kernel(x, meta_tokens, norm_gain, w_in, sink_logits, rel_pos_bias, w_proj_a, w_proj_b, w_out,
           final_norm_gain):
    bsz, seq, _ = x.shape
    assert seq % CHUNK == 0 and seq // GRID_W >= NA_KH and norm_gain.shape[0] == 1
    bf16 = jnp.bfloat16
    w = w_in[0]
    qa, ka, va, za, qb, kb, vb, zb, ga, gb = jnp.split(w, list(np.cumsum(SPLIT_SIZES)[:-1]), axis=1)
    dup = lambda t: [t[:, h * HEAD_DIM:(h + 1) * HEAD_DIM] for h in range(A_KV_HEADS) for _ in range(2)]
    w_kv = jnp.concatenate(dup(ka) + dup(va) + [kb, vb], axis=1).astype(bf16)
    w_rest = jnp.concatenate([qa, qb, za, zb, ga, gb], axis=1).astype(bf16)

    pos = jnp.arange(N_META + seq, dtype=jnp.float32)
    rope_meta = _rope_tables(pos[:N_META])
    rope_real = _rope_tables(pos[N_META:])

    gain = norm_gain[0][None]
    kv_meta = _kv_call(meta_tokens[None], gain, w_kv, rope_meta, N_META)
    kv = _kv_call(x, gain, w_kv, rope_real, CHUNK)
    return _layer_call(sink_logits[0], x, kv, kv_meta, rope_real, gain, final_norm_gain[None],
                       w_rest, w_proj_a[0].astype(bf16), w_proj_b[0].astype(bf16),
                       w_out[0].astype(bf16), _band_bias(), _na_bias(rel_pos_bias[0]))
```

```python
import functools

import numpy as np
import jax
import jax.numpy as jnp
from jax import lax
from jax.experimental import pallas as pl
from jax.experimental.pallas import tpu as pltpu

D_MODEL = 1024
N_META = 16
HEAD_DIM = 64
RMS_EPS = 1e-6
NEG_INF = -1e30
A_HEADS = 8
A_KV_HEADS = 2
A_GROUP = A_HEADS // A_KV_HEADS
A_WIDTH = A_HEADS * HEAD_DIM
A_KV_WIDTH = A_KV_HEADS * HEAD_DIM
WINDOW = 128
BLOCK = 128
ROT_DIM = HEAD_DIM // 4
ROT_HALF = ROT_DIM // 2
ROPE_THETA = 500000.0
B_HEADS = 8
B_WIDTH = B_HEADS * HEAD_DIM
GRID_W = 64
NA_KH = 8
NA_KW = 16
SPLIT_SIZES = (A_WIDTH, A_KV_WIDTH, A_KV_WIDTH, A_WIDTH, B_WIDTH, B_WIDTH, B_WIDTH, B_WIDTH,
               D_MODEL, D_MODEL)

LANES = 128
CHUNK = 512
ROWS_PER_CHUNK = CHUNK // GRID_W
BLOCKS_PER_CHUNK = CHUNK // BLOCK
HALO = CHUNK // 2
assert HALO >= BLOCK and HALO >= (NA_KH // 2) * GRID_W
META_PAD = LANES
B_KEYS = NA_KH * GRID_W
B_ROWS_PER_STEP = 2

KV_KA = 0
KV_VA = KV_KA + A_KV_HEADS * LANES
KV_KB = KV_VA + A_KV_HEADS * LANES
KV_VB = KV_KB + B_WIDTH
KV_COLS = KV_VB + B_WIDTH

W_KA = 0
W_VA = W_KA + A_KV_WIDTH
W_KB = W_VA + A_KV_WIDTH
W_KV_COLS = W_KB + 2 * B_WIDTH
assert A_KV_WIDTH == LANES

R_QA = 0
R_QB = R_QA + A_WIDTH
R_ZA = R_QB + B_WIDTH
R_ZB = R_ZA + A_WIDTH
R_GA = R_ZB + B_WIDTH
R_GB = R_GA + D_MODEL
R_COLS = R_GB + D_MODEL

VMEM_LIMIT_BYTES = 58 * 1024 * 1024

_NT = (((1,), (1,)), ((), ()))


def _rmsnorm(x, gain):
    return x * lax.rsqrt(jnp.mean(x * x, axis=-1, keepdims=True) + RMS_EPS) * gain


def _lane_patterns():
    lane = lax.broadcasted_iota(jnp.int32, (1, LANES), 1)
    return lane < HEAD_DIM, (lane & (HEAD_DIM - 1)) < ROT_HALF


def _rope_coeffs(cb_ref, sb_ref, off_ref, first_half):
    cb, sb = cb_ref[...], sb_ref[...]
    ca, sa = off_ref[0:1, :], off_ref[1:2, :]
    c = cb * ca - sb * sa
    s = (sb * ca + cb * sa) * jnp.where(first_half, -1.0, 1.0)
    return c, s


def _rope_tile(t, c, s, first_half):
    partner = jnp.where(first_half, pltpu.roll(t, LANES - ROT_HALF, 1), pltpu.roll(t, ROT_HALF, 1))
    return t * c + partner * s


def _kv_body(x_ref, gain_ref, w_ref, cb_ref, sb_ref, off_ref, kv_ref):
    bf16 = jnp.bfloat16
    lo_half, first_half = _lane_patterns()
    n = _rmsnorm(x_ref[0], gain_ref[...]).astype(bf16)
    kv = jnp.dot(n, w_ref[...], preferred_element_type=jnp.float32)
    c, s = _rope_coeffs(cb_ref, sb_ref, off_ref, first_half)
    ka = _rope_tile(kv[:, W_KA:W_KA + LANES], c, s, first_half)
    va = kv[:, W_VA:W_VA + LANES]
    for src, col in ((ka, KV_KA), (va, KV_VA)):
        swapped = pltpu.roll(src, HEAD_DIM, 1)
        kv_ref[0, :, col:col + LANES] = jnp.where(lo_half, src, swapped).astype(bf16)
        kv_ref[0, :, col + LANES:col + 2 * LANES] = jnp.where(lo_half, swapped, src).astype(bf16)
    kv_ref[0, :, KV_KB:] = kv[:, W_KB:].astype(bf16)


def _kv_call(x, gain, w_kv, rope_base, rope_off, rows):
    bsz, seq, _ = x.shape
    const = lambda b, i: (0, 0)
    return pl.pallas_call(
        _kv_body,
        grid=(bsz, seq // rows),
        in_specs=[
            pl.BlockSpec((1, rows, D_MODEL), lambda b, i: (b, i, 0)),
            pl.BlockSpec((1, D_MODEL), const),
            pl.BlockSpec((D_MODEL, W_KV_COLS), const, pipeline_mode=pl.Buffered(1)),
            pl.BlockSpec((rows, LANES), const),
            pl.BlockSpec((rows, LANES), const),
            pl.BlockSpec((None, 2, LANES), lambda b, i: (i, 0, 0)),
        ],
        out_specs=pl.BlockSpec((1, rows, KV_COLS), lambda b, i: (b, i, 0)),
        out_shape=jax.ShapeDtypeStruct((bsz, seq, KV_COLS), jnp.bfloat16),
        compiler_params=pltpu.CompilerParams(
            dimension_semantics=("parallel", "parallel"), vmem_limit_bytes=VMEM_LIMIT_BYTES),
        name="kv_proj",
    )(x, gain, w_kv, *rope_base, rope_off)


def _layer_body(sink_ref, x_ref, kvp_ref, kvc_ref, kvn_ref, kvm_ref, cb_ref, sb_ref, off_ref,
                gain_ref, fgain_ref, wr_ref, wpa_ref, wpb_ref, wout_ref, abias_ref, bbias_ref,
                out_ref, n_ref, win_ref, qa_ref, qb_ref, oa_ref, ob_ref, *, n_chunks, n_rows):
    i = pl.program_id(1)
    f32, bf16 = jnp.float32, jnp.bfloat16
    scale = HEAD_DIM ** -0.5

    x = x_ref[0]
    n_ref[...] = _rmsnorm(x, gain_ref[...]).astype(bf16)

    win_ref[0:HALO] = kvp_ref[0]
    win_ref[HALO:HALO + CHUNK] = kvc_ref[0]
    win_ref[HALO + CHUNK:CHUNK + 2 * HALO] = kvn_ref[0]

    lo_half, first_half = _lane_patterns()
    c, s = _rope_coeffs(cb_ref, sb_ref, off_ref, first_half)
    qa = jnp.dot(n_ref[...], wr_ref[:, R_QA:R_QA + A_WIDTH], preferred_element_type=f32)
    for t in range(A_WIDTH // LANES):
        sl = slice(t * LANES, (t + 1) * LANES)
        qa_ref[:, sl] = (_rope_tile(qa[:, sl], c, s, first_half) * scale).astype(bf16)
    qb = jnp.dot(n_ref[...], wr_ref[:, R_QB:R_QB + B_WIDTH], preferred_element_type=f32)
    qb_ref[...] = (qb * scale).astype(bf16)

    lane = lax.broadcasted_iota(jnp.int32, (1, LANES), 1)
    zero_pad = jnp.zeros((META_PAD - N_META, LANES), bf16)

    def meta_tile(col):
        return jnp.concatenate([kvm_ref[0, :, col:col + LANES], zero_pad], axis=0)

    def split_heads(q):
        zeros = jnp.zeros_like(q)
        return jnp.concatenate([jnp.where(lo_half, q, zeros), jnp.where(lo_half, zeros, q)], axis=0)

    def pipelined(items, scores, finish):
        pending = scores(items[0])
        for t, item in enumerate(items):
            nxt = scores(items[t + 1]) if t + 1 < len(items) else None
            finish(item, pending)
            pending = nxt

    kmetas_a = [meta_tile(KV_KA + g * LANES) for g in range(A_KV_HEADS)]
    vmetas_a = [meta_tile(KV_VA + g * LANES) for g in range(A_KV_HEADS)]

    def a_window(col, g, j):
        w0 = HALO + j * BLOCK - BLOCK
        return win_ref[w0:w0 + 3 * BLOCK, col + g * LANES:col + (g + 1) * LANES]

    def a_scores(item):
        g, j = item
        kcat = jnp.concatenate([kmetas_a[g], a_window(KV_KA, g, j)], axis=0)
        t0 = (A_GROUP // 2) * g
        qs = jnp.concatenate(
            [split_heads(qa_ref[j * BLOCK:(j + 1) * BLOCK, (t0 + t) * LANES:(t0 + t + 1) * LANES])
             for t in range(A_GROUP // 2)], axis=0)
        return lax.dot_general(qs, kcat, _NT, preferred_element_type=f32)

    def a_finish(item, s):
        g, j = item
        vcat = jnp.concatenate([vmetas_a[g], a_window(KV_VA, g, j)], axis=0)
        first = jnp.logical_and(i == 0, j == 0)
        last = jnp.logical_and(i == n_chunks - 1, j == BLOCKS_PER_CHUNK - 1)
        ab = abias_ref[jnp.where(first, 1, jnp.where(last, 2, 0))]
        ps, ls = [], []
        for hh in range(A_GROUP):
            sink = sink_ref[A_GROUP * g + hh]
            sh = s[hh * BLOCK:(hh + 1) * BLOCK] + ab
            m = jnp.maximum(jnp.max(sh, axis=-1, keepdims=True), sink)
            p = jnp.exp(sh - m)
            ls.append(jnp.sum(p, axis=-1, keepdims=True) + jnp.exp(sink - m))
            ps.append(p.astype(bf16))
        o = jnp.dot(jnp.concatenate(ps, axis=0), vcat, preferred_element_type=f32)
        os_ = [o[hh * BLOCK:(hh + 1) * BLOCK] / ls[hh] for hh in range(A_GROUP)]
        t0 = (A_GROUP // 2) * g
        for t in range(A_GROUP // 2):
            oa_ref[j * BLOCK:(j + 1) * BLOCK, (t0 + t) * LANES:(t0 + t + 1) * LANES] = jnp.where(
                lo_half, os_[2 * t], os_[2 * t + 1])

    pipelined([(g, j) for g in range(A_KV_HEADS) for j in range(BLOCKS_PER_CHUNK)],
              a_scores, a_finish)

    meta_bias = jnp.where(lane < N_META, 0.0, NEG_INF).astype(f32)
    kmetas = [meta_tile(KV_KB + p * LANES) for p in range(B_HEADS // 2)]
    vmetas = [meta_tile(KV_VB + p * LANES) for p in range(B_HEADS // 2)]

    def row_step(step, carry):
        def geometry(u):
            rr = step * B_ROWS_PER_STEP + u
            r = i * ROWS_PER_CHUNK + rr
            r_start = jnp.clip(r - NA_KH // 2, 0, n_rows - NA_KH)
            shift = r - r_start
            woff = pl.multiple_of((r_start - i * ROWS_PER_CHUNK) * GRID_W + HALO, GRID_W)
            return pl.multiple_of(rr * GRID_W, GRID_W), woff, shift

        geo = [geometry(u) for u in range(B_ROWS_PER_STEP)]

        def b_scores(item):
            u, p = item
            q0, woff, _ = geo[u]
            kcat = jnp.concatenate(
                [win_ref[pl.ds(woff, B_KEYS), KV_KB + p * LANES:KV_KB + (p + 1) * LANES], kmetas[p]],
                axis=0)
            qs = split_heads(qb_ref[pl.ds(q0, GRID_W), p * LANES:(p + 1) * LANES])
            return lax.dot_general(qs, kcat, _NT, preferred_element_type=f32)

        def b_finish(item, s):
            u, p = item
            q0, woff, shift = geo[u]
            vcat = jnp.concatenate(
                [win_ref[pl.ds(woff, B_KEYS), KV_VB + p * LANES:KV_VB + (p + 1) * LANES], vmetas[p]],
                axis=0)
            s_loc = s[:, :B_KEYS] + bbias_ref[shift, 2 * p:2 * p + 2].reshape(2 * GRID_W, B_KEYS)
            s_met = s[:, B_KEYS:] + meta_bias
            m = jnp.maximum(jnp.max(s_loc, axis=-1, keepdims=True),
                            jnp.max(s_met, axis=-1, keepdims=True))
            pr = jnp.concatenate([jnp.exp(s_loc - m), jnp.exp(s_met - m)], axis=1)
            l = jnp.sum(pr, axis=-1, keepdims=True)
            o = jnp.dot(pr.astype(bf16), vcat, preferred_element_type=f32) / l
            ob_ref[pl.ds(q0, GRID_W), p * LANES:(p + 1) * LANES] = jnp.where(
                lo_half, o[:GRID_W], o[GRID_W:])

        pipelined([(u, p) for u in range(B_ROWS_PER_STEP) for p in range(B_HEADS // 2)],
                  b_scores, b_finish)
        return carry

    lax.fori_loop(0, ROWS_PER_CHUNK // B_ROWS_PER_STEP, row_step, 0)

    def gated_branch(o_ref, z_col, w_ref, g_col):
        z = jnp.dot(n_ref[...], wr_ref[:, z_col:z_col + A_WIDTH], preferred_element_type=f32)
        a = (o_ref[...] * (z * jax.nn.sigmoid(z))).astype(bf16)
        y = jnp.dot(a, w_ref[...], preferred_element_type=f32)
        gate = jnp.dot(n_ref[...], wr_ref[:, g_col:g_col + D_MODEL], preferred_element_type=f32)
        return jax.nn.sigmoid(gate) * y

    merged = gated_branch(oa_ref, R_ZA, wpa_ref, R_GA) + gated_branch(ob_ref, R_ZB, wpb_ref, R_GB)
    h = x + jnp.dot(merged.astype(bf16), wout_ref[...], preferred_element_type=f32)
    out_ref[0] = _rmsnorm(h, fgain_ref[...])


def _layer_call(sink, x, kv, kv_meta, rope_base, rope_off, gain, fgain, w_rest, w_pa, w_pb, w_out,
                abias, bbias):
    bsz, seq, _ = x.shape
    n_chunks = seq // CHUNK
    halo_per = CHUNK // HALO
    const2 = lambda b, i: (0, 0)
    const3 = lambda b, i: (0, 0, 0)
    const4 = lambda b, i: (0, 0, 0, 0)
    resident = dict(pipeline_mode=pl.Buffered(1))
    body = functools.partial(_layer_body, n_chunks=n_chunks, n_rows=seq // GRID_W)
    return pl.pallas_call(
        body,
        grid=(bsz, n_chunks),
        in_specs=[
            pl.BlockSpec(memory_space=pltpu.SMEM),
            pl.BlockSpec((1, CHUNK, D_MODEL), lambda b, i: (b, i, 0)),
            pl.BlockSpec((1, HALO, KV_COLS), lambda b, i: (b, jnp.maximum(i * halo_per - 1, 0), 0)),
            pl.BlockSpec((1, CHUNK, KV_COLS), lambda b, i: (b, i, 0)),
            pl.BlockSpec((1, HALO, KV_COLS),
                         lambda b, i: (b, jnp.minimum((i + 1) * halo_per, n_chunks * halo_per - 1), 0)),
            pl.BlockSpec((1, N_META, KV_COLS), const3, **resident),
            pl.BlockSpec((CHUNK, LANES), const2, **resident),
            pl.BlockSpec((CHUNK, LANES), const2, **resident),
            pl.BlockSpec((None, 2, LANES), lambda b, i: (i, 0, 0)),
            pl.BlockSpec((1, D_MODEL), const2, **resident),
            pl.BlockSpec((1, D_MODEL), const2, **resident),
            pl.BlockSpec((D_MODEL, R_COLS), const2, **resident),
            pl.BlockSpec((A_WIDTH, D_MODEL), const2, **resident),
            pl.BlockSpec((B_WIDTH, D_MODEL), const2, **resident),
            pl.BlockSpec((D_MODEL, D_MODEL), const2, **resident),
            pl.BlockSpec((3, BLOCK, META_PAD + 3 * BLOCK), const3, **resident),
            pl.BlockSpec((NA_KH, B_HEADS, GRID_W, B_KEYS), const4, **resident),
        ],
        out_specs=pl.BlockSpec((1, CHUNK, D_MODEL), lambda b, i: (b, i, 0)),
        out_shape=jax.ShapeDtypeStruct((bsz, seq, D_MODEL), jnp.float32),
        scratch_shapes=[
            pltpu.VMEM((CHUNK, D_MODEL), jnp.bfloat16),
            pltpu.VMEM((CHUNK + 2 * HALO, KV_COLS), jnp.bfloat16),
            pltpu.VMEM((CHUNK, A_WIDTH), jnp.bfloat16),
            pltpu.VMEM((CHUNK, B_WIDTH), jnp.bfloat16),
            pltpu.VMEM((CHUNK, A_WIDTH), jnp.float32),
            pltpu.VMEM((CHUNK, B_WIDTH), jnp.float32),
        ],
        compiler_params=pltpu.CompilerParams(
            dimension_semantics=("parallel", "arbitrary"), vmem_limit_bytes=VMEM_LIMIT_BYTES),
        name="hybrid_layer",
    )(sink, x, kv, kv, kv, kv_meta, *rope_base, rope_off, gain, fgain, w_rest, w_pa, w_pb, w_out,
      abias, bbias)


def _rope_tables(n_chunks):
    inv_freq = ROPE_THETA ** (-jnp.arange(ROT_HALF, dtype=jnp.float32) / ROT_HALF)
    d = np.arange(LANES) % HEAD_DIM
    inv_lane = jnp.where(d < ROT_DIM, inv_freq[d % ROT_HALF], 0.0)[None, :]
    base = jnp.arange(CHUNK, dtype=jnp.float32)[:, None] * inv_lane
    off = (N_META + CHUNK * jnp.arange(n_chunks, dtype=jnp.float32))[:, None] * inv_lane
    off_real = jnp.stack([jnp.cos(off), jnp.sin(off)], axis=1)
    off_meta = jnp.stack([jnp.ones((1, LANES), jnp.float32), jnp.zeros((1, LANES), jnp.float32)], axis=1)
    return (jnp.cos(base), jnp.sin(base)), off_meta, off_real


def _band_bias():
    qi = np.arange(BLOCK)[:, None]
    col = np.arange(META_PAD + 3 * BLOCK)[None, :]
    kj = col - META_PAD
    band = (col >= META_PAD) & (np.abs(kj - BLOCK - qi) <= WINDOW)
    meta = np.broadcast_to(col < N_META, band.shape)
    variants = [meta | band, meta | (band & (kj >= BLOCK)), meta | (band & (kj < 2 * BLOCK))]
    return jnp.asarray(np.where(np.stack(variants), 0.0, NEG_INF).astype(np.float32))


def _na_bias(rpb):
    cq = np.arange(GRID_W)[:, None]
    kc = np.arange(GRID_W)[None, :]
    start = np.clip(cq - NA_KW // 2, 0, GRID_W - NA_KW)
    valid = (kc >= start) & (kc < start + NA_KW)
    onehot = ((kc - cq + NA_KW - 1)[..., None] == np.arange(2 * NA_KW - 1)) & valid[..., None]
    t15 = jnp.einsum('hrd,ckd->hcrk', rpb, jnp.asarray(onehot, jnp.float32),
                     precision=lax.Precision.HIGHEST)
    t15 = jnp.where(valid[None, :, None, :], t15, NEG_INF)
    return jnp.stack(
        [t15[:, :, NA_KH - 1 - k:2 * NA_KH - 1 - k, :].reshape(B_HEADS, GRID_W, B_KEYS)
         for k in range(NA_KH)], axis=0)


def kernel(x, meta_tokens, norm_gain, w_in, sink_logits, rel_pos_bias, w_proj_a, w_proj_b, w_out,
           final_norm_gain):
    bsz, seq, _ = x.shape
    assert seq % CHUNK == 0 and seq // GRID_W >= NA_KH and norm_gain.shape[0] == 1
    bf16 = jnp.bfloat16
    w = w_in[0].astype(bf16)
    qa, ka, va, za, qb, kb, vb, zb, ga, gb = jnp.split(w, list(np.cumsum(SPLIT_SIZES)[:-1]), axis=1)
    w_kv = jnp.concatenate([ka, va, kb, vb], axis=1)
    w_rest = jnp.concatenate([qa, qb, za, zb, ga, gb], axis=1)

    rope_base, off_meta, off_real = _rope_tables(seq // CHUNK)
    rope_base_meta = tuple(t[:N_META] for t in rope_base)

    gain = norm_gain[0][None]
    kv_meta = _kv_call(meta_tokens[None], gain, w_kv, rope_base_meta, off_meta, N_META)
    kv = _kv_call(x, gain, w_kv, rope_base, off_real, CHUNK)
    return _layer_call(sink_logits[0], x, kv, kv_meta, rope_base, off_real, gain,
                       final_norm_gain[None], w_rest, w_proj_a[0].astype(bf16),
                       w_proj_b[0].astype(bf16), w_out[0].astype(bf16), _band_bias(),
                       _na_bias(rel_pos_bias[0]))
```

```python
import functools

import numpy as np
import jax
import jax.numpy as jnp
from jax import lax
from jax.experimental import pallas as pl
from jax.experimental.pallas import tpu as pltpu

D_MODEL = 1024
N_META = 16
HEAD_DIM = 64
RMS_EPS = 1e-6
NEG_INF = -1e30
A_HEADS = 8
A_KV_HEADS = 2
A_GROUP = A_HEADS // A_KV_HEADS
A_WIDTH = A_HEADS * HEAD_DIM
A_KV_WIDTH = A_KV_HEADS * HEAD_DIM
WINDOW = 128
BLOCK = 128
ROT_DIM = HEAD_DIM // 4
ROT_HALF = ROT_DIM // 2
ROPE_THETA = 500000.0
B_HEADS = 8
B_WIDTH = B_HEADS * HEAD_DIM
GRID_W = 64
NA_KH = 8
NA_KW = 16
SPLIT_SIZES = (A_WIDTH, A_KV_WIDTH, A_KV_WIDTH, A_WIDTH, B_WIDTH, B_WIDTH, B_WIDTH, B_WIDTH,
               D_MODEL, D_MODEL)

LANES = 128
CHUNK = 512
ROWS_PER_CHUNK = CHUNK // GRID_W
BLOCKS_PER_CHUNK = CHUNK // BLOCK
HALO = CHUNK // 2
assert HALO >= BLOCK and HALO >= (NA_KH // 2) * GRID_W
META_PAD = LANES
B_KEYS = NA_KH * GRID_W
A_BLOCKS_PER_ITEM = 1
B_ROWS_PER_STEP = 2
B_PAIRS_PER_ITEM = 2

KV_KA = 0
KV_VA = KV_KA + A_KV_HEADS * LANES
KV_KB = KV_VA + 2 * A_KV_HEADS * LANES
KV_VB = KV_KB + B_WIDTH
KV_COLS = KV_VB + B_WIDTH

W_KA = 0
W_VA = W_KA + A_KV_WIDTH
W_KB = W_VA + A_KV_WIDTH
W_KV_COLS = W_KB + 2 * B_WIDTH
assert A_KV_WIDTH == LANES

R_QA = 0
R_QB = R_QA + A_WIDTH
R_ZA = R_QB + B_WIDTH
R_ZB = R_ZA + A_WIDTH
R_GA = R_ZB + B_WIDTH
R_GB = R_GA + D_MODEL
R_COLS = R_GB + D_MODEL

VMEM_LIMIT_BYTES = 58 * 1024 * 1024

_NT = (((1,), (1,)), ((), ()))


def _rmsnorm(x, gain):
    return x * lax.rsqrt(jnp.mean(x * x, axis=-1, keepdims=True) + RMS_EPS) * gain


def _lane_patterns():
    lane = lax.broadcasted_iota(jnp.int32, (1, LANES), 1)
    return lane < HEAD_DIM, (lane & (HEAD_DIM - 1)) < ROT_HALF


def _rope_coeffs(cb_ref, sb_ref, off_ref, first_half):
    cb, sb = cb_ref[...], sb_ref[...]
    ca, sa = off_ref[0:1, :], off_ref[1:2, :]
    c = cb * ca - sb * sa
    s = (sb * ca + cb * sa) * jnp.where(first_half, -1.0, 1.0)
    return c, s


def _rope_tile(t, c, s, first_half):
    partner = jnp.where(first_half, pltpu.roll(t, LANES - ROT_HALF, 1), pltpu.roll(t, ROT_HALF, 1))
    return t * c + partner * s


def _kv_body(x_ref, gain_ref, w_ref, cb_ref, sb_ref, off_ref, kv_ref):
    bf16 = jnp.bfloat16
    lo_half, first_half = _lane_patterns()
    n = _rmsnorm(x_ref[0], gain_ref[...]).astype(bf16)
    kv = jnp.dot(n, w_ref[...], preferred_element_type=jnp.float32)
    c, s = _rope_coeffs(cb_ref, sb_ref, off_ref, first_half)
    ka = _rope_tile(kv[:, W_KA:W_KA + LANES], c, s, first_half)
    va = kv[:, W_VA:W_VA + LANES]
    ka_sw = pltpu.roll(ka, HEAD_DIM, 1)
    kv_ref[0, :, KV_KA:KV_KA + LANES] = jnp.where(lo_half, ka, ka_sw).astype(bf16)
    kv_ref[0, :, KV_KA + LANES:KV_KA + 2 * LANES] = jnp.where(lo_half, ka_sw, ka).astype(bf16)
    va_sw = pltpu.roll(va, HEAD_DIM, 1)
    ones = jnp.ones_like(va)
    for t, (lo_src, hi_src) in enumerate(((va, ones), (ones, va_sw), (va_sw, ones), (ones, va))):
        col = KV_VA + t * LANES
        kv_ref[0, :, col:col + LANES] = jnp.where(lo_half, lo_src, hi_src).astype(bf16)
    kv_ref[0, :, KV_KB:] = kv[:, W_KB:].astype(bf16)


def _kv_call(x, gain, w_kv, rope_base, rope_off, rows):
    bsz, seq, _ = x.shape
    const = lambda b, i: (0, 0)
    return pl.pallas_call(
        _kv_body,
        grid=(bsz, seq // rows),
        in_specs=[
            pl.BlockSpec((1, rows, D_MODEL), lambda b, i: (b, i, 0)),
            pl.BlockSpec((1, D_MODEL), const),
            pl.BlockSpec((D_MODEL, W_KV_COLS), const, pipeline_mode=pl.Buffered(1)),
            pl.BlockSpec((rows, LANES), const),
            pl.BlockSpec((rows, LANES), const),
            pl.BlockSpec((None, 2, LANES), lambda b, i: (i, 0, 0)),
        ],
        out_specs=pl.BlockSpec((1, rows, KV_COLS), lambda b, i: (b, i, 0)),
        out_shape=jax.ShapeDtypeStruct((bsz, seq, KV_COLS), jnp.bfloat16),
        compiler_params=pltpu.CompilerParams(
            dimension_semantics=("parallel", "parallel"), vmem_limit_bytes=VMEM_LIMIT_BYTES),
        name="kv_proj",
    )(x, gain, w_kv, *rope_base, rope_off)


def _layer_body(sink_ref, x_ref, kvp_ref, kvc_ref, kvn_ref, kvm_ref, cb_ref, sb_ref, off_ref,
                gain_ref, fgain_ref, wr_ref, wpa_ref, wpb_ref, wout_ref, amask_ref, eye_ref, bbias_ref,
                out_ref, n_ref, win_ref, qa_ref, qb_ref, oa_ref, ob_ref, *, n_chunks, n_rows):
    i = pl.program_id(1)
    f32, bf16 = jnp.float32, jnp.bfloat16
    scale = HEAD_DIM ** -0.5

    x = x_ref[0]
    n_ref[...] = _rmsnorm(x, gain_ref[...]).astype(bf16)

    win_ref[0:HALO] = kvp_ref[0]
    win_ref[HALO:HALO + CHUNK] = kvc_ref[0]
    win_ref[HALO + CHUNK:CHUNK + 2 * HALO] = kvn_ref[0]

    lo_half, first_half = _lane_patterns()
    c, s = _rope_coeffs(cb_ref, sb_ref, off_ref, first_half)
    qa = jnp.dot(n_ref[...], wr_ref[:, R_QA:R_QA + A_WIDTH], preferred_element_type=f32)
    for t in range(A_WIDTH // LANES):
        sl = slice(t * LANES, (t + 1) * LANES)
        qa_ref[:, sl] = (_rope_tile(qa[:, sl], c, s, first_half) * scale).astype(bf16)
    qb = jnp.dot(n_ref[...], wr_ref[:, R_QB:R_QB + B_WIDTH], preferred_element_type=f32)
    qb_ref[...] = (qb * scale).astype(bf16)

    lane = lax.broadcasted_iota(jnp.int32, (1, LANES), 1)
    zero_pad = jnp.zeros((META_PAD - N_META, LANES), bf16)

    def meta_tile(col):
        return jnp.concatenate([kvm_ref[0, :, col:col + LANES], zero_pad], axis=0)

    def split_heads(q):
        zeros = jnp.zeros_like(q)
        return jnp.concatenate([jnp.where(lo_half, q, zeros), jnp.where(lo_half, zeros, q)], axis=0)

    def pipelined(items, scores, finish):
        pending = scores(items[0])
        for t, item in enumerate(items):
            nxt = scores(items[t + 1]) if t + 1 < len(items) else None
            finish(item, pending)
            pending = nxt

    a_tiles = A_GROUP // 2
    kmetas_a = [meta_tile(KV_KA + g * LANES) for g in range(A_KV_HEADS)]
    vmetas_a = [meta_tile(KV_VA + t * LANES) for t in range(2 * A_KV_HEADS)]

    def a_window(col, tile, j):
        w0 = HALO + j * BLOCK - BLOCK
        return win_ref[w0:w0 + 3 * BLOCK, col + tile * LANES:col + (tile + 1) * LANES]

    def a_scores(item):
        g, j = item
        first = jnp.logical_and(i == 0, j == 0)
        last = jnp.logical_and(i == n_chunks - 1, j == BLOCKS_PER_CHUNK - 1)
        mask_t = amask_ref[jnp.where(first, 1, jnp.where(last, 2, 0))]
        kcat = jnp.concatenate([kmetas_a[g], a_window(KV_KA, g, j)], axis=0)
        k_aug = jnp.concatenate([kcat, mask_t], axis=1)
        q_tiles = [qa_ref[j * BLOCK:(j + 1) * BLOCK, (a_tiles * g + t) * LANES:(a_tiles * g + t + 1) * LANES]
                   for t in range(a_tiles)]
        zeros = jnp.zeros_like(q_tiles[0])
        qs = jnp.concatenate([jnp.where(lo_half, q, zeros) for q in q_tiles]
                             + [jnp.where(lo_half, zeros, q) for q in q_tiles], axis=0)
        q_aug = jnp.concatenate([qs, eye_ref[...]], axis=1)
        return lax.dot_general(q_aug, k_aug, _NT, preferred_element_type=f32)

    def a_finish(item, s):
        g, j = item
        ps, es = [], []
        for rb in range(A_GROUP):
            sink = sink_ref[A_GROUP * g + 2 * (rb % a_tiles) + rb // a_tiles]
            sh = s[rb * BLOCK:(rb + 1) * BLOCK]
            m = jnp.maximum(jnp.max(sh, axis=-1, keepdims=True), sink)
            ps.append(jnp.exp(sh - m).astype(bf16))
            es.append(jnp.exp(sink - m))
        outs = []
        for parity in range(2):
            tile = 2 * g + parity
            vcat = jnp.concatenate([vmetas_a[tile], a_window(KV_VA, tile, j)], axis=0)
            outs.append(jnp.dot(jnp.concatenate(ps[parity * a_tiles:(parity + 1) * a_tiles], axis=0),
                                vcat, preferred_element_type=f32))
        for t in range(a_tiles):
            oe = outs[0][t * BLOCK:(t + 1) * BLOCK]
            oo = outs[1][t * BLOCK:(t + 1) * BLOCK]
            den = (pltpu.roll(jnp.where(lo_half, oo, oe), HEAD_DIM, 1)
                   + jnp.where(lo_half, es[t], es[a_tiles + t]))
            oa_ref[j * BLOCK:(j + 1) * BLOCK, (a_tiles * g + t) * LANES:(a_tiles * g + t + 1) * LANES] = (
                jnp.where(lo_half, oe, oo) / den)

    a_items = [(g, j) for g in range(A_KV_HEADS) for j in range(BLOCKS_PER_CHUNK)]
    pipelined([tuple(a_items[t:t + A_BLOCKS_PER_ITEM]) for t in range(0, len(a_items), A_BLOCKS_PER_ITEM)],
              lambda group: [a_scores(item) for item in group],
              lambda group, scores: [a_finish(item, s) for item, s in zip(group, scores)])

    meta_bias = jnp.where(lane < N_META, 0.0, NEG_INF).astype(f32)
    kmetas = [meta_tile(KV_KB + p * LANES) for p in range(B_HEADS // 2)]
    vmetas = [meta_tile(KV_VB + p * LANES) for p in range(B_HEADS // 2)]

    def row_step(step, carry):
        def geometry(u):
            rr = step * B_ROWS_PER_STEP + u
            r = i * ROWS_PER_CHUNK + rr
            r_start = jnp.clip(r - NA_KH // 2, 0, n_rows - NA_KH)
            shift = r - r_start
            woff = pl.multiple_of((r_start - i * ROWS_PER_CHUNK) * GRID_W + HALO, GRID_W)
            return pl.multiple_of(rr * GRID_W, GRID_W), woff, shift

        geo = [geometry(u) for u in range(B_ROWS_PER_STEP)]

        def b_scores(item):
            u, pairs = item
            q0, woff, _ = geo[u]
            out = []
            for p in pairs:
                kcat = jnp.concatenate(
                    [win_ref[pl.ds(woff, B_KEYS), KV_KB + p * LANES:KV_KB + (p + 1) * LANES], kmetas[p]],
                    axis=0)
                qs = split_heads(qb_ref[pl.ds(q0, GRID_W), p * LANES:(p + 1) * LANES])
                out.append(lax.dot_general(qs, kcat, _NT, preferred_element_type=f32))
            return out

        def b_finish(item, scores):
            u, pairs = item
            q0, woff, shift = geo[u]
            probs, sums = [], []
            for p, s in zip(pairs, scores):
                s_loc = s[:, :B_KEYS] + bbias_ref[shift, 2 * p:2 * p + 2].reshape(2 * GRID_W, B_KEYS)
                s_met = s[:, B_KEYS:] + meta_bias
                m = jnp.maximum(jnp.max(s_loc, axis=-1, keepdims=True),
                                jnp.max(s_met, axis=-1, keepdims=True))
                pr = jnp.concatenate([jnp.exp(s_loc - m), jnp.exp(s_met - m)], axis=1)
                sums.append(jnp.sum(pr, axis=-1, keepdims=True))
                probs.append(pr.astype(bf16))
            for p, pr, l in zip(pairs, probs, sums):
                vcat = jnp.concatenate(
                    [win_ref[pl.ds(woff, B_KEYS), KV_VB + p * LANES:KV_VB + (p + 1) * LANES], vmetas[p]],
                    axis=0)
                o = jnp.dot(pr, vcat, preferred_element_type=f32) / l
                ob_ref[pl.ds(q0, GRID_W), p * LANES:(p + 1) * LANES] = jnp.where(
                    lo_half, o[:GRID_W], o[GRID_W:])

        n_pairs = B_HEADS // 2
        pipelined([(u, tuple(range(p0, p0 + B_PAIRS_PER_ITEM))) for u in range(B_ROWS_PER_STEP)
                   for p0 in range(0, n_pairs, B_PAIRS_PER_ITEM)], b_scores, b_finish)
        return carry

    lax.fori_loop(0, ROWS_PER_CHUNK // B_ROWS_PER_STEP, row_step, 0)

    def gated_branch(o_ref, z_col, w_ref, g_col):
        z = jnp.dot(n_ref[...], wr_ref[:, z_col:z_col + A_WIDTH], preferred_element_type=f32)
        a = (o_ref[...] * (z * jax.nn.sigmoid(z))).astype(bf16)
        y = jnp.dot(a, w_ref[...], preferred_element_type=f32)
        gate = jnp.dot(n_ref[...], wr_ref[:, g_col:g_col + D_MODEL], preferred_element_type=f32)
        return jax.nn.sigmoid(gate) * y

    merged = gated_branch(oa_ref, R_ZA, wpa_ref, R_GA) + gated_branch(ob_ref, R_ZB, wpb_ref, R_GB)
    h = x + jnp.dot(merged.astype(bf16), wout_ref[...], preferred_element_type=f32)
    out_ref[0] = _rmsnorm(h, fgain_ref[...])


def _layer_call(sink, x, kv, kv_meta, rope_base, rope_off, gain, fgain, w_rest, w_pa, w_pb, w_out,
                amasks, bbias):
    bsz, seq, _ = x.shape
    n_chunks = seq // CHUNK
    halo_per = CHUNK // HALO
    const2 = lambda b, i: (0, 0)
    const3 = lambda b, i: (0, 0, 0)
    const4 = lambda b, i: (0, 0, 0, 0)
    resident = dict(pipeline_mode=pl.Buffered(1))
    body = functools.partial(_layer_body, n_chunks=n_chunks, n_rows=seq // GRID_W)
    return pl.pallas_call(
        body,
        grid=(bsz, n_chunks),
        in_specs=[
            pl.BlockSpec(memory_space=pltpu.SMEM),
            pl.BlockSpec((1, CHUNK, D_MODEL), lambda b, i: (b, i, 0)),
            pl.BlockSpec((1, HALO, KV_COLS), lambda b, i: (b, jnp.maximum(i * halo_per - 1, 0), 0)),
            pl.BlockSpec((1, CHUNK, KV_COLS), lambda b, i: (b, i, 0)),
            pl.BlockSpec((1, HALO, KV_COLS),
                         lambda b, i: (b, jnp.minimum((i + 1) * halo_per, n_chunks * halo_per - 1), 0)),
            pl.BlockSpec((1, N_META, KV_COLS), const3, **resident),
            pl.BlockSpec((CHUNK, LANES), const2, **resident),
            pl.BlockSpec((CHUNK, LANES), const2, **resident),
            pl.BlockSpec((None, 2, LANES), lambda b, i: (i, 0, 0)),
            pl.BlockSpec((1, D_MODEL), const2, **resident),
            pl.BlockSpec((1, D_MODEL), const2, **resident),
            pl.BlockSpec((D_MODEL, R_COLS), const2, **resident),
            pl.BlockSpec((A_WIDTH, D_MODEL), const2, **resident),
            pl.BlockSpec((B_WIDTH, D_MODEL), const2, **resident),
            pl.BlockSpec((D_MODEL, D_MODEL), const2, **resident),
            pl.BlockSpec((3, META_PAD + 3 * BLOCK, BLOCK), const3, **resident),
            pl.BlockSpec((A_GROUP * BLOCK, BLOCK), const2, **resident),
            pl.BlockSpec((NA_KH, B_HEADS, GRID_W, B_KEYS), const4, **resident),
        ],
        out_specs=pl.BlockSpec((1, CHUNK, D_MODEL), lambda b, i: (b, i, 0)),
        out_shape=jax.ShapeDtypeStruct((bsz, seq, D_MODEL), jnp.float32),
        scratch_shapes=[
            pltpu.VMEM((CHUNK, D_MODEL), jnp.bfloat16),
            pltpu.VMEM((CHUNK + 2 * HALO, KV_COLS), jnp.bfloat16),
            pltpu.VMEM((CHUNK, A_WIDTH), jnp.bfloat16),
            pltpu.VMEM((CHUNK, B_WIDTH), jnp.bfloat16),
            pltpu.VMEM((CHUNK, A_WIDTH), jnp.float32),
            pltpu.VMEM((CHUNK, B_WIDTH), jnp.float32),
        ],
        compiler_params=pltpu.CompilerParams(
            dimension_semantics=("parallel", "arbitrary"), vmem_limit_bytes=VMEM_LIMIT_BYTES),
        name="hybrid_layer",
    )(sink, x, kv, kv, kv, kv_meta, *rope_base, rope_off, gain, fgain, w_rest, w_pa, w_pb, w_out,
      *amasks, bbias)


def _rope_tables(n_chunks):
    inv_freq = ROPE_THETA ** (-jnp.arange(ROT_HALF, dtype=jnp.float32) / ROT_HALF)
    d = np.arange(LANES) % HEAD_DIM
    inv_lane = jnp.where(d < ROT_DIM, inv_freq[d % ROT_HALF], 0.0)[None, :]
    base = jnp.arange(CHUNK, dtype=jnp.float32)[:, None] * inv_lane
    off = (N_META + CHUNK * jnp.arange(n_chunks, dtype=jnp.float32))[:, None] * inv_lane
    off_real = jnp.stack([jnp.cos(off), jnp.sin(off)], axis=1)
    off_meta = jnp.stack([jnp.ones((1, LANES), jnp.float32), jnp.zeros((1, LANES), jnp.float32)], axis=1)
    return (jnp.cos(base), jnp.sin(base)), off_meta, off_real


def _band_masks():
    qi = np.arange(BLOCK)[:, None]
    col = np.arange(META_PAD + 3 * BLOCK)[None, :]
    kj = col - META_PAD
    band = (col >= META_PAD) & (np.abs(kj - BLOCK - qi) <= WINDOW)
    meta = np.broadcast_to(col < N_META, band.shape)
    variants = [meta | band, meta | (band & (kj >= BLOCK)), meta | (band & (kj < 2 * BLOCK))]
    masks = np.where(np.stack(variants), 0.0, NEG_INF).astype(np.float32).transpose(0, 2, 1)
    eye = np.tile(np.eye(BLOCK, dtype=np.float32), (A_GROUP, 1))
    return jnp.asarray(masks, jnp.bfloat16), jnp.asarray(eye, jnp.bfloat16)


def _na_bias(rpb):
    cq = np.arange(GRID_W)[:, None]
    kc = np.arange(GRID_W)[None, :]
    start = np.clip(cq - NA_KW // 2, 0, GRID_W - NA_KW)
    valid = (kc >= start) & (kc < start + NA_KW)
    onehot = ((kc - cq + NA_KW - 1)[..., None] == np.arange(2 * NA_KW - 1)) & valid[..., None]
    t15 = jnp.einsum('hrd,ckd->hcrk', rpb, jnp.asarray(onehot, jnp.float32),
                     precision=lax.Precision.HIGHEST)
    t15 = jnp.where(valid[None, :, None, :], t15, NEG_INF)
    return jnp.stack(
        [t15[:, :, NA_KH - 1 - k:2 * NA_KH - 1 - k, :].reshape(B_HEADS, GRID_W, B_KEYS)
         for k in range(NA_KH)], axis=0)


def kernel(x, meta_tokens, norm_gain, w_in, sink_logits, rel_pos_bias, w_proj_a, w_proj_b, w_out,
           final_norm_gain):
    bsz, seq, _ = x.shape
    assert seq % CHUNK == 0 and seq // GRID_W >= NA_KH and norm_gain.shape[0] == 1
    bf16 = jnp.bfloat16
    w = w_in[0].astype(bf16)
    qa, ka, va, za, qb, kb, vb, zb, ga, gb = jnp.split(w, list(np.cumsum(SPLIT_SIZES)[:-1]), axis=1)
    w_kv = jnp.concatenate([ka, va, kb, vb], axis=1)
    w_rest = jnp.concatenate([qa, qb, za, zb, ga, gb], axis=1)

    rope_base, off_meta, off_real = _rope_tables(seq // CHUNK)
    rope_base_meta = tuple(t[:N_META] for t in rope_base)

    gain = norm_gain[0][None]
    kv_meta = _kv_call(meta_tokens[None], gain, w_kv, rope_base_meta, off_meta, N_META)
    kv = _kv_call(x, gain, w_kv, rope_base, off_real, CHUNK)
    return _layer_call(sink_logits[0], x, kv, kv_meta, rope_base, off_real, gain,
                       final_norm_gain[None], w_rest, w_proj_a[0].astype(bf16),
                       w_proj_b[0].astype(bf16), w_out[0].astype(bf16), _band_masks(),
                       _na_bias(rel_pos_bias[0]))
```

```python
import functools

import numpy as np
import jax
import jax.numpy as jnp
from jax import lax
from jax.experimental import pallas as pl
from jax.experimental.pallas import tpu as pltpu

D_MODEL = 1024
N_META = 16
HEAD_DIM = 64
RMS_EPS = 1e-6
NEG_INF = -1e30
A_HEADS = 8
A_KV_HEADS = 2
A_GROUP = A_HEADS // A_KV_HEADS
A_WIDTH = A_HEADS * HEAD_DIM
A_KV_WIDTH = A_KV_HEADS * HEAD_DIM
WINDOW = 128
BLOCK = 128
ROT_DIM = HEAD_DIM // 4
ROT_HALF = ROT_DIM // 2
ROPE_THETA = 500000.0
B_HEADS = 8
B_WIDTH = B_HEADS * HEAD_DIM
GRID_W = 64
NA_KH = 8
NA_KW = 16
SPLIT_SIZES = (A_WIDTH, A_KV_WIDTH, A_KV_WIDTH, A_WIDTH, B_WIDTH, B_WIDTH, B_WIDTH, B_WIDTH,
               D_MODEL, D_MODEL)

LANES = 128
CHUNK = 512
ROWS_PER_CHUNK = CHUNK // GRID_W
BLOCKS_PER_CHUNK = CHUNK // BLOCK
HALO = CHUNK // 2
assert HALO >= BLOCK and HALO >= (NA_KH // 2) * GRID_W
META_PAD = LANES
B_WIN_ROWS = NA_KH + 2
B_WIN_KEYS = B_WIN_ROWS * GRID_W
A_BLOCKS_PER_ITEM = 1
B_PAIRS_PER_ITEM = 2

KV_KA = 0
KV_VA = KV_KA + A_KV_HEADS * LANES
KV_KB = KV_VA + A_KV_HEADS * LANES
KV_VB = KV_KB + B_WIDTH
KV_COLS = KV_VB + B_WIDTH

W_KA = 0
W_VA = W_KA + A_KV_WIDTH
W_KB = W_VA + A_KV_WIDTH
W_KV_COLS = W_KB + 2 * B_WIDTH
assert A_KV_WIDTH == LANES

R_QA = 0
R_QB = R_QA + A_WIDTH
R_ZA = R_QB + B_WIDTH
R_ZB = R_ZA + A_WIDTH
R_GA = R_ZB + B_WIDTH
R_GB = R_GA + D_MODEL
R_COLS = R_GB + D_MODEL

VMEM_LIMIT_BYTES = 58 * 1024 * 1024

_NT = (((1,), (1,)), ((), ()))


def _rmsnorm(x, gain):
    return x * lax.rsqrt(jnp.mean(x * x, axis=-1, keepdims=True) + RMS_EPS) * gain


def _lane_patterns():
    lane = lax.broadcasted_iota(jnp.int32, (1, LANES), 1)
    return lane < HEAD_DIM, (lane & (HEAD_DIM - 1)) < ROT_HALF


def _rope_coeffs(cb_ref, sb_ref, off_ref, first_half):
    cb, sb = cb_ref[...], sb_ref[...]
    ca, sa = off_ref[0:1, :], off_ref[1:2, :]
    c = cb * ca - sb * sa
    s = (sb * ca + cb * sa) * jnp.where(first_half, -1.0, 1.0)
    return c, s


def _rope_tile(t, c, s, first_half):
    partner = jnp.where(first_half, pltpu.roll(t, LANES - ROT_HALF, 1), pltpu.roll(t, ROT_HALF, 1))
    return t * c + partner * s


def _kv_body(x_ref, gain_ref, w_ref, cb_ref, sb_ref, off_ref, kv_ref):
    bf16 = jnp.bfloat16
    lo_half, first_half = _lane_patterns()
    n = _rmsnorm(x_ref[0], gain_ref[...]).astype(bf16)
    kv = jnp.dot(n, w_ref[...], preferred_element_type=jnp.float32)
    c, s = _rope_coeffs(cb_ref, sb_ref, off_ref, first_half)
    ka = _rope_tile(kv[:, W_KA:W_KA + LANES], c, s, first_half)
    va = kv[:, W_VA:W_VA + LANES]
    for src, col in ((ka, KV_KA), (va, KV_VA)):
        swapped = pltpu.roll(src, HEAD_DIM, 1)
        kv_ref[0, :, col:col + LANES] = jnp.where(lo_half, src, swapped).astype(bf16)
        kv_ref[0, :, col + LANES:col + 2 * LANES] = jnp.where(lo_half, swapped, src).astype(bf16)
    kv_ref[0, :, KV_KB:] = kv[:, W_KB:].astype(bf16)


def _kv_call(x, gain, w_kv, rope_base, rope_off, rows):
    bsz, seq, _ = x.shape
    const = lambda b, i: (0, 0)
    return pl.pallas_call(
        _kv_body,
        grid=(bsz, seq // rows),
        in_specs=[
            pl.BlockSpec((1, rows, D_MODEL), lambda b, i: (b, i, 0)),
            pl.BlockSpec((1, D_MODEL), const),
            pl.BlockSpec((D_MODEL, W_KV_COLS), const, pipeline_mode=pl.Buffered(1)),
            pl.BlockSpec((rows, LANES), const),
            pl.BlockSpec((rows, LANES), const),
            pl.BlockSpec((None, 2, LANES), lambda b, i: (i, 0, 0)),
        ],
        out_specs=pl.BlockSpec((1, rows, KV_COLS), lambda b, i: (b, i, 0)),
        out_shape=jax.ShapeDtypeStruct((bsz, seq, KV_COLS), jnp.bfloat16),
        compiler_params=pltpu.CompilerParams(
            dimension_semantics=("parallel", "parallel"), vmem_limit_bytes=VMEM_LIMIT_BYTES),
        name="kv_proj",
    )(x, gain, w_kv, *rope_base, rope_off)


def _layer_body(sink_ref, x_ref, kvp_ref, kvc_ref, kvn_ref, kvm_ref, cb_ref, sb_ref, off_ref,
                gain_ref, fgain_ref, wr_ref, wpa_ref, wpb_ref, wout_ref, amask_ref, eye_ref, bpair_ref,
                brow_ref, bsel_ref,
                out_ref, n_ref, win_ref, qa_ref, qb_ref, oa_ref, ob_ref, *, n_chunks, n_rows):
    i = pl.program_id(1)
    f32, bf16 = jnp.float32, jnp.bfloat16
    scale = HEAD_DIM ** -0.5

    x = x_ref[0]
    n_ref[...] = _rmsnorm(x, gain_ref[...]).astype(bf16)

    win_ref[0:HALO] = kvp_ref[0]
    win_ref[HALO:HALO + CHUNK] = kvc_ref[0]
    win_ref[HALO + CHUNK:CHUNK + 2 * HALO] = kvn_ref[0]

    lo_half, first_half = _lane_patterns()
    c, s = _rope_coeffs(cb_ref, sb_ref, off_ref, first_half)
    qa = jnp.dot(n_ref[...], wr_ref[:, R_QA:R_QA + A_WIDTH], preferred_element_type=f32)
    for t in range(A_WIDTH // LANES):
        sl = slice(t * LANES, (t + 1) * LANES)
        qa_ref[:, sl] = (_rope_tile(qa[:, sl], c, s, first_half) * scale).astype(bf16)
    qb = jnp.dot(n_ref[...], wr_ref[:, R_QB:R_QB + B_WIDTH], preferred_element_type=f32)
    qb_ref[...] = (qb * scale).astype(bf16)

    zero_pad = jnp.zeros((META_PAD - N_META, LANES), bf16)

    def meta_tile(col):
        return jnp.concatenate([kvm_ref[0, :, col:col + LANES], zero_pad], axis=0)

    def split_heads(q):
        zeros = jnp.zeros_like(q)
        return jnp.concatenate([jnp.where(lo_half, q, zeros), jnp.where(lo_half, zeros, q)], axis=0)

    def with_ones(v):
        return jnp.concatenate([v, jnp.ones_like(v)], axis=1)

    def pipelined(items, scores, finish):
        pending = scores(items[0])
        for t, item in enumerate(items):
            nxt = scores(items[t + 1]) if t + 1 < len(items) else None
            finish(item, pending)
            pending = nxt

    a_tiles = A_GROUP // 2
    kmetas_a = [meta_tile(KV_KA + g * LANES) for g in range(A_KV_HEADS)]
    vmetas_a = [meta_tile(KV_VA + g * LANES) for g in range(A_KV_HEADS)]

    def a_window(col, tile, j):
        w0 = HALO + j * BLOCK - BLOCK
        return win_ref[w0:w0 + 3 * BLOCK, col + tile * LANES:col + (tile + 1) * LANES]

    def a_scores(item):
        g, j = item
        first = jnp.logical_and(i == 0, j == 0)
        last = jnp.logical_and(i == n_chunks - 1, j == BLOCKS_PER_CHUNK - 1)
        mask_t = amask_ref[jnp.where(first, 1, jnp.where(last, 2, 0))]
        kcat = jnp.concatenate([kmetas_a[g], a_window(KV_KA, g, j)], axis=0)
        k_aug = jnp.concatenate([kcat, mask_t], axis=1)
        q_tiles = [qa_ref[j * BLOCK:(j + 1) * BLOCK, (a_tiles * g + t) * LANES:(a_tiles * g + t + 1) * LANES]
                   for t in range(a_tiles)]
        zeros = jnp.zeros_like(q_tiles[0])
        qs = jnp.concatenate([jnp.where(lo_half, q, zeros) for q in q_tiles]
                             + [jnp.where(lo_half, zeros, q) for q in q_tiles], axis=0)
        q_aug = jnp.concatenate([qs, eye_ref[...]], axis=1)
        return lax.dot_general(q_aug, k_aug, _NT, preferred_element_type=f32)

    def a_finish(item, s):
        g, j = item
        ps, es = [], []
        for rb in range(A_GROUP):
            sink = sink_ref[A_GROUP * g + 2 * (rb % a_tiles) + rb // a_tiles]
            sh = s[rb * BLOCK:(rb + 1) * BLOCK]
            m = jnp.maximum(jnp.max(sh, axis=-1, keepdims=True), sink)
            ps.append(jnp.exp(sh - m).astype(bf16))
            es.append(jnp.exp(sink - m))
        vcat = jnp.concatenate([vmetas_a[g], a_window(KV_VA, g, j)], axis=0)
        o = jnp.dot(jnp.concatenate(ps, axis=0), with_ones(vcat), preferred_element_type=f32)
        for t in range(a_tiles):
            oe = o[t * BLOCK:(t + 1) * BLOCK]
            oo = o[(a_tiles + t) * BLOCK:(a_tiles + t + 1) * BLOCK]
            den = (jnp.where(lo_half, oe[:, LANES:], oo[:, LANES:])
                   + jnp.where(lo_half, es[t], es[a_tiles + t]))
            oa_ref[j * BLOCK:(j + 1) * BLOCK, (a_tiles * g + t) * LANES:(a_tiles * g + t + 1) * LANES] = (
                jnp.where(lo_half, oe[:, :LANES], oo[:, :LANES]) / den)

    a_items = [(g, j) for g in range(A_KV_HEADS) for j in range(BLOCKS_PER_CHUNK)]
    pipelined([tuple(a_items[t:t + A_BLOCKS_PER_ITEM]) for t in range(0, len(a_items), A_BLOCKS_PER_ITEM)],
              lambda group: [a_scores(item) for item in group],
              lambda group, scores: [a_finish(item, s) for item, s in zip(group, scores)])

    kmetas = [meta_tile(KV_KB + p * LANES) for p in range(B_HEADS // 2)]
    vmetas = [meta_tile(KV_VB + p * LANES) for p in range(B_HEADS // 2)]

    def unit_step(step, carry):
        r0 = i * ROWS_PER_CHUNK + 2 * step
        rs0 = jnp.clip(r0 - NA_KH // 2, 0, n_rows - NA_KH)
        rs1 = jnp.clip(r0 + 1 - NA_KH // 2, 0, n_rows - NA_KH)
        shifts = (r0 - rs0, r0 + 1 - rs0)
        row_mask_t = brow_ref[1 - (rs1 - rs0)]
        woff = pl.multiple_of((rs0 - i * ROWS_PER_CHUNK) * GRID_W + HALO, BLOCK)
        q0 = pl.multiple_of(step * BLOCK, BLOCK)

        def b_scores(pairs):
            out = []
            for p in pairs:
                kcat = jnp.concatenate(
                    [win_ref[pl.ds(woff, B_WIN_KEYS), KV_KB + p * LANES:KV_KB + (p + 1) * LANES],
                     kmetas[p]], axis=0)
                k_aug = jnp.concatenate([kcat, row_mask_t], axis=1)
                qs = split_heads(qb_ref[pl.ds(q0, BLOCK), p * LANES:(p + 1) * LANES])
                q_aug = jnp.concatenate([qs, bsel_ref[...]], axis=1)
                out.append(lax.dot_general(q_aug, k_aug, _NT, preferred_element_type=f32))
            return out

        def b_finish(pairs, scores):
            probs = []
            for p, s in zip(pairs, scores):
                bias = jnp.concatenate(
                    [jnp.concatenate([bpair_ref[p, 2 * jj + NA_KH - 1 - shifts[u], hh]
                                      for jj in range(B_WIN_ROWS // 2)], axis=1)
                     for hh in range(2) for u in range(2)], axis=0)
                s_loc = s[:, :B_WIN_KEYS] + bias
                s_met = s[:, B_WIN_KEYS:]
                m = jnp.maximum(jnp.max(s_loc, axis=-1, keepdims=True),
                                jnp.max(s_met, axis=-1, keepdims=True))
                probs.append(jnp.concatenate([jnp.exp(s_loc - m), jnp.exp(s_met - m)],
                                             axis=1).astype(bf16))
            for p, pr in zip(pairs, probs):
                vcat = jnp.concatenate(
                    [win_ref[pl.ds(woff, B_WIN_KEYS), KV_VB + p * LANES:KV_VB + (p + 1) * LANES],
                     vmetas[p]], axis=0)
                o = jnp.dot(pr, with_ones(vcat), preferred_element_type=f32)
                o = o[:, :LANES] / o[:, LANES:]
                ob_ref[pl.ds(q0, BLOCK), p * LANES:(p + 1) * LANES] = jnp.where(
                    lo_half, o[:BLOCK], o[BLOCK:])

        n_pairs = B_HEADS // 2
        pipelined([tuple(range(p0, p0 + B_PAIRS_PER_ITEM)) for p0 in range(0, n_pairs, B_PAIRS_PER_ITEM)],
                  b_scores, b_finish)
        return carry

    lax.fori_loop(0, ROWS_PER_CHUNK // 2, unit_step, 0)

    def gated_branch(o_ref, z_col, w_ref, g_col):
        z = jnp.dot(n_ref[...], wr_ref[:, z_col:z_col + A_WIDTH], preferred_element_type=f32)
        a = (o_ref[...] * (z * jax.nn.sigmoid(z))).astype(bf16)
        y = jnp.dot(a, w_ref[...], preferred_element_type=f32)
        gate = jnp.dot(n_ref[...], wr_ref[:, g_col:g_col + D_MODEL], preferred_element_type=f32)
        return jax.nn.sigmoid(gate) * y

    merged = gated_branch(oa_ref, R_ZA, wpa_ref, R_GA) + gated_branch(ob_ref, R_ZB, wpb_ref, R_GB)
    h = x + jnp.dot(merged.astype(bf16), wout_ref[...], preferred_element_type=f32)
    out_ref[0] = _rmsnorm(h, fgain_ref[...])


def _layer_call(sink, x, kv, kv_meta, rope_base, rope_off, gain, fgain, w_rest, w_pa, w_pb, w_out,
                amasks, btables):
    bsz, seq, _ = x.shape
    n_chunks = seq // CHUNK
    halo_per = CHUNK // HALO
    const2 = lambda b, i: (0, 0)
    const3 = lambda b, i: (0, 0, 0)
    const4 = lambda b, i: (0, 0, 0, 0)
    resident = dict(pipeline_mode=pl.Buffered(1))
    body = functools.partial(_layer_body, n_chunks=n_chunks, n_rows=seq // GRID_W)
    return pl.pallas_call(
        body,
        grid=(bsz, n_chunks),
        in_specs=[
            pl.BlockSpec(memory_space=pltpu.SMEM),
            pl.BlockSpec((1, CHUNK, D_MODEL), lambda b, i: (b, i, 0)),
            pl.BlockSpec((1, HALO, KV_COLS), lambda b, i: (b, jnp.maximum(i * halo_per - 1, 0), 0)),
            pl.BlockSpec((1, CHUNK, KV_COLS), lambda b, i: (b, i, 0)),
            pl.BlockSpec((1, HALO, KV_COLS),
                         lambda b, i: (b, jnp.minimum((i + 1) * halo_per, n_chunks * halo_per - 1), 0)),
            pl.BlockSpec((1, N_META, KV_COLS), const3, **resident),
            pl.BlockSpec((CHUNK, LANES), const2, **resident),
            pl.BlockSpec((CHUNK, LANES), const2, **resident),
            pl.BlockSpec((None, 2, LANES), lambda b, i: (i, 0, 0)),
            pl.BlockSpec((1, D_MODEL), const2, **resident),
            pl.BlockSpec((1, D_MODEL), const2, **resident),
            pl.BlockSpec((D_MODEL, R_COLS), const2, **resident),
            pl.BlockSpec((A_WIDTH, D_MODEL), const2, **resident),
            pl.BlockSpec((B_WIDTH, D_MODEL), const2, **resident),
            pl.BlockSpec((D_MODEL, D_MODEL), const2, **resident),
            pl.BlockSpec((3, META_PAD + 3 * BLOCK, BLOCK), const3, **resident),
            pl.BlockSpec((A_GROUP * BLOCK, BLOCK), const2, **resident),
            pl.BlockSpec((B_HEADS // 2, 2 * NA_KH, 2, GRID_W, LANES), lambda b, i: (0, 0, 0, 0, 0),
                         **resident),
            pl.BlockSpec((2, B_WIN_KEYS + META_PAD, LANES), const3, **resident),
            pl.BlockSpec((2 * BLOCK, LANES), const2, **resident),
        ],
        out_specs=pl.BlockSpec((1, CHUNK, D_MODEL), lambda b, i: (b, i, 0)),
        out_shape=jax.ShapeDtypeStruct((bsz, seq, D_MODEL), jnp.float32),
        scratch_shapes=[
            pltpu.VMEM((CHUNK, D_MODEL), jnp.bfloat16),
            pltpu.VMEM((CHUNK + 2 * HALO, KV_COLS), jnp.bfloat16),
            pltpu.VMEM((CHUNK, A_WIDTH), jnp.bfloat16),
            pltpu.VMEM((CHUNK, B_WIDTH), jnp.bfloat16),
            pltpu.VMEM((CHUNK, A_WIDTH), jnp.float32),
            pltpu.VMEM((CHUNK, B_WIDTH), jnp.float32),
        ],
        compiler_params=pltpu.CompilerParams(
            dimension_semantics=("parallel", "arbitrary"), vmem_limit_bytes=VMEM_LIMIT_BYTES),
        name="hybrid_layer",
    )(sink, x, kv, kv, kv, kv_meta, *rope_base, rope_off, gain, fgain, w_rest, w_pa, w_pb, w_out,
      *amasks, *btables)


def _rope_tables(n_chunks):
    inv_freq = ROPE_THETA ** (-jnp.arange(ROT_HALF, dtype=jnp.float32) / ROT_HALF)
    d = np.arange(LANES) % HEAD_DIM
    inv_lane = jnp.where(d < ROT_DIM, inv_freq[d % ROT_HALF], 0.0)[None, :]
    base = jnp.arange(CHUNK, dtype=jnp.float32)[:, None] * inv_lane
    off = (N_META + CHUNK * jnp.arange(n_chunks, dtype=jnp.float32))[:, None] * inv_lane
    off_real = jnp.stack([jnp.cos(off), jnp.sin(off)], axis=1)
    off_meta = jnp.stack([jnp.ones((1, LANES), jnp.float32), jnp.zeros((1, LANES), jnp.float32)], axis=1)
    return (jnp.cos(base), jnp.sin(base)), off_meta, off_real


def _band_masks():
    qi = np.arange(BLOCK)[:, None]
    col = np.arange(META_PAD + 3 * BLOCK)[None, :]
    kj = col - META_PAD
    band = (col >= META_PAD) & (np.abs(kj - BLOCK - qi) <= WINDOW)
    meta = np.broadcast_to(col < N_META, band.shape)
    variants = [meta | band, meta | (band & (kj >= BLOCK)), meta | (band & (kj < 2 * BLOCK))]
    masks = np.where(np.stack(variants), 0.0, NEG_INF).astype(np.float32).transpose(0, 2, 1)
    eye = np.tile(np.eye(BLOCK, dtype=np.float32), (A_GROUP, 1))
    return jnp.asarray(masks, jnp.bfloat16), jnp.asarray(eye, jnp.bfloat16)


def _na_bias(rpb):
    cq = np.arange(GRID_W)[:, None]
    kc = np.arange(GRID_W)[None, :]
    start = np.clip(cq - NA_KW // 2, 0, GRID_W - NA_KW)
    valid = (kc >= start) & (kc < start + NA_KW)
    onehot = ((kc - cq + NA_KW - 1)[..., None] == np.arange(2 * NA_KW - 1)) & valid[..., None]
    n_pairs, n_dr = B_HEADS // 2, 2 * NA_KH - 1
    t15 = jnp.einsum('phrd,ckd->prhck', rpb.reshape(n_pairs, 2, n_dr, 2 * NA_KW - 1),
                     jnp.asarray(onehot, jnp.float32), precision=lax.Precision.HIGHEST)
    t15 = jnp.pad(t15, ((0, 0), (0, 2 * NA_KH + 1 - n_dr), (0, 0), (0, 0), (0, 0)))
    t15 = jnp.where(valid[None, None, None], t15, NEG_INF)
    pair = jnp.concatenate([t15[:, :2 * NA_KH], t15[:, 1:]], axis=-1)

    key_row = np.arange(B_WIN_KEYS + META_PAD) // GRID_W
    is_meta = (np.arange(B_WIN_KEYS + META_PAD) >= B_WIN_KEYS) & (
        np.arange(B_WIN_KEYS + META_PAD) < B_WIN_KEYS + N_META)
    rows = np.zeros((2, B_WIN_KEYS + META_PAD, LANES), np.float32)
    for variant, starts in enumerate(((0, 1), (0, 0))):
        for u, s0 in enumerate(starts):
            seen = ((key_row >= s0) & (key_row < s0 + NA_KH)) | is_meta
            rows[variant, :, u] = np.where(seen, 0.0, NEG_INF)
    sel = np.zeros((2, 2, GRID_W, LANES), np.float32)
    sel[:, 0, :, 0] = 1.0
    sel[:, 1, :, 1] = 1.0
    return (pair, jnp.asarray(rows, jnp.bfloat16),
            jnp.asarray(sel.reshape(2 * BLOCK, LANES), jnp.bfloat16))


def kernel(x, meta_tokens, norm_gain, w_in, sink_logits, rel_pos_bias, w_proj_a, w_proj_b, w_out,
           final_norm_gain):
    bsz, seq, _ = x.shape
    assert seq % CHUNK == 0 and seq // GRID_W >= NA_KH and norm_gain.shape[0] == 1
    bf16 = jnp.bfloat16
    w = w_in[0].astype(bf16)
    qa, ka, va, za, qb, kb, vb, zb, ga, gb = jnp.split(w, list(np.cumsum(SPLIT_SIZES)[:-1]), axis=1)
    w_kv = jnp.concatenate([ka, va, kb, vb], axis=1)
    w_rest = jnp.concatenate([qa, qb, za, zb, ga, gb], axis=1)

    rope_base, off_meta, off_real = _rope_tables(seq // CHUNK)
    rope_base_meta = tuple(t[:N_META] for t in rope_base)

    gain = norm_gain[0][None]
    kv_meta = _kv_call(meta_tokens[None], gain, w_kv, rope_base_meta, off_meta, N_META)
    kv = _kv_call(x, gain, w_kv, rope_base, off_real, CHUNK)
    return _layer_call(sink_logits[0], x, kv, kv_meta, rope_base, off_real, gain,
                       final_norm_gain[None], w_rest, w_proj_a[0].astype(bf16),
                       w_proj_b[0].astype(bf16), w_out[0].astype(bf16), _band_masks(),
                       _na_bias(rel_pos_bias[0]))
```

```python
import functools

import numpy as np
import jax
import jax.numpy as jnp
from jax import lax
from jax.experimental import pallas as pl
from jax.experimental.pallas import tpu as pltpu

D_MODEL = 1024
N_META = 16
HEAD_DIM = 64
RMS_EPS = 1e-6
NEG_INF = -1e30
A_HEADS = 8
A_KV_HEADS = 2
A_GROUP = A_HEADS // A_KV_HEADS
A_WIDTH = A_HEADS * HEAD_DIM
A_KV_WIDTH = A_KV_HEADS * HEAD_DIM
WINDOW = 128
BLOCK = 128
ROT_DIM = HEAD_DIM // 4
ROT_HALF = ROT_DIM // 2
ROPE_THETA = 500000.0
B_HEADS = 8
B_WIDTH = B_HEADS * HEAD_DIM
GRID_W = 64
NA_KH = 8
NA_KW = 16
SPLIT_SIZES = (A_WIDTH, A_KV_WIDTH, A_KV_WIDTH, A_WIDTH, B_WIDTH, B_WIDTH, B_WIDTH, B_WIDTH,
               D_MODEL, D_MODEL)

LANES = 128
CHUNK = 512
ROWS_PER_CHUNK = CHUNK // GRID_W
BLOCKS_PER_CHUNK = CHUNK // BLOCK
HALO = CHUNK // 2
assert HALO >= BLOCK and HALO >= (NA_KH // 2) * GRID_W
META_PAD = LANES
B_WIN_ROWS = NA_KH + 2
B_WIN_KEYS = B_WIN_ROWS * GRID_W
B_PAIRS_PER_ITEM = 2
PROJ_PIECE = 256
A_FILL = 4
B_FILL = 8

KV_KA = 0
KV_VA = KV_KA + A_KV_HEADS * LANES
KV_KB = KV_VA + A_KV_HEADS * LANES
KV_VB = KV_KB + B_WIDTH
KV_COLS = KV_VB + B_WIDTH

W_KA = 0
W_VA = W_KA + A_KV_WIDTH
W_KB = W_VA + A_KV_WIDTH
W_KV_COLS = W_KB + 2 * B_WIDTH
assert A_KV_WIDTH == LANES

R_QA = 0
R_QB = R_QA + A_WIDTH
R_ZA = R_QB + B_WIDTH
R_ZB = R_ZA + A_WIDTH
R_GA = R_ZB + B_WIDTH
R_GB = R_GA + D_MODEL
R_COLS = R_GB + D_MODEL

VMEM_LIMIT_BYTES = 58 * 1024 * 1024

_NT = (((1,), (1,)), ((), ()))


def _rmsnorm(x, gain):
    return x * lax.rsqrt(jnp.mean(x * x, axis=-1, keepdims=True) + RMS_EPS) * gain


def _lane_patterns():
    lane = lax.broadcasted_iota(jnp.int32, (1, LANES), 1)
    return lane < HEAD_DIM, (lane & (HEAD_DIM - 1)) < ROT_HALF


def _rope_coeffs(cb_ref, sb_ref, off_ref, first_half):
    cb, sb = cb_ref[...], sb_ref[...]
    ca, sa = off_ref[0:1, :], off_ref[1:2, :]
    c = cb * ca - sb * sa
    s = (sb * ca + cb * sa) * jnp.where(first_half, -1.0, 1.0)
    return c, s


def _rope_tile(t, c, s, first_half):
    partner = jnp.where(first_half, pltpu.roll(t, LANES - ROT_HALF, 1), pltpu.roll(t, ROT_HALF, 1))
    return t * c + partner * s


def _kv_body(x_ref, gain_ref, w_ref, cb_ref, sb_ref, off_ref, kv_ref):
    bf16 = jnp.bfloat16
    lo_half, first_half = _lane_patterns()
    n = _rmsnorm(x_ref[0], gain_ref[...]).astype(bf16)
    kv = jnp.dot(n, w_ref[...], preferred_element_type=jnp.float32)
    c, s = _rope_coeffs(cb_ref, sb_ref, off_ref, first_half)
    ka = _rope_tile(kv[:, W_KA:W_KA + LANES], c, s, first_half)
    va = kv[:, W_VA:W_VA + LANES]
    for src, col in ((ka, KV_KA), (va, KV_VA)):
        swapped = pltpu.roll(src, HEAD_DIM, 1)
        kv_ref[0, :, col:col + LANES] = jnp.where(lo_half, src, swapped).astype(bf16)
        kv_ref[0, :, col + LANES:col + 2 * LANES] = jnp.where(lo_half, swapped, src).astype(bf16)
    kv_ref[0, :, KV_KB:] = kv[:, W_KB:].astype(bf16)


def _kv_call(x, gain, w_kv, rope_base, rope_off, rows):
    bsz, seq, _ = x.shape
    const = lambda b, i: (0, 0)
    return pl.pallas_call(
        _kv_body,
        grid=(bsz, seq // rows),
        in_specs=[
            pl.BlockSpec((1, rows, D_MODEL), lambda b, i: (b, i, 0)),
            pl.BlockSpec((1, D_MODEL), const),
            pl.BlockSpec((D_MODEL, W_KV_COLS), const, pipeline_mode=pl.Buffered(1)),
            pl.BlockSpec((rows, LANES), const),
            pl.BlockSpec((rows, LANES), const),
            pl.BlockSpec((None, 2, LANES), lambda b, i: (i, 0, 0)),
        ],
        out_specs=pl.BlockSpec((1, rows, KV_COLS), lambda b, i: (b, i, 0)),
        out_shape=jax.ShapeDtypeStruct((bsz, seq, KV_COLS), jnp.bfloat16),
        compiler_params=pltpu.CompilerParams(
            dimension_semantics=("parallel", "parallel"), vmem_limit_bytes=VMEM_LIMIT_BYTES),
        name="kv_proj",
    )(x, gain, w_kv, *rope_base, rope_off)


def _layer_body(sink_ref, x_ref, kvp_ref, kvc_ref, kvn_ref, kvm_ref, cb_ref, sb_ref, off_ref,
                gain_ref, fgain_ref, wr_ref, wpa_ref, wpb_ref, wout_ref, amask_ref, eye_ref, bpair_ref,
                brow_ref, bsel_ref,
                out_ref, n_ref, win_ref, qa_ref, qb_ref, oa_ref, ob_ref, proj_ref, *, n_chunks, n_rows):
    i = pl.program_id(1)
    f32, bf16 = jnp.float32, jnp.bfloat16
    scale = HEAD_DIM ** -0.5

    x = x_ref[0]
    n_ref[...] = _rmsnorm(x, gain_ref[...]).astype(bf16)

    win_ref[0:HALO] = kvp_ref[0]
    win_ref[HALO:HALO + CHUNK] = kvc_ref[0]
    win_ref[HALO + CHUNK:CHUNK + 2 * HALO] = kvn_ref[0]

    lo_half, first_half = _lane_patterns()
    c, s = _rope_coeffs(cb_ref, sb_ref, off_ref, first_half)
    qa = jnp.dot(n_ref[...], wr_ref[:, R_QA:R_QA + A_WIDTH], preferred_element_type=f32)
    for t in range(A_WIDTH // LANES):
        sl = slice(t * LANES, (t + 1) * LANES)
        qa_ref[:, sl] = (_rope_tile(qa[:, sl], c, s, first_half) * scale).astype(bf16)
    qb = jnp.dot(n_ref[...], wr_ref[:, R_QB:R_QB + B_WIDTH], preferred_element_type=f32)
    qb_ref[...] = (qb * scale).astype(bf16)

    zero_pad = jnp.zeros((META_PAD - N_META, LANES), bf16)

    def meta_tile(col):
        return jnp.concatenate([kvm_ref[0, :, col:col + LANES], zero_pad], axis=0)

    def split_heads(q):
        zeros = jnp.zeros_like(q)
        return jnp.concatenate([jnp.where(lo_half, q, zeros), jnp.where(lo_half, zeros, q)], axis=0)

    def with_ones(v):
        return jnp.concatenate([v, jnp.ones_like(v)], axis=1)

    def proj_piece(k):
        cols = slice(k * PROJ_PIECE, (k + 1) * PROJ_PIECE)
        proj_ref[:, cols] = jnp.dot(n_ref[...], wr_ref[:, R_ZA + k * PROJ_PIECE:R_ZA + (k + 1) * PROJ_PIECE],
                                    preferred_element_type=f32)

    pieces = list(range((R_COLS - R_ZA) // PROJ_PIECE))

    def pipelined(items, scores, finish, n_fill):
        pending = scores(items[0])
        for t, item in enumerate(items):
            nxt = scores(items[t + 1]) if t + 1 < len(items) else None
            if t < n_fill and pieces:
                proj_piece(pieces.pop(0))
            finish(item, pending)
            pending = nxt

    a_tiles = A_GROUP // 2
    kmetas_a = [meta_tile(KV_KA + g * LANES) for g in range(A_KV_HEADS)]
    vmetas_a = [meta_tile(KV_VA + g * LANES) for g in range(A_KV_HEADS)]

    def a_window(col, tile, j):
        w0 = HALO + j * BLOCK - BLOCK
        return win_ref[w0:w0 + 3 * BLOCK, col + tile * LANES:col + (tile + 1) * LANES]

    def a_scores(item):
        g, j = item
        first = jnp.logical_and(i == 0, j == 0)
        last = jnp.logical_and(i == n_chunks - 1, j == BLOCKS_PER_CHUNK - 1)
        mask_t = amask_ref[jnp.where(first, 1, jnp.where(last, 2, 0))]
        kcat = jnp.concatenate([kmetas_a[g], a_window(KV_KA, g, j)], axis=0)
        k_aug = jnp.concatenate([kcat, mask_t], axis=1)
        q_tiles = [qa_ref[j * BLOCK:(j + 1) * BLOCK, (a_tiles * g + t) * LANES:(a_tiles * g + t + 1) * LANES]
                   for t in range(a_tiles)]
        zeros = jnp.zeros_like(q_tiles[0])
        qs = jnp.concatenate([jnp.where(lo_half, q, zeros) for q in q_tiles]
                             + [jnp.where(lo_half, zeros, q) for q in q_tiles], axis=0)
        q_aug = jnp.concatenate([qs, eye_ref[...]], axis=1)
        return lax.dot_general(q_aug, k_aug, _NT, preferred_element_type=f32)

    def a_finish(item, s):
        g, j = item
        ps, es = [], []
        for rb in range(A_GROUP):
            sink = sink_ref[A_GROUP * g + 2 * (rb % a_tiles) + rb // a_tiles]
            sh = s[rb * BLOCK:(rb + 1) * BLOCK]
            m = jnp.maximum(jnp.max(sh, axis=-1, keepdims=True), sink)
            ps.append(jnp.exp(sh - m).astype(bf16))
            es.append(jnp.exp(sink - m))
        vcat = jnp.concatenate([vmetas_a[g], a_window(KV_VA, g, j)], axis=0)
        o = jnp.dot(jnp.concatenate(ps, axis=0), with_ones(vcat), preferred_element_type=f32)
        for t in range(a_tiles):
            oe = o[t * BLOCK:(t + 1) * BLOCK]
            oo = o[(a_tiles + t) * BLOCK:(a_tiles + t + 1) * BLOCK]
            den = (jnp.where(lo_half, oe[:, LANES:], oo[:, LANES:])
                   + jnp.where(lo_half, es[t], es[a_tiles + t]))
            oa_ref[j * BLOCK:(j + 1) * BLOCK, (a_tiles * g + t) * LANES:(a_tiles * g + t + 1) * LANES] = (
                jnp.where(lo_half, oe[:, :LANES], oo[:, :LANES]) / den)

    pipelined([(g, j) for g in range(A_KV_HEADS) for j in range(BLOCKS_PER_CHUNK)],
              a_scores, a_finish, A_FILL)

    kmetas = [meta_tile(KV_KB + p * LANES) for p in range(B_HEADS // 2)]
    vmetas = [meta_tile(KV_VB + p * LANES) for p in range(B_HEADS // 2)]

    def unit_geometry(step):
        r0 = i * ROWS_PER_CHUNK + 2 * step
        rs0 = jnp.clip(r0 - NA_KH // 2, 0, n_rows - NA_KH)
        rs1 = jnp.clip(r0 + 1 - NA_KH // 2, 0, n_rows - NA_KH)
        shifts = (r0 - rs0, r0 + 1 - rs0)
        variant = 1 - (rs1 - rs0)
        woff = pl.multiple_of((rs0 - i * ROWS_PER_CHUNK) * GRID_W + HALO, BLOCK)
        return shifts, variant, woff

    geometry = [unit_geometry(step) for step in range(ROWS_PER_CHUNK // 2)]

    def b_scores(item):
        step, pairs = item
        _, variant, woff = geometry[step]
        out = []
        for p in pairs:
            kcat = jnp.concatenate(
                [win_ref[pl.ds(woff, B_WIN_KEYS), KV_KB + p * LANES:KV_KB + (p + 1) * LANES],
                 kmetas[p]], axis=0)
            k_aug = jnp.concatenate([kcat, brow_ref[variant]], axis=1)
            qs = split_heads(qb_ref[step * BLOCK:(step + 1) * BLOCK, p * LANES:(p + 1) * LANES])
            q_aug = jnp.concatenate([qs, bsel_ref[...]], axis=1)
            out.append(lax.dot_general(q_aug, k_aug, _NT, preferred_element_type=f32))
        return out

    def b_finish(item, scores):
        step, pairs = item
        shifts, _, woff = geometry[step]
        probs = []
        for p, s in zip(pairs, scores):
            bias = jnp.concatenate(
                [jnp.concatenate([bpair_ref[p, 2 * jj + NA_KH - 1 - shifts[u], hh]
                                  for jj in range(B_WIN_ROWS // 2)], axis=1)
                 for hh in range(2) for u in range(2)], axis=0)
            s_loc = s[:, :B_WIN_KEYS] + bias
            s_met = s[:, B_WIN_KEYS:]
            m = jnp.maximum(jnp.max(s_loc, axis=-1, keepdims=True),
                            jnp.max(s_met, axis=-1, keepdims=True))
            probs.append(jnp.concatenate([jnp.exp(s_loc - m), jnp.exp(s_met - m)],
                                         axis=1).astype(bf16))
        for p, pr in zip(pairs, probs):
            vcat = jnp.concatenate(
                [win_ref[pl.ds(woff, B_WIN_KEYS), KV_VB + p * LANES:KV_VB + (p + 1) * LANES],
                 vmetas[p]], axis=0)
            o = jnp.dot(pr, with_ones(vcat), preferred_element_type=f32)
            o = o[:, :LANES] / o[:, LANES:]
            ob_ref[step * BLOCK:(step + 1) * BLOCK, p * LANES:(p + 1) * LANES] = jnp.where(
                lo_half, o[:BLOCK], o[BLOCK:])

    pipelined([(step, tuple(range(p0, p0 + B_PAIRS_PER_ITEM)))
               for step in range(ROWS_PER_CHUNK // 2)
               for p0 in range(0, B_HEADS // 2, B_PAIRS_PER_ITEM)], b_scores, b_finish, B_FILL)

    while pieces:
        proj_piece(pieces.pop(0))

    def gated_branch(o_ref, z_col, w_ref, g_col):
        z = proj_ref[:, z_col - R_ZA:z_col - R_ZA + A_WIDTH]
        a = (o_ref[...] * (z * jax.nn.sigmoid(z))).astype(bf16)
        y = jnp.dot(a, w_ref[...], preferred_element_type=f32)
        return jax.nn.sigmoid(proj_ref[:, g_col - R_ZA:g_col - R_ZA + D_MODEL]) * y

    merged = gated_branch(oa_ref, R_ZA, wpa_ref, R_GA) + gated_branch(ob_ref, R_ZB, wpb_ref, R_GB)
    h = x + jnp.dot(merged.astype(bf16), wout_ref[...], preferred_element_type=f32)
    out_ref[0] = _rmsnorm(h, fgain_ref[...])


def _layer_call(sink, x, kv, kv_meta, rope_base, rope_off, gain, fgain, w_rest, w_pa, w_pb, w_out,
                amasks, btables):
    bsz, seq, _ = x.shape
    n_chunks = seq // CHUNK
    halo_per = CHUNK // HALO
    const2 = lambda b, i: (0, 0)
    const3 = lambda b, i: (0, 0, 0)
    const4 = lambda b, i: (0, 0, 0, 0)
    resident = dict(pipeline_mode=pl.Buffered(1))
    body = functools.partial(_layer_body, n_chunks=n_chunks, n_rows=seq // GRID_W)
    return pl.pallas_call(
        body,
        grid=(bsz, n_chunks),
        in_specs=[
            pl.BlockSpec(memory_space=pltpu.SMEM),
            pl.BlockSpec((1, CHUNK, D_MODEL), lambda b, i: (b, i, 0)),
            pl.BlockSpec((1, HALO, KV_COLS), lambda b, i: (b, jnp.maximum(i * halo_per - 1, 0), 0)),
            pl.BlockSpec((1, CHUNK, KV_COLS), lambda b, i: (b, i, 0)),
            pl.BlockSpec((1, HALO, KV_COLS),
                         lambda b, i: (b, jnp.minimum((i + 1) * halo_per, n_chunks * halo_per - 1), 0)),
            pl.BlockSpec((1, N_META, KV_COLS), const3, **resident),
            pl.BlockSpec((CHUNK, LANES), const2, **resident),
            pl.BlockSpec((CHUNK, LANES), const2, **resident),
            pl.BlockSpec((None, 2, LANES), lambda b, i: (i, 0, 0)),
            pl.BlockSpec((1, D_MODEL), const2, **resident),
            pl.BlockSpec((1, D_MODEL), const2, **resident),
            pl.BlockSpec((D_MODEL, R_COLS), const2, **resident),
            pl.BlockSpec((A_WIDTH, D_MODEL), const2, **resident),
            pl.BlockSpec((B_WIDTH, D_MODEL), const2, **resident),
            pl.BlockSpec((D_MODEL, D_MODEL), const2, **resident),
            pl.BlockSpec((3, META_PAD + 3 * BLOCK, BLOCK), const3, **resident),
            pl.BlockSpec((A_GROUP * BLOCK, BLOCK), const2, **resident),
            pl.BlockSpec((B_HEADS // 2, 2 * NA_KH, 2, GRID_W, LANES), lambda b, i: (0, 0, 0, 0, 0),
                         **resident),
            pl.BlockSpec((2, B_WIN_KEYS + META_PAD, LANES), const3, **resident),
            pl.BlockSpec((2 * BLOCK, LANES), const2, **resident),
        ],
        out_specs=pl.BlockSpec((1, CHUNK, D_MODEL), lambda b, i: (b, i, 0)),
        out_shape=jax.ShapeDtypeStruct((bsz, seq, D_MODEL), jnp.float32),
        scratch_shapes=[
            pltpu.VMEM((CHUNK, D_MODEL), jnp.bfloat16),
            pltpu.VMEM((CHUNK + 2 * HALO, KV_COLS), jnp.bfloat16),
            pltpu.VMEM((CHUNK, A_WIDTH), jnp.bfloat16),
            pltpu.VMEM((CHUNK, B_WIDTH), jnp.bfloat16),
            pltpu.VMEM((CHUNK, A_WIDTH), jnp.float32),
            pltpu.VMEM((CHUNK, B_WIDTH), jnp.float32),
            pltpu.VMEM((CHUNK, R_COLS - R_ZA), jnp.float32),
        ],
        compiler_params=pltpu.CompilerParams(
            dimension_semantics=("parallel", "arbitrary"), vmem_limit_bytes=VMEM_LIMIT_BYTES),
        name="hybrid_layer",
    )(sink, x, kv, kv, kv, kv_meta, *rope_base, rope_off, gain, fgain, w_rest, w_pa, w_pb, w_out,
      *amasks, *btables)


def _rope_tables(n_chunks):
    inv_freq = ROPE_THETA ** (-jnp.arange(ROT_HALF, dtype=jnp.float32) / ROT_HALF)
    d = np.arange(LANES) % HEAD_DIM
    inv_lane = jnp.where(d < ROT_DIM, inv_freq[d % ROT_HALF], 0.0)[None, :]
    base = jnp.arange(CHUNK, dtype=jnp.float32)[:, None] * inv_lane
    off = (N_META + CHUNK * jnp.arange(n_chunks, dtype=jnp.float32))[:, None] * inv_lane
    off_real = jnp.stack([jnp.cos(off), jnp.sin(off)], axis=1)
    off_meta = jnp.stack([jnp.ones((1, LANES), jnp.float32), jnp.zeros((1, LANES), jnp.float32)], axis=1)
    return (jnp.cos(base), jnp.sin(base)), off_meta, off_real


def _band_masks():
    qi = np.arange(BLOCK)[:, None]
    col = np.arange(META_PAD + 3 * BLOCK)[None, :]
    kj = col - META_PAD
    band = (col >= META_PAD) & (np.abs(kj - BLOCK - qi) <= WINDOW)
    meta = np.broadcast_to(col < N_META, band.shape)
    variants = [meta | band, meta | (band & (kj >= BLOCK)), meta | (band & (kj < 2 * BLOCK))]
    masks = np.where(np.stack(variants), 0.0, NEG_INF).astype(np.float32).transpose(0, 2, 1)
    eye = np.tile(np.eye(BLOCK, dtype=np.float32), (A_GROUP, 1))
    return jnp.asarray(masks, jnp.bfloat16), jnp.asarray(eye, jnp.bfloat16)


def _na_bias(rpb):
    cq = np.arange(GRID_W)[:, None]
    kc = np.arange(GRID_W)[None, :]
    start = np.clip(cq - NA_KW // 2, 0, GRID_W - NA_KW)
    valid = (kc >= start) & (kc < start + NA_KW)
    onehot = ((kc - cq + NA_KW - 1)[..., None] == np.arange(2 * NA_KW - 1)) & valid[..., None]
    n_pairs, n_dr = B_HEADS // 2, 2 * NA_KH - 1
    t15 = jnp.einsum('phrd,ckd->prhck', rpb.reshape(n_pairs, 2, n_dr, 2 * NA_KW - 1),
                     jnp.asarray(onehot, jnp.float32), precision=lax.Precision.HIGHEST)
    t15 = jnp.pad(t15, ((0, 0), (0, 2 * NA_KH + 1 - n_dr), (0, 0), (0, 0), (0, 0)))
    t15 = jnp.where(valid[None, None, None], t15, NEG_INF)
    pair = jnp.concatenate([t15[:, :2 * NA_KH], t15[:, 1:]], axis=-1)

    key_row = np.arange(B_WIN_KEYS + META_PAD) // GRID_W
    is_meta = (np.arange(B_WIN_KEYS + META_PAD) >= B_WIN_KEYS) & (
        np.arange(B_WIN_KEYS + META_PAD) < B_WIN_KEYS + N_META)
    rows = np.zeros((2, B_WIN_KEYS + META_PAD, LANES), np.float32)
    for variant, starts in enumerate(((0, 1), (0, 0))):
        for u, s0 in enumerate(starts):
            seen = ((key_row >= s0) & (key_row < s0 + NA_KH)) | is_meta
            rows[variant, :, u] = np.where(seen, 0.0, NEG_INF)
    sel = np.zeros((2, 2, GRID_W, LANES), np.float32)
    sel[:, 0, :, 0] = 1.0
    sel[:, 1, :, 1] = 1.0
    return (pair, jnp.asarray(rows, jnp.bfloat16),
            jnp.asarray(sel.reshape(2 * BLOCK, LANES), jnp.bfloat16))


def kernel(x, meta_tokens, norm_gain, w_in, sink_logits, rel_pos_bias, w_proj_a, w_proj_b, w_out,
           final_norm_gain):
    bsz, seq, _ = x.shape
    assert seq % CHUNK == 0 and seq // GRID_W >= NA_KH and norm_gain.shape[0] == 1
    bf16 = jnp.bfloat16
    w = w_in[0].astype(bf16)
    qa, ka, va, za, qb, kb, vb, zb, ga, gb = jnp.split(w, list(np.cumsum(SPLIT_SIZES)[:-1]), axis=1)
    w_kv = jnp.concatenate([ka, va, kb, vb], axis=1)
    w_rest = jnp.concatenate([qa, qb, za, zb, ga, gb], axis=1)

    rope_base, off_meta, off_real = _rope_tables(seq // CHUNK)
    rope_base_meta = tuple(t[:N_META] for t in rope_base)

    gain = norm_gain[0][None]
    kv_meta = _kv_call(meta_tokens[None], gain, w_kv, rope_base_meta, off_meta, N_META)
    kv = _kv_call(x, gain, w_kv, rope_base, off_real, CHUNK)
    return _layer_call(sink_logits[0], x, kv, kv_meta, rope_base, off_real, gain,
                       final_norm_gain[None], w_rest, w_proj_a[0].astype(bf16),
                       w_proj_b[0].astype(bf16), w_out[0].astype(bf16), _band_masks(),
                       _na_bias(rel_pos_bias[0]))
```

```python
import functools

import numpy as np
import jax
import jax.numpy as jnp
from jax import lax
from jax.experimental import pallas as pl
from jax.experimental.pallas import tpu as pltpu

D_MODEL = 1024
N_META = 16
HEAD_DIM = 64
RMS_EPS = 1e-6
NEG_INF = -1e30
A_HEADS = 8
A_KV_HEADS = 2
A_GROUP = A_HEADS // A_KV_HEADS
A_WIDTH = A_HEADS * HEAD_DIM
A_KV_WIDTH = A_KV_HEADS * HEAD_DIM
WINDOW = 128
BLOCK = 128
ROT_DIM = HEAD_DIM // 4
ROT_HALF = ROT_DIM // 2
ROPE_THETA = 500000.0
B_HEADS = 8
B_WIDTH = B_HEADS * HEAD_DIM
GRID_W = 64
NA_KH = 8
NA_KW = 16
SPLIT_SIZES = (A_WIDTH, A_KV_WIDTH, A_KV_WIDTH, A_WIDTH, B_WIDTH, B_WIDTH, B_WIDTH, B_WIDTH,
               D_MODEL, D_MODEL)

LANES = 128
CHUNK = 512
ROWS_PER_CHUNK = CHUNK // GRID_W
BLOCKS_PER_CHUNK = CHUNK // BLOCK
HALO = CHUNK // 2
assert HALO >= BLOCK and HALO >= (NA_KH // 2) * GRID_W
META_PAD = LANES
B_WIN_ROWS = NA_KH + 2
B_WIN_KEYS = B_WIN_ROWS * GRID_W
B_PAIRS_PER_ITEM = 2
PROJ_PIECE = 256
FILL_PLAN = (1, 1, 1, 1, 1, 1, 1, 1, 1, 1, 1, 1, 0, 0, 0, 0)

KV_KA = 0
KV_VA = KV_KA + A_KV_HEADS * LANES
KV_KB = KV_VA + A_KV_HEADS * LANES
KV_VB = KV_KB + B_WIDTH
KV_COLS = KV_VB + B_WIDTH

W_KA = 0
W_VA = W_KA + A_KV_WIDTH
W_KB = W_VA + A_KV_WIDTH
W_KV_COLS = W_KB + 2 * B_WIDTH
assert A_KV_WIDTH == LANES

R_QA = 0
R_QB = R_QA + A_WIDTH
R_ZA = R_QB + B_WIDTH
R_ZB = R_ZA + A_WIDTH
R_GA = R_ZB + B_WIDTH
R_GB = R_GA + D_MODEL
R_COLS = R_GB + D_MODEL

VMEM_LIMIT_BYTES = 58 * 1024 * 1024

_NT = (((1,), (1,)), ((), ()))


def _rmsnorm(x, gain):
    return x * lax.rsqrt(jnp.mean(x * x, axis=-1, keepdims=True) + RMS_EPS) * gain


def _lane_patterns():
    lane = lax.broadcasted_iota(jnp.int32, (1, LANES), 1)
    return lane < HEAD_DIM, (lane & (HEAD_DIM - 1)) < ROT_HALF


def _rope_coeffs(cb_ref, sb_ref, off_ref, first_half):
    cb, sb = cb_ref[...], sb_ref[...]
    ca, sa = off_ref[0:1, :], off_ref[1:2, :]
    c = cb * ca - sb * sa
    s = (sb * ca + cb * sa) * jnp.where(first_half, -1.0, 1.0)
    return c, s


def _rope_tile(t, c, s, first_half):
    partner = jnp.where(first_half, pltpu.roll(t, LANES - ROT_HALF, 1), pltpu.roll(t, ROT_HALF, 1))
    return t * c + partner * s


def _kv_body(x_ref, gain_ref, w_ref, cb_ref, sb_ref, off_ref, kv_ref):
    bf16 = jnp.bfloat16
    lo_half, first_half = _lane_patterns()
    n = _rmsnorm(x_ref[0], gain_ref[...]).astype(bf16)
    kv = jnp.dot(n, w_ref[...], preferred_element_type=jnp.float32)
    c, s = _rope_coeffs(cb_ref, sb_ref, off_ref, first_half)
    ka = _rope_tile(kv[:, W_KA:W_KA + LANES], c, s, first_half)
    va = kv[:, W_VA:W_VA + LANES]
    for src, col in ((ka, KV_KA), (va, KV_VA)):
        swapped = pltpu.roll(src, HEAD_DIM, 1)
        kv_ref[0, :, col:col + LANES] = jnp.where(lo_half, src, swapped).astype(bf16)
        kv_ref[0, :, col + LANES:col + 2 * LANES] = jnp.where(lo_half, swapped, src).astype(bf16)
    kv_ref[0, :, KV_KB:] = kv[:, W_KB:].astype(bf16)


def _kv_call(x, gain, w_kv, rope_base, rope_off, rows):
    bsz, seq, _ = x.shape
    const = lambda b, i: (0, 0)
    return pl.pallas_call(
        _kv_body,
        grid=(bsz, seq // rows),
        in_specs=[
            pl.BlockSpec((1, rows, D_MODEL), lambda b, i: (b, i, 0)),
            pl.BlockSpec((1, D_MODEL), const),
            pl.BlockSpec((D_MODEL, W_KV_COLS), const, pipeline_mode=pl.Buffered(1)),
            pl.BlockSpec((rows, LANES), const),
            pl.BlockSpec((rows, LANES), const),
            pl.BlockSpec((None, 2, LANES), lambda b, i: (i, 0, 0)),
        ],
        out_specs=pl.BlockSpec((1, rows, KV_COLS), lambda b, i: (b, i, 0)),
        out_shape=jax.ShapeDtypeStruct((bsz, seq, KV_COLS), jnp.bfloat16),
        compiler_params=pltpu.CompilerParams(
            dimension_semantics=("parallel", "parallel"), vmem_limit_bytes=VMEM_LIMIT_BYTES),
        name="kv_proj",
    )(x, gain, w_kv, *rope_base, rope_off)


def _layer_body(sink_ref, x_ref, kvp_ref, kvc_ref, kvn_ref, kvm_ref, cb_ref, sb_ref, off_ref,
                gain_ref, fgain_ref, wr_ref, wpa_ref, wpb_ref, wout_ref, amask_ref, eye_ref, bpair_ref,
                brow_ref, bsel_ref,
                out_ref, n_ref, win_ref, qa_ref, qb_ref, oa_ref, ob_ref, proj_ref, *, n_chunks, n_rows):
    i = pl.program_id(1)
    f32, bf16 = jnp.float32, jnp.bfloat16
    scale = HEAD_DIM ** -0.5

    x = x_ref[0]
    n_ref[...] = _rmsnorm(x, gain_ref[...]).astype(bf16)

    win_ref[0:HALO] = kvp_ref[0]
    win_ref[HALO:HALO + CHUNK] = kvc_ref[0]
    win_ref[HALO + CHUNK:CHUNK + 2 * HALO] = kvn_ref[0]

    lo_half, first_half = _lane_patterns()
    c, s = _rope_coeffs(cb_ref, sb_ref, off_ref, first_half)
    qa = jnp.dot(n_ref[...], wr_ref[:, R_QA:R_QA + A_WIDTH], preferred_element_type=f32)
    for t in range(A_WIDTH // LANES):
        sl = slice(t * LANES, (t + 1) * LANES)
        qa_ref[:, sl] = (_rope_tile(qa[:, sl], c, s, first_half) * scale).astype(bf16)
    qb = jnp.dot(n_ref[...], wr_ref[:, R_QB:R_QB + B_WIDTH], preferred_element_type=f32)
    qb_ref[...] = (qb * scale).astype(bf16)

    zero_pad = jnp.zeros((META_PAD - N_META, LANES), bf16)

    def meta_tile(col):
        return jnp.concatenate([kvm_ref[0, :, col:col + LANES], zero_pad], axis=0)

    def split_heads(q):
        zeros = jnp.zeros_like(q)
        return jnp.concatenate([jnp.where(lo_half, q, zeros), jnp.where(lo_half, zeros, q)], axis=0)

    def with_ones(v):
        return jnp.concatenate([v, jnp.ones_like(v)], axis=1)

    def proj_piece(k):
        cols = slice(k * PROJ_PIECE, (k + 1) * PROJ_PIECE)
        proj_ref[:, cols] = jnp.dot(n_ref[...], wr_ref[:, R_ZA + k * PROJ_PIECE:R_ZA + (k + 1) * PROJ_PIECE],
                                    preferred_element_type=f32)

    pieces = list(range((R_COLS - R_ZA) // PROJ_PIECE))

    def pipelined(stages, fill_plan):
        pending = stages[0][0](stages[0][2])
        for t, (_, finish, item) in enumerate(stages):
            nxt = stages[t + 1][0](stages[t + 1][2]) if t + 1 < len(stages) else None
            for _ in range(fill_plan[t]):
                proj_piece(pieces.pop(0))
            finish(item, pending)
            pending = nxt

    a_tiles = A_GROUP // 2
    kmetas_a = [meta_tile(KV_KA + g * LANES) for g in range(A_KV_HEADS)]
    vmetas_a = [meta_tile(KV_VA + g * LANES) for g in range(A_KV_HEADS)]

    def a_window(col, tile, j):
        w0 = HALO + j * BLOCK - BLOCK
        return win_ref[w0:w0 + 3 * BLOCK, col + tile * LANES:col + (tile + 1) * LANES]

    def a_scores(item):
        g, j = item
        first = jnp.logical_and(i == 0, j == 0)
        last = jnp.logical_and(i == n_chunks - 1, j == BLOCKS_PER_CHUNK - 1)
        mask_t = amask_ref[jnp.where(first, 1, jnp.where(last, 2, 0))]
        kcat = jnp.concatenate([kmetas_a[g], a_window(KV_KA, g, j)], axis=0)
        k_aug = jnp.concatenate([kcat, mask_t], axis=1)
        q_tiles = [qa_ref[j * BLOCK:(j + 1) * BLOCK, (a_tiles * g + t) * LANES:(a_tiles * g + t + 1) * LANES]
                   for t in range(a_tiles)]
        zeros = jnp.zeros_like(q_tiles[0])
        qs = jnp.concatenate([jnp.where(lo_half, q, zeros) for q in q_tiles]
                             + [jnp.where(lo_half, zeros, q) for q in q_tiles], axis=0)
        q_aug = jnp.concatenate([qs, eye_ref[...]], axis=1)
        return lax.dot_general(q_aug, k_aug, _NT, preferred_element_type=f32)

    def a_finish(item, s):
        g, j = item
        ps, es = [], []
        for rb in range(A_GROUP):
            sink = sink_ref[A_GROUP * g + 2 * (rb % a_tiles) + rb // a_tiles]
            sh = s[rb * BLOCK:(rb + 1) * BLOCK]
            m = jnp.maximum(jnp.max(sh, axis=-1, keepdims=True), sink)
            ps.append(jnp.exp(sh - m).astype(bf16))
            es.append(jnp.exp(sink - m))
        vcat = jnp.concatenate([vmetas_a[g], a_window(KV_VA, g, j)], axis=0)
        o = jnp.dot(jnp.concatenate(ps, axis=0), with_ones(vcat), preferred_element_type=f32)
        for t in range(a_tiles):
            oe = o[t * BLOCK:(t + 1) * BLOCK]
            oo = o[(a_tiles + t) * BLOCK:(a_tiles + t + 1) * BLOCK]
            den = (jnp.where(lo_half, oe[:, LANES:], oo[:, LANES:])
                   + jnp.where(lo_half, es[t], es[a_tiles + t]))
            oa_ref[j * BLOCK:(j + 1) * BLOCK, (a_tiles * g + t) * LANES:(a_tiles * g + t + 1) * LANES] = (
                jnp.where(lo_half, oe[:, :LANES], oo[:, :LANES]) / den)

    a_stages = [(a_scores, a_finish, (g, j)) for g in range(A_KV_HEADS) for j in range(BLOCKS_PER_CHUNK)]

    kmetas = [meta_tile(KV_KB + p * LANES) for p in range(B_HEADS // 2)]
    vmetas = [meta_tile(KV_VB + p * LANES) for p in range(B_HEADS // 2)]

    def unit_geometry(step):
        r0 = i * ROWS_PER_CHUNK + 2 * step
        rs0 = jnp.clip(r0 - NA_KH // 2, 0, n_rows - NA_KH)
        rs1 = jnp.clip(r0 + 1 - NA_KH // 2, 0, n_rows - NA_KH)
        shifts = (r0 - rs0, r0 + 1 - rs0)
        variant = 1 - (rs1 - rs0)
        woff = pl.multiple_of((rs0 - i * ROWS_PER_CHUNK) * GRID_W + HALO, BLOCK)
        return shifts, variant, woff

    geometry = [unit_geometry(step) for step in range(ROWS_PER_CHUNK // 2)]

    def b_scores(item):
        step, pairs = item
        _, variant, woff = geometry[step]
        out = []
        for p in pairs:
            kcat = jnp.concatenate(
                [win_ref[pl.ds(woff, B_WIN_KEYS), KV_KB + p * LANES:KV_KB + (p + 1) * LANES],
                 kmetas[p]], axis=0)
            k_aug = jnp.concatenate([kcat, brow_ref[variant]], axis=1)
            qs = split_heads(qb_ref[step * BLOCK:(step + 1) * BLOCK, p * LANES:(p + 1) * LANES])
            q_aug = jnp.concatenate([qs, bsel_ref[...]], axis=1)
            out.append(lax.dot_general(q_aug, k_aug, _NT, preferred_element_type=f32))
        return out

    def b_finish(item, scores):
        step, pairs = item
        shifts, _, woff = geometry[step]
        probs = []
        for p, s in zip(pairs, scores):
            bias = jnp.concatenate(
                [jnp.concatenate([bpair_ref[p, 2 * jj + NA_KH - 1 - shifts[u], hh]
                                  for jj in range(B_WIN_ROWS // 2)], axis=1)
                 for hh in range(2) for u in range(2)], axis=0)
            s_loc = s[:, :B_WIN_KEYS] + bias
            s_met = s[:, B_WIN_KEYS:]
            m = jnp.maximum(jnp.max(s_loc, axis=-1, keepdims=True),
                            jnp.max(s_met, axis=-1, keepdims=True))
            probs.append(jnp.concatenate([jnp.exp(s_loc - m), jnp.exp(s_met - m)],
                                         axis=1).astype(bf16))
        for p, pr in zip(pairs, probs):
            vcat = jnp.concatenate(
                [win_ref[pl.ds(woff, B_WIN_KEYS), KV_VB + p * LANES:KV_VB + (p + 1) * LANES],
                 vmetas[p]], axis=0)
            o = jnp.dot(pr, with_ones(vcat), preferred_element_type=f32)
            o = o[:, :LANES] / o[:, LANES:]
            ob_ref[step * BLOCK:(step + 1) * BLOCK, p * LANES:(p + 1) * LANES] = jnp.where(
                lo_half, o[:BLOCK], o[BLOCK:])

    b_stages = [(b_scores, b_finish, (step, tuple(range(p0, p0 + B_PAIRS_PER_ITEM))))
                for step in range(ROWS_PER_CHUNK // 2)
                for p0 in range(0, B_HEADS // 2, B_PAIRS_PER_ITEM)]
    assert len(a_stages) + len(b_stages) == len(FILL_PLAN) and sum(FILL_PLAN) <= len(pieces)
    pipelined(a_stages + b_stages, FILL_PLAN)

    while pieces:
        proj_piece(pieces.pop(0))

    def gated_branch(o_ref, z_col, w_ref, g_col):
        z = proj_ref[:, z_col - R_ZA:z_col - R_ZA + A_WIDTH]
        a = (o_ref[...] * (z * jax.nn.sigmoid(z))).astype(bf16)
        y = jnp.dot(a, w_ref[...], preferred_element_type=f32)
        return jax.nn.sigmoid(proj_ref[:, g_col - R_ZA:g_col - R_ZA + D_MODEL]) * y

    merged = gated_branch(oa_ref, R_ZA, wpa_ref, R_GA) + gated_branch(ob_ref, R_ZB, wpb_ref, R_GB)
    h = x + jnp.dot(merged.astype(bf16), wout_ref[...], preferred_element_type=f32)
    out_ref[0] = _rmsnorm(h, fgain_ref[...])


def _layer_call(sink, x, kv, kv_meta, rope_base, rope_off, gain, fgain, w_rest, w_pa, w_pb, w_out,
                amasks, btables):
    bsz, seq, _ = x.shape
    n_chunks = seq // CHUNK
    halo_per = CHUNK // HALO
    const2 = lambda b, i: (0, 0)
    const3 = lambda b, i: (0, 0, 0)
    const4 = lambda b, i: (0, 0, 0, 0)
    resident = dict(pipeline_mode=pl.Buffered(1))
    body = functools.partial(_layer_body, n_chunks=n_chunks, n_rows=seq // GRID_W)
    return pl.pallas_call(
        body,
        grid=(bsz, n_chunks),
        in_specs=[
            pl.BlockSpec(memory_space=pltpu.SMEM),
            pl.BlockSpec((1, CHUNK, D_MODEL), lambda b, i: (b, i, 0)),
            pl.BlockSpec((1, HALO, KV_COLS), lambda b, i: (b, jnp.maximum(i * halo_per - 1, 0), 0)),
            pl.BlockSpec((1, CHUNK, KV_COLS), lambda b, i: (b, i, 0)),
            pl.BlockSpec((1, HALO, KV_COLS),
                         lambda b, i: (b, jnp.minimum((i + 1) * halo_per, n_chunks * halo_per - 1), 0)),
            pl.BlockSpec((1, N_META, KV_COLS), const3, **resident),
            pl.BlockSpec((CHUNK, LANES), const2, **resident),
            pl.BlockSpec((CHUNK, LANES), const2, **resident),
            pl.BlockSpec((None, 2, LANES), lambda b, i: (i, 0, 0)),
            pl.BlockSpec((1, D_MODEL), const2, **resident),
            pl.BlockSpec((1, D_MODEL), const2, **resident),
            pl.BlockSpec((D_MODEL, R_COLS), const2, **resident),
            pl.BlockSpec((A_WIDTH, D_MODEL), const2, **resident),
            pl.BlockSpec((B_WIDTH, D_MODEL), const2, **resident),
            pl.BlockSpec((D_MODEL, D_MODEL), const2, **resident),
            pl.BlockSpec((3, META_PAD + 3 * BLOCK, BLOCK), const3, **resident),
            pl.BlockSpec((A_GROUP * BLOCK, BLOCK), const2, **resident),
            pl.BlockSpec((B_HEADS // 2, 2 * NA_KH, 2, GRID_W, LANES), lambda b, i: (0, 0, 0, 0, 0),
                         **resident),
            pl.BlockSpec((2, B_WIN_KEYS + META_PAD, LANES), const3, **resident),
            pl.BlockSpec((2 * BLOCK, LANES), const2, **resident),
        ],
        out_specs=pl.BlockSpec((1, CHUNK, D_MODEL), lambda b, i: (b, i, 0)),
        out_shape=jax.ShapeDtypeStruct((bsz, seq, D_MODEL), jnp.float32),
        scratch_shapes=[
            pltpu.VMEM((CHUNK, D_MODEL), jnp.bfloat16),
            pltpu.VMEM((CHUNK + 2 * HALO, KV_COLS), jnp.bfloat16),
            pltpu.VMEM((CHUNK, A_WIDTH), jnp.bfloat16),
            pltpu.VMEM((CHUNK, B_WIDTH), jnp.bfloat16),
            pltpu.VMEM((CHUNK, A_WIDTH), jnp.float32),
            pltpu.VMEM((CHUNK, B_WIDTH), jnp.float32),
            pltpu.VMEM((CHUNK, R_COLS - R_ZA), jnp.float32),
        ],
        compiler_params=pltpu.CompilerParams(
            dimension_semantics=("parallel", "arbitrary"), vmem_limit_bytes=VMEM_LIMIT_BYTES),
        name="hybrid_layer",
    )(sink, x, kv, kv, kv, kv_meta, *rope_base, rope_off, gain, fgain, w_rest, w_pa, w_pb, w_out,
      *amasks, *btables)


def _rope_tables(n_chunks):
    d = np.arange(LANES) % HEAD_DIM
    inv_freq = ROPE_THETA ** (-jnp.asarray(d % ROT_HALF, jnp.float32) / ROT_HALF)
    inv_lane = jnp.where(d < ROT_DIM, inv_freq, 0.0)[None, :]
    base = jnp.arange(CHUNK, dtype=jnp.float32)[:, None] * inv_lane
    off = (N_META + CHUNK * jnp.arange(n_chunks, dtype=jnp.float32))[:, None] * inv_lane
    off_real = jnp.stack([jnp.cos(off), jnp.sin(off)], axis=1)
    off_meta = jnp.stack([jnp.ones((1, LANES), jnp.float32), jnp.zeros((1, LANES), jnp.float32)], axis=1)
    return (jnp.cos(base), jnp.sin(base)), off_meta, off_real


def _band_masks():
    qi = np.arange(BLOCK)[:, None]
    col = np.arange(META_PAD + 3 * BLOCK)[None, :]
    kj = col - META_PAD
    band = (col >= META_PAD) & (np.abs(kj - BLOCK - qi) <= WINDOW)
    meta = np.broadcast_to(col < N_META, band.shape)
    variants = [meta | band, meta | (band & (kj >= BLOCK)), meta | (band & (kj < 2 * BLOCK))]
    masks = np.where(np.stack(variants), 0.0, NEG_INF).astype(np.float32).transpose(0, 2, 1)
    eye = np.tile(np.eye(BLOCK, dtype=np.float32), (A_GROUP, 1))
    return jnp.asarray(masks, jnp.bfloat16), jnp.asarray(eye, jnp.bfloat16)


def _na_bias(rpb):
    cq = np.arange(GRID_W)[:, None]
    kc = np.arange(GRID_W)[None, :]
    start = np.clip(cq - NA_KW // 2, 0, GRID_W - NA_KW)
    valid = (kc >= start) & (kc < start + NA_KW)
    n_pairs, n_dr, n_dc = B_HEADS // 2, 2 * NA_KH - 1, 2 * NA_KW - 1
    onehot = ((kc - cq + NA_KW - 1)[..., None] == np.arange(n_dc)) & valid[..., None]
    select = np.zeros((GRID_W, 2, n_dc, 2, GRID_W), np.float32)
    for t in range(2):
        select[:, t, :, t, :] = onehot.transpose(0, 2, 1)
    rows = jnp.pad(rpb.reshape(n_pairs, 2, n_dr, n_dc), ((0, 0), (0, 0), (0, 2), (0, 0)))
    rows = jnp.stack([rows[:, :, :2 * NA_KH], rows[:, :, 1:]], axis=3)
    pair = jnp.einsum('phre,cen->prhcn', rows.reshape(n_pairs, 2, 2 * NA_KH, 2 * n_dc),
                      jnp.asarray(select.reshape(GRID_W, 2 * n_dc, LANES)),
                      precision=lax.Precision.HIGHEST)
    pair = jnp.where(np.tile(valid, (1, 2))[None, None, None], pair, NEG_INF)

    key_row = np.arange(B_WIN_KEYS + META_PAD) // GRID_W
    is_meta = (np.arange(B_WIN_KEYS + META_PAD) >= B_WIN_KEYS) & (
        np.arange(B_WIN_KEYS + META_PAD) < B_WIN_KEYS + N_META)
    rows = np.zeros((2, B_WIN_KEYS + META_PAD, LANES), np.float32)
    for variant, starts in enumerate(((0, 1), (0, 0))):
        for u, s0 in enumerate(starts):
            seen = ((key_row >= s0) & (key_row < s0 + NA_KH)) | is_meta
            rows[variant, :, u] = np.where(seen, 0.0, NEG_INF)
    sel = np.zeros((2, 2, GRID_W, LANES), np.float32)
    sel[:, 0, :, 0] = 1.0
    sel[:, 1, :, 1] = 1.0
    return (pair, jnp.asarray(rows, jnp.bfloat16),
            jnp.asarray(sel.reshape(2 * BLOCK, LANES), jnp.bfloat16))


def kernel(x, meta_tokens, norm_gain, w_in, sink_logits, rel_pos_bias, w_proj_a, w_proj_b, w_out,
           final_norm_gain):
    bsz, seq, _ = x.shape
    assert seq % CHUNK == 0 and seq // GRID_W >= NA_KH and norm_gain.shape[0] == 1
    bf16 = jnp.bfloat16
    w = w_in[0].astype(bf16)
    qa, ka, va, za, qb, kb, vb, zb, ga, gb = jnp.split(w, list(np.cumsum(SPLIT_SIZES)[:-1]), axis=1)
    w_kv = jnp.concatenate([ka, va, kb, vb], axis=1)
    w_rest = jnp.concatenate([qa, qb, za, zb, ga, gb], axis=1)

    rope_base, off_meta, off_real = _rope_tables(seq // CHUNK)
    rope_base_meta = tuple(t[:N_META] for t in rope_base)

    gain = norm_gain[0][None]
    kv_meta = _kv_call(meta_tokens[None], gain, w_kv, rope_base_meta, off_meta, N_META)
    kv = _kv_call(x, gain, w_kv, rope_base, off_real, CHUNK)
    return _layer_call(sink_logits[0], x, kv, kv_meta, rope_base, off_real, gain,
                       final_norm_gain[None], w_rest, w_proj_a[0].astype(bf16),
                       w_proj_b[0].astype(bf16), w_out[0].astype(bf16), _band_masks(),
                       _na_bias(rel_pos_bias[0]))
```

```python
import functools

import numpy as np
import jax
import jax.numpy as jnp
from jax import lax
from jax.experimental import pallas as pl
from jax.experimental.pallas import tpu as pltpu

D_MODEL = 1024
N_META = 16
HEAD_DIM = 64
RMS_EPS = 1e-6
NEG_INF = -1e30
A_HEADS = 8
A_KV_HEADS = 2
A_GROUP = A_HEADS // A_KV_HEADS
A_WIDTH = A_HEADS * HEAD_DIM
A_KV_WIDTH = A_KV_HEADS * HEAD_DIM
WINDOW = 128
BLOCK = 128
ROT_DIM = HEAD_DIM // 4
ROT_HALF = ROT_DIM // 2
ROPE_THETA = 500000.0
B_HEADS = 8
B_WIDTH = B_HEADS * HEAD_DIM
GRID_W = 64
NA_KH = 8
NA_KW = 16
SPLIT_SIZES = (A_WIDTH, A_KV_WIDTH, A_KV_WIDTH, A_WIDTH, B_WIDTH, B_WIDTH, B_WIDTH, B_WIDTH,
               D_MODEL, D_MODEL)

LANES = 128
CHUNK = 512
ROWS_PER_CHUNK = CHUNK // GRID_W
BLOCKS_PER_CHUNK = CHUNK // BLOCK
HALO = CHUNK // 2
assert HALO >= BLOCK and HALO >= (NA_KH // 2) * GRID_W
META_PAD = LANES
B_WIN_ROWS = NA_KH + 2
B_WIN_KEYS = B_WIN_ROWS * GRID_W
B_PAIRS_PER_ITEM = 2
PROJ_PIECE = 256
FILL_PLAN = (1, 1, 1, 1, 0, 0, 0, 0, 1, 1, 1, 1, 1, 1, 1, 1)

KV_KA = 0
KV_VA = KV_KA + A_KV_HEADS * LANES
KV_KB = KV_VA + A_KV_HEADS * LANES
KV_VB = KV_KB + B_WIDTH
KV_COLS = KV_VB + B_WIDTH

(C_QA, C_KA, C_VA, C_ZA, C_QB, C_KB, C_VB, C_ZB, C_GA, C_GB, IN_COLS) = (
    int(c) for c in np.cumsum((0,) + SPLIT_SIZES))
assert A_KV_WIDTH == LANES and C_VA == C_KA + A_KV_WIDTH and C_VB == C_KB + B_WIDTH

P_ZA = 0
P_ZB = P_ZA + A_WIDTH
P_GA = P_ZB + B_WIDTH
P_GB = P_GA + D_MODEL
P_COLS = P_GB + D_MODEL
PROJ_SOURCES = tuple(
    src + off for src, width in ((C_ZA, A_WIDTH), (C_ZB, B_WIDTH), (C_GA, D_MODEL), (C_GB, D_MODEL))
    for off in range(0, width, PROJ_PIECE))

VMEM_LIMIT_BYTES = 58 * 1024 * 1024

_NT = (((1,), (1,)), ((), ()))


def _rmsnorm(x, gain):
    return x * lax.rsqrt(jnp.mean(x * x, axis=-1, keepdims=True) + RMS_EPS) * gain


def _lane_patterns():
    lane = lax.broadcasted_iota(jnp.int32, (1, LANES), 1)
    return lane < HEAD_DIM, (lane & (HEAD_DIM - 1)) < ROT_HALF


def _rope_coeffs(cb_ref, sb_ref, off_ref, first_half):
    cb, sb = cb_ref[...], sb_ref[...]
    ca, sa = off_ref[0:1, :], off_ref[1:2, :]
    c = cb * ca - sb * sa
    s = (sb * ca + cb * sa) * jnp.where(first_half, -1.0, 1.0)
    return c, s


def _rope_tile(t, c, s, first_half):
    partner = jnp.where(first_half, pltpu.roll(t, LANES - ROT_HALF, 1), pltpu.roll(t, ROT_HALF, 1))
    return t * c + partner * s


def _kv_body(x_ref, gain_ref, w_ref, cb_ref, sb_ref, off_ref, kv_ref):
    bf16 = jnp.bfloat16
    lo_half, first_half = _lane_patterns()
    n = _rmsnorm(x_ref[0], gain_ref[...]).astype(bf16)
    kv_a = jnp.dot(n, w_ref[:, C_KA:C_KA + 2 * A_KV_WIDTH], preferred_element_type=jnp.float32)
    kv_b = jnp.dot(n, w_ref[:, C_KB:C_KB + 2 * B_WIDTH], preferred_element_type=jnp.float32)
    c, s = _rope_coeffs(cb_ref, sb_ref, off_ref, first_half)
    ka = _rope_tile(kv_a[:, :LANES], c, s, first_half)
    va = kv_a[:, LANES:]
    for src, col in ((ka, KV_KA), (va, KV_VA)):
        swapped = pltpu.roll(src, HEAD_DIM, 1)
        kv_ref[0, :, col:col + LANES] = jnp.where(lo_half, src, swapped).astype(bf16)
        kv_ref[0, :, col + LANES:col + 2 * LANES] = jnp.where(lo_half, swapped, src).astype(bf16)
    kv_ref[0, :, KV_KB:] = kv_b.astype(bf16)


def _kv_call(x, gain, w, rope_base, rope_off, rows):
    bsz, seq, _ = x.shape
    const = lambda b, i: (0, 0)
    return pl.pallas_call(
        _kv_body,
        grid=(bsz, seq // rows),
        in_specs=[
            pl.BlockSpec((1, rows, D_MODEL), lambda b, i: (b, i, 0)),
            pl.BlockSpec((1, D_MODEL), const),
            pl.BlockSpec((D_MODEL, IN_COLS), const, pipeline_mode=pl.Buffered(1)),
            pl.BlockSpec((rows, LANES), const),
            pl.BlockSpec((rows, LANES), const),
            pl.BlockSpec((None, 2, LANES), lambda b, i: (i, 0, 0)),
        ],
        out_specs=pl.BlockSpec((1, rows, KV_COLS), lambda b, i: (b, i, 0)),
        out_shape=jax.ShapeDtypeStruct((bsz, seq, KV_COLS), jnp.bfloat16),
        compiler_params=pltpu.CompilerParams(
            dimension_semantics=("parallel", "parallel"), vmem_limit_bytes=VMEM_LIMIT_BYTES),
        name="kv_proj",
    )(x, gain, w, *rope_base, rope_off)


def _layer_body(sink_ref, x_ref, kvp_ref, kvc_ref, kvn_ref, kvm_ref, cb_ref, sb_ref, off_ref,
                gain_ref, fgain_ref, wr_ref, wpa_ref, wpb_ref, wout_ref, amask_ref, eye_ref, bpair_ref,
                brow_ref, bsel_ref,
                out_ref, n_ref, win_ref, qa_ref, qb_ref, oa_ref, ob_ref, proj_ref, *, n_chunks, n_rows):
    i = pl.program_id(1)
    f32, bf16 = jnp.float32, jnp.bfloat16
    scale = HEAD_DIM ** -0.5

    x = x_ref[0]
    n_ref[...] = _rmsnorm(x, gain_ref[...]).astype(bf16)

    win_ref[0:HALO] = kvp_ref[0]
    win_ref[HALO:HALO + CHUNK] = kvc_ref[0]
    win_ref[HALO + CHUNK:CHUNK + 2 * HALO] = kvn_ref[0]

    lo_half, first_half = _lane_patterns()
    c, s = _rope_coeffs(cb_ref, sb_ref, off_ref, first_half)
    qa = jnp.dot(n_ref[...], wr_ref[:, C_QA:C_QA + A_WIDTH], preferred_element_type=f32)
    for t in range(A_WIDTH // LANES):
        sl = slice(t * LANES, (t + 1) * LANES)
        qa_ref[:, sl] = (_rope_tile(qa[:, sl], c, s, first_half) * scale).astype(bf16)
    qb = jnp.dot(n_ref[...], wr_ref[:, C_QB:C_QB + B_WIDTH], preferred_element_type=f32)
    qb_ref[...] = (qb * scale).astype(bf16)

    zero_pad = jnp.zeros((META_PAD - N_META, LANES), bf16)

    def meta_tile(col):
        return jnp.concatenate([kvm_ref[0, :, col:col + LANES], zero_pad], axis=0)

    def split_heads(q):
        zeros = jnp.zeros_like(q)
        return jnp.concatenate([jnp.where(lo_half, q, zeros), jnp.where(lo_half, zeros, q)], axis=0)

    def with_ones(v):
        return jnp.concatenate([v, jnp.ones_like(v)], axis=1)

    def proj_piece(k):
        src = PROJ_SOURCES[k]
        proj_ref[:, k * PROJ_PIECE:(k + 1) * PROJ_PIECE] = jnp.dot(
            n_ref[...], wr_ref[:, src:src + PROJ_PIECE], preferred_element_type=f32)

    pieces = list(range(len(PROJ_SOURCES)))

    def pipelined(stages, fill_plan):
        pending = stages[0][0](stages[0][2])
        for t, (_, finish, item) in enumerate(stages):
            nxt = stages[t + 1][0](stages[t + 1][2]) if t + 1 < len(stages) else None
            for _ in range(fill_plan[t]):
                proj_piece(pieces.pop(0))
            finish(item, pending)
            pending = nxt

    a_tiles = A_GROUP // 2
    kmetas_a = [meta_tile(KV_KA + g * LANES) for g in range(A_KV_HEADS)]
    vmetas_a = [meta_tile(KV_VA + g * LANES) for g in range(A_KV_HEADS)]

    def a_window(col, tile, j):
        w0 = HALO + j * BLOCK - BLOCK
        return win_ref[w0:w0 + 3 * BLOCK, col + tile * LANES:col + (tile + 1) * LANES]

    def a_scores(item):
        g, j = item
        first = jnp.logical_and(i == 0, j == 0)
        last = jnp.logical_and(i == n_chunks - 1, j == BLOCKS_PER_CHUNK - 1)
        mask_t = amask_ref[jnp.where(first, 1, jnp.where(last, 2, 0))]
        kcat = jnp.concatenate([kmetas_a[g], a_window(KV_KA, g, j)], axis=0)
        k_aug = jnp.concatenate([kcat, mask_t], axis=1)
        q_tiles = [qa_ref[j * BLOCK:(j + 1) * BLOCK, (a_tiles * g + t) * LANES:(a_tiles * g + t + 1) * LANES]
                   for t in range(a_tiles)]
        zeros = jnp.zeros_like(q_tiles[0])
        qs = jnp.concatenate([jnp.where(lo_half, q, zeros) for q in q_tiles]
                             + [jnp.where(lo_half, zeros, q) for q in q_tiles], axis=0)
        q_aug = jnp.concatenate([qs, eye_ref[...]], axis=1)
        return lax.dot_general(q_aug, k_aug, _NT, preferred_element_type=f32)

    def a_finish(item, s):
        g, j = item
        ps, es = [], []
        for rb in range(A_GROUP):
            sink = sink_ref[A_GROUP * g + 2 * (rb % a_tiles) + rb // a_tiles]
            sh = s[rb * BLOCK:(rb + 1) * BLOCK]
            m = jnp.maximum(jnp.max(sh, axis=-1, keepdims=True), sink)
            ps.append(jnp.exp(sh - m).astype(bf16))
            es.append(jnp.exp(sink - m))
        vcat = jnp.concatenate([vmetas_a[g], a_window(KV_VA, g, j)], axis=0)
        o = jnp.dot(jnp.concatenate(ps, axis=0), with_ones(vcat), preferred_element_type=f32)
        for t in range(a_tiles):
            oe = o[t * BLOCK:(t + 1) * BLOCK]
            oo = o[(a_tiles + t) * BLOCK:(a_tiles + t + 1) * BLOCK]
            den = (jnp.where(lo_half, oe[:, LANES:], oo[:, LANES:])
                   + jnp.where(lo_half, es[t], es[a_tiles + t]))
            oa_ref[j * BLOCK:(j + 1) * BLOCK, (a_tiles * g + t) * LANES:(a_tiles * g + t + 1) * LANES] = (
                jnp.where(lo_half, oe[:, :LANES], oo[:, :LANES]) / den)

    a_stages = [(a_scores, a_finish, (g, j)) for g in range(A_KV_HEADS) for j in range(BLOCKS_PER_CHUNK)]

    kmetas = [meta_tile(KV_KB + p * LANES) for p in range(B_HEADS // 2)]
    vmetas = [meta_tile(KV_VB + p * LANES) for p in range(B_HEADS // 2)]

    def unit_geometry(step):
        r0 = i * ROWS_PER_CHUNK + 2 * step
        rs0 = jnp.clip(r0 - NA_KH // 2, 0, n_rows - NA_KH)
        rs1 = jnp.clip(r0 + 1 - NA_KH // 2, 0, n_rows - NA_KH)
        shifts = (r0 - rs0, r0 + 1 - rs0)
        variant = 1 - (rs1 - rs0)
        woff = pl.multiple_of((rs0 - i * ROWS_PER_CHUNK) * GRID_W + HALO, BLOCK)
        return shifts, variant, woff

    geometry = [unit_geometry(step) for step in range(ROWS_PER_CHUNK // 2)]

    def b_scores(item):
        step, pairs = item
        _, variant, woff = geometry[step]
        out = []
        for p in pairs:
            kcat = jnp.concatenate(
                [win_ref[pl.ds(woff, B_WIN_KEYS), KV_KB + p * LANES:KV_KB + (p + 1) * LANES],
                 kmetas[p]], axis=0)
            k_aug = jnp.concatenate([kcat, brow_ref[variant]], axis=1)
            qs = split_heads(qb_ref[step * BLOCK:(step + 1) * BLOCK, p * LANES:(p + 1) * LANES])
            q_aug = jnp.concatenate([qs, bsel_ref[...]], axis=1)
            out.append(lax.dot_general(q_aug, k_aug, _NT, preferred_element_type=f32))
        return out

    def b_finish(item, scores):
        step, pairs = item
        shifts, _, woff = geometry[step]
        probs = []
        for p, s in zip(pairs, scores):
            bias = jnp.concatenate(
                [jnp.concatenate([bpair_ref[p, 2 * jj + NA_KH - 1 - shifts[u], hh]
                                  for jj in range(B_WIN_ROWS // 2)], axis=1)
                 for hh in range(2) for u in range(2)], axis=0)
            s_loc = s[:, :B_WIN_KEYS] + bias
            s_met = s[:, B_WIN_KEYS:]
            m = jnp.maximum(jnp.max(s_loc, axis=-1, keepdims=True),
                            jnp.max(s_met, axis=-1, keepdims=True))
            probs.append(jnp.concatenate([jnp.exp(s_loc - m), jnp.exp(s_met - m)],
                                         axis=1).astype(bf16))
        for p, pr in zip(pairs, probs):
            vcat = jnp.concatenate(
                [win_ref[pl.ds(woff, B_WIN_KEYS), KV_VB + p * LANES:KV_VB + (p + 1) * LANES],
                 vmetas[p]], axis=0)
            o = jnp.dot(pr, with_ones(vcat), preferred_element_type=f32)
            o = o[:, :LANES] / o[:, LANES:]
            ob_ref[step * BLOCK:(step + 1) * BLOCK, p * LANES:(p + 1) * LANES] = jnp.where(
                lo_half, o[:BLOCK], o[BLOCK:])

    b_stages = [(b_scores, b_finish, (step, tuple(range(p0, p0 + B_PAIRS_PER_ITEM))))
                for step in range(ROWS_PER_CHUNK // 2)
                for p0 in range(0, B_HEADS // 2, B_PAIRS_PER_ITEM)]
    assert len(a_stages) + len(b_stages) == len(FILL_PLAN) and sum(FILL_PLAN) <= len(pieces)
    pipelined(a_stages + b_stages, FILL_PLAN)

    while pieces:
        proj_piece(pieces.pop(0))

    def gated_branch(o_ref, z_col, w_ref, g_col):
        z = proj_ref[:, z_col:z_col + A_WIDTH]
        a = (o_ref[...] * (z * jax.nn.sigmoid(z))).astype(bf16)
        y = jnp.dot(a, w_ref[...], preferred_element_type=f32)
        return jax.nn.sigmoid(proj_ref[:, g_col:g_col + D_MODEL]) * y

    merged = gated_branch(oa_ref, P_ZA, wpa_ref, P_GA) + gated_branch(ob_ref, P_ZB, wpb_ref, P_GB)
    h = x + jnp.dot(merged.astype(bf16), wout_ref[...], preferred_element_type=f32)
    out_ref[0] = _rmsnorm(h, fgain_ref[...])


def _layer_call(sink, x, kv, kv_meta, rope_base, rope_off, gain, fgain, w, w_pa, w_pb, w_out,
                amasks, btables):
    bsz, seq, _ = x.shape
    n_chunks = seq // CHUNK
    halo_per = CHUNK // HALO
    const2 = lambda b, i: (0, 0)
    const3 = lambda b, i: (0, 0, 0)
    resident = dict(pipeline_mode=pl.Buffered(1))
    body = functools.partial(_layer_body, n_chunks=n_chunks, n_rows=seq // GRID_W)
    return pl.pallas_call(
        body,
        grid=(bsz, n_chunks),
        in_specs=[
            pl.BlockSpec(memory_space=pltpu.SMEM),
            pl.BlockSpec((1, CHUNK, D_MODEL), lambda b, i: (b, i, 0)),
            pl.BlockSpec((1, HALO, KV_COLS), lambda b, i: (b, jnp.maximum(i * halo_per - 1, 0), 0)),
            pl.BlockSpec((1, CHUNK, KV_COLS), lambda b, i: (b, i, 0)),
            pl.BlockSpec((1, HALO, KV_COLS),
                         lambda b, i: (b, jnp.minimum((i + 1) * halo_per, n_chunks * halo_per - 1), 0)),
            pl.BlockSpec((1, N_META, KV_COLS), const3, **resident),
            pl.BlockSpec((CHUNK, LANES), const2, **resident),
            pl.BlockSpec((CHUNK, LANES), const2, **resident),
            pl.BlockSpec((None, 2, LANES), lambda b, i: (i, 0, 0)),
            pl.BlockSpec((1, D_MODEL), const2, **resident),
            pl.BlockSpec((1, D_MODEL), const2, **resident),
            pl.BlockSpec((D_MODEL, IN_COLS), const2, **resident),
            pl.BlockSpec((A_WIDTH, D_MODEL), const2, **resident),
            pl.BlockSpec((B_WIDTH, D_MODEL), const2, **resident),
            pl.BlockSpec((D_MODEL, D_MODEL), const2, **resident),
            pl.BlockSpec((3, META_PAD + 3 * BLOCK, BLOCK), const3, **resident),
            pl.BlockSpec((A_GROUP * BLOCK, BLOCK), const2, **resident),
            pl.BlockSpec((B_HEADS // 2, 2 * NA_KH, 2, GRID_W, LANES), lambda b, i: (0, 0, 0, 0, 0),
                         **resident),
            pl.BlockSpec((2, B_WIN_KEYS + META_PAD, LANES), const3, **resident),
            pl.BlockSpec((2 * BLOCK, LANES), const2, **resident),
        ],
        out_specs=pl.BlockSpec((1, CHUNK, D_MODEL), lambda b, i: (b, i, 0)),
        out_shape=jax.ShapeDtypeStruct((bsz, seq, D_MODEL), jnp.float32),
        scratch_shapes=[
            pltpu.VMEM((CHUNK, D_MODEL), jnp.bfloat16),
            pltpu.VMEM((CHUNK + 2 * HALO, KV_COLS), jnp.bfloat16),
            pltpu.VMEM((CHUNK, A_WIDTH), jnp.bfloat16),
            pltpu.VMEM((CHUNK, B_WIDTH), jnp.bfloat16),
            pltpu.VMEM((CHUNK, A_WIDTH), jnp.float32),
            pltpu.VMEM((CHUNK, B_WIDTH), jnp.float32),
            pltpu.VMEM((CHUNK, P_COLS), jnp.float32),
        ],
        compiler_params=pltpu.CompilerParams(
            dimension_semantics=("parallel", "arbitrary"), vmem_limit_bytes=VMEM_LIMIT_BYTES),
        name="hybrid_layer",
    )(sink, x, kv, kv, kv, kv_meta, *rope_base, rope_off, gain, fgain, w, w_pa, w_pb, w_out,
      *amasks, *btables)


def _rope_tables(n_chunks):
    d = np.arange(LANES) % HEAD_DIM
    inv_freq = ROPE_THETA ** (-jnp.asarray(d % ROT_HALF, jnp.float32) / ROT_HALF)
    inv_lane = jnp.where(d < ROT_DIM, inv_freq, 0.0)[None, :]
    base = jnp.arange(CHUNK, dtype=jnp.float32)[:, None] * inv_lane
    off = (N_META + CHUNK * jnp.arange(n_chunks, dtype=jnp.float32))[:, None] * inv_lane
    off_real = jnp.stack([jnp.cos(off), jnp.sin(off)], axis=1)
    off_meta = jnp.stack([jnp.ones((1, LANES), jnp.float32), jnp.zeros((1, LANES), jnp.float32)], axis=1)
    return (jnp.cos(base), jnp.sin(base)), off_meta, off_real


def _band_masks():
    qi = np.arange(BLOCK)[:, None]
    col = np.arange(META_PAD + 3 * BLOCK)[None, :]
    kj = col - META_PAD
    band = (col >= META_PAD) & (np.abs(kj - BLOCK - qi) <= WINDOW)
    meta = np.broadcast_to(col < N_META, band.shape)
    variants = [meta | band, meta | (band & (kj >= BLOCK)), meta | (band & (kj < 2 * BLOCK))]
    masks = np.where(np.stack(variants), 0.0, NEG_INF).astype(np.float32).transpose(0, 2, 1)
    eye = np.tile(np.eye(BLOCK, dtype=np.float32), (A_GROUP, 1))
    return jnp.asarray(masks, jnp.bfloat16), jnp.asarray(eye, jnp.bfloat16)


def _na_bias(rpb):
    cq = np.arange(GRID_W)[:, None]
    kc = np.arange(GRID_W)[None, :]
    start = np.clip(cq - NA_KW // 2, 0, GRID_W - NA_KW)
    valid = (kc >= start) & (kc < start + NA_KW)
    n_pairs, n_dr, n_dc = B_HEADS // 2, 2 * NA_KH - 1, 2 * NA_KW - 1
    onehot = ((kc - cq + NA_KW - 1)[..., None] == np.arange(n_dc)) & valid[..., None]
    select = np.zeros((GRID_W, 2, n_dc, 2, GRID_W), np.float32)
    for t in range(2):
        select[:, t, :, t, :] = onehot.transpose(0, 2, 1)
    rows = jnp.pad(rpb.reshape(n_pairs, 2, n_dr, n_dc), ((0, 0), (0, 0), (0, 2), (0, 0)))
    rows = jnp.stack([rows[:, :, :2 * NA_KH], rows[:, :, 1:]], axis=3)
    pair = jnp.einsum('phre,cen->prhcn', rows.reshape(n_pairs, 2, 2 * NA_KH, 2 * n_dc),
                      jnp.asarray(select.reshape(GRID_W, 2 * n_dc, LANES)),
                      precision=lax.Precision.HIGHEST)
    pair = jnp.where(np.tile(valid, (1, 2))[None, None, None], pair, NEG_INF)

    key_row = np.arange(B_WIN_KEYS + META_PAD) // GRID_W
    is_meta = (np.arange(B_WIN_KEYS + META_PAD) >= B_WIN_KEYS) & (
        np.arange(B_WIN_KEYS + META_PAD) < B_WIN_KEYS + N_META)
    rows = np.zeros((2, B_WIN_KEYS + META_PAD, LANES), np.float32)
    for variant, starts in enumerate(((0, 1), (0, 0))):
        for u, s0 in enumerate(starts):
            seen = ((key_row >= s0) & (key_row < s0 + NA_KH)) | is_meta
            rows[variant, :, u] = np.where(seen, 0.0, NEG_INF)
    sel = np.zeros((2, 2, GRID_W, LANES), np.float32)
    sel[:, 0, :, 0] = 1.0
    sel[:, 1, :, 1] = 1.0
    return (pair, jnp.asarray(rows, jnp.bfloat16),
            jnp.asarray(sel.reshape(2 * BLOCK, LANES), jnp.bfloat16))


def kernel(x, meta_tokens, norm_gain, w_in, sink_logits, rel_pos_bias, w_proj_a, w_proj_b, w_out,
           final_norm_gain):
    bsz, seq, _ = x.shape
    assert seq % CHUNK == 0 and seq // GRID_W >= NA_KH and norm_gain.shape[0] == 1
    bf16 = jnp.bfloat16
    w = w_in[0].astype(bf16)

    rope_base, off_meta, off_real = _rope_tables(seq // CHUNK)
    rope_base_meta = tuple(t[:N_META] for t in rope_base)

    gain = norm_gain[0][None]
    kv_meta = _kv_call(meta_tokens[None], gain, w, rope_base_meta, off_meta, N_META)
    kv = _kv_call(x, gain, w, rope_base, off_real, CHUNK)
    return _layer_call(sink_logits[0], x, kv, kv_meta, rope_base, off_real, gain,
                       final_norm_gain[None], w, w_proj_a[0].astype(bf16),
                       w_proj_b[0].astype(bf16), w_out[0].astype(bf16), _band_masks(),
                       _na_bias(rel_pos_bias[0]))
```

```python
import functools

import numpy as np
import jax
import jax.numpy as jnp
from jax import lax
from jax.experimental import pallas as pl
from jax.experimental.pallas import tpu as pltpu

D_MODEL = 1024
N_META = 16
HEAD_DIM = 64
RMS_EPS = 1e-6
NEG_INF = -1e30
A_HEADS = 8
A_KV_HEADS = 2
A_GROUP = A_HEADS // A_KV_HEADS
A_WIDTH = A_HEADS * HEAD_DIM
A_KV_WIDTH = A_KV_HEADS * HEAD_DIM
WINDOW = 128
BLOCK = 128
ROT_DIM = HEAD_DIM // 4
ROT_HALF = ROT_DIM // 2
ROPE_THETA = 500000.0
B_HEADS = 8
B_WIDTH = B_HEADS * HEAD_DIM
GRID_W = 64
NA_KH = 8
NA_KW = 16
SPLIT_SIZES = (A_WIDTH, A_KV_WIDTH, A_KV_WIDTH, A_WIDTH, B_WIDTH, B_WIDTH, B_WIDTH, B_WIDTH,
               D_MODEL, D_MODEL)

LANES = 128
CHUNK = 512
ROWS_PER_CHUNK = CHUNK // GRID_W
BLOCKS_PER_CHUNK = CHUNK // BLOCK
KV_CHUNKS_PER_STEP = 2
HALO = CHUNK // 2
assert HALO >= BLOCK and HALO >= (NA_KH // 2) * GRID_W
META_PAD = LANES
B_WIN_ROWS = NA_KH + 2
B_WIN_KEYS = B_WIN_ROWS * GRID_W
B_PAIRS_PER_ITEM = 2
PROJ_PIECE = 256
FILL_PLAN = (1, 1, 1, 1, 0, 0, 0, 0, 1, 1, 1, 1, 1, 1, 1, 1)

KV_KA = 0
KV_VA = KV_KA + A_KV_HEADS * LANES
KV_KB = KV_VA + A_KV_HEADS * LANES
KV_VB = KV_KB + B_WIDTH
KV_COLS = KV_VB + B_WIDTH

(C_QA, C_KA, C_VA, C_ZA, C_QB, C_KB, C_VB, C_ZB, C_GA, C_GB, IN_COLS) = (
    int(c) for c in np.cumsum((0,) + SPLIT_SIZES))
assert A_KV_WIDTH == LANES and C_VA == C_KA + A_KV_WIDTH and C_VB == C_KB + B_WIDTH

P_ZA = 0
P_ZB = P_ZA + A_WIDTH
P_GA = P_ZB + B_WIDTH
P_GB = P_GA + D_MODEL
P_COLS = P_GB + D_MODEL
PROJ_SOURCES = tuple(
    src + off for src, width in ((C_ZA, A_WIDTH), (C_ZB, B_WIDTH), (C_GA, D_MODEL), (C_GB, D_MODEL))
    for off in range(0, width, PROJ_PIECE))

VMEM_LIMIT_BYTES = 58 * 1024 * 1024

_NT = (((1,), (1,)), ((), ()))


def _rmsnorm(x, gain):
    return x * lax.rsqrt(jnp.mean(x * x, axis=-1, keepdims=True) + RMS_EPS) * gain


def _lane_patterns():
    lane = lax.broadcasted_iota(jnp.int32, (1, LANES), 1)
    return lane < HEAD_DIM, (lane & (HEAD_DIM - 1)) < ROT_HALF


def _rope_coeffs(cb_ref, sb_ref, off_ref, first_half):
    cb, sb = cb_ref[...], sb_ref[...]
    ca, sa = off_ref[0:1, :], off_ref[1:2, :]
    c = cb * ca - sb * sa
    s = (sb * ca + cb * sa) * jnp.where(first_half, -1.0, 1.0)
    return c, s


def _rope_tile(t, c, s, first_half):
    partner = jnp.where(first_half, pltpu.roll(t, LANES - ROT_HALF, 1), pltpu.roll(t, ROT_HALF, 1))
    return t * c + partner * s


def _kv_body(x_ref, meta_ref, gain_ref, w_ref, cb_ref, sb_ref, off_ref, kv_ref, kvm_ref):
    f32, bf16 = jnp.float32, jnp.bfloat16
    lo_half, first_half = _lane_patterns()
    sign = jnp.where(first_half, -1.0, 1.0)

    def project(n):
        return (jnp.dot(n, w_ref[:, C_KA:C_KA + 2 * A_KV_WIDTH], preferred_element_type=f32),
                jnp.dot(n, w_ref[:, C_KB:C_KB + 2 * B_WIDTH], preferred_element_type=f32))

    def store(dst_ref, rows, kv_a, kv_b, c, s):
        ka = _rope_tile(kv_a[:, :LANES], c, s, first_half)
        va = kv_a[:, LANES:]
        for src, col in ((ka, KV_KA), (va, KV_VA)):
            swapped = pltpu.roll(src, HEAD_DIM, 1)
            dst_ref[0, rows, col:col + LANES] = jnp.where(lo_half, src, swapped).astype(bf16)
            dst_ref[0, rows, col + LANES:col + 2 * LANES] = jnp.where(lo_half, swapped, src).astype(bf16)
        dst_ref[0, rows, KV_KB:] = kv_b.astype(bf16)

    @pl.when(jnp.logical_and(pl.program_id(0) == 0, pl.program_id(1) == 0))
    def _():
        n = _rmsnorm(meta_ref[...], gain_ref[...]).astype(bf16)
        store(kvm_ref, slice(None), *project(n), cb_ref[0:N_META], sb_ref[0:N_META] * sign)

    chunks = [slice(h * CHUNK, (h + 1) * CHUNK) for h in range(KV_CHUNKS_PER_STEP)]
    ns = [_rmsnorm(x_ref[0, rows], gain_ref[...]).astype(bf16) for rows in chunks]
    kvs = [project(n) for n in ns]
    for h, rows in enumerate(chunks):
        c, s = _rope_coeffs(cb_ref, sb_ref, off_ref.at[h], first_half)
        store(kv_ref, rows, *kvs[h], c, s)


def _kv_call(x, meta, gain, w, rope_base, rope_off):
    bsz, seq, _ = x.shape
    rows = KV_CHUNKS_PER_STEP * CHUNK
    const = lambda b, i: (0, 0)
    return pl.pallas_call(
        _kv_body,
        grid=(bsz, seq // rows),
        in_specs=[
            pl.BlockSpec((1, rows, D_MODEL), lambda b, i: (b, i, 0)),
            pl.BlockSpec((N_META, D_MODEL), const, pipeline_mode=pl.Buffered(1)),
            pl.BlockSpec((1, D_MODEL), const),
            pl.BlockSpec((D_MODEL, IN_COLS), const, pipeline_mode=pl.Buffered(1)),
            pl.BlockSpec((CHUNK, LANES), const),
            pl.BlockSpec((CHUNK, LANES), const),
            pl.BlockSpec((KV_CHUNKS_PER_STEP, 2, LANES), lambda b, i: (i, 0, 0)),
        ],
        out_specs=[pl.BlockSpec((1, rows, KV_COLS), lambda b, i: (b, i, 0)),
                   pl.BlockSpec((1, N_META, KV_COLS), lambda b, i: (0, 0, 0))],
        out_shape=[jax.ShapeDtypeStruct((bsz, seq, KV_COLS), jnp.bfloat16),
                   jax.ShapeDtypeStruct((1, N_META, KV_COLS), jnp.bfloat16)],
        compiler_params=pltpu.CompilerParams(
            dimension_semantics=("arbitrary", "arbitrary"), vmem_limit_bytes=VMEM_LIMIT_BYTES),
        name="kv_proj",
    )(x, meta, gain, w, *rope_base, rope_off)


def _layer_body(sink_ref, x_ref, kvp_ref, kvc_ref, kvn_ref, kvm_ref, cb_ref, sb_ref, off_ref,
                gain_ref, fgain_ref, wr_ref, wpa_ref, wpb_ref, wout_ref, amask_ref, eye_ref, bpair_ref,
                brow_ref, bsel_ref,
                out_ref, n_ref, win_ref, qa_ref, qb_ref, oa_ref, ob_ref, proj_ref, *, n_chunks, n_rows):
    i = pl.program_id(1)
    f32, bf16 = jnp.float32, jnp.bfloat16
    scale = HEAD_DIM ** -0.5

    x = x_ref[0]
    n_ref[...] = _rmsnorm(x, gain_ref[...]).astype(bf16)

    win_ref[0:HALO] = kvp_ref[0]
    win_ref[HALO:HALO + CHUNK] = kvc_ref[0]
    win_ref[HALO + CHUNK:CHUNK + 2 * HALO] = kvn_ref[0]

    lo_half, first_half = _lane_patterns()
    c, s = _rope_coeffs(cb_ref, sb_ref, off_ref, first_half)
    qa = jnp.dot(n_ref[...], wr_ref[:, C_QA:C_QA + A_WIDTH], preferred_element_type=f32)
    for t in range(A_WIDTH // LANES):
        sl = slice(t * LANES, (t + 1) * LANES)
        qa_ref[:, sl] = (_rope_tile(qa[:, sl], c, s, first_half) * scale).astype(bf16)
    qb = jnp.dot(n_ref[...], wr_ref[:, C_QB:C_QB + B_WIDTH], preferred_element_type=f32)
    qb_ref[...] = (qb * scale).astype(bf16)

    zero_pad = jnp.zeros((META_PAD - N_META, LANES), bf16)

    def meta_tile(col):
        return jnp.concatenate([kvm_ref[0, :, col:col + LANES], zero_pad], axis=0)

    def split_heads(q):
        zeros = jnp.zeros_like(q)
        return jnp.concatenate([jnp.where(lo_half, q, zeros), jnp.where(lo_half, zeros, q)], axis=0)

    def with_ones(v):
        return jnp.concatenate([v, jnp.ones_like(v)], axis=1)

    def proj_piece(k):
        src = PROJ_SOURCES[k]
        proj_ref[:, k * PROJ_PIECE:(k + 1) * PROJ_PIECE] = jnp.dot(
            n_ref[...], wr_ref[:, src:src + PROJ_PIECE], preferred_element_type=f32)

    pieces = list(range(len(PROJ_SOURCES)))

    def pipelined(stages, fill_plan):
        pending = stages[0][0](stages[0][2])
        for t, (_, finish, item) in enumerate(stages):
            nxt = stages[t + 1][0](stages[t + 1][2]) if t + 1 < len(stages) else None
            for _ in range(fill_plan[t]):
                proj_piece(pieces.pop(0))
            finish(item, pending)
            pending = nxt

    a_tiles = A_GROUP // 2
    kmetas_a = [meta_tile(KV_KA + g * LANES) for g in range(A_KV_HEADS)]
    vmetas_a = [meta_tile(KV_VA + g * LANES) for g in range(A_KV_HEADS)]

    def a_window(col, tile, j):
        w0 = HALO + j * BLOCK - BLOCK
        return win_ref[w0:w0 + 3 * BLOCK, col + tile * LANES:col + (tile + 1) * LANES]

    def a_scores(item):
        g, j = item
        first = jnp.logical_and(i == 0, j == 0)
        last = jnp.logical_and(i == n_chunks - 1, j == BLOCKS_PER_CHUNK - 1)
        mask_t = amask_ref[jnp.where(first, 1, jnp.where(last, 2, 0))]
        kcat = jnp.concatenate([kmetas_a[g], a_window(KV_KA, g, j)], axis=0)
        k_aug = jnp.concatenate([kcat, mask_t], axis=1)
        q_tiles = [qa_ref[j * BLOCK:(j + 1) * BLOCK, (a_tiles * g + t) * LANES:(a_tiles * g + t + 1) * LANES]
                   for t in range(a_tiles)]
        zeros = jnp.zeros_like(q_tiles[0])
        qs = jnp.concatenate([jnp.where(lo_half, q, zeros) for q in q_tiles]
                             + [jnp.where(lo_half, zeros, q) for q in q_tiles], axis=0)
        q_aug = jnp.concatenate([qs, eye_ref[...]], axis=1)
        return lax.dot_general(q_aug, k_aug, _NT, preferred_element_type=f32)

    def a_finish(item, s):
        g, j = item
        ps, es = [], []
        for rb in range(A_GROUP):
            sink = sink_ref[A_GROUP * g + 2 * (rb % a_tiles) + rb // a_tiles]
            sh = s[rb * BLOCK:(rb + 1) * BLOCK]
            m = jnp.maximum(jnp.max(sh, axis=-1, keepdims=True), sink)
            ps.append(jnp.exp(sh - m).astype(bf16))
            es.append(jnp.exp(sink - m))
        vcat = jnp.concatenate([vmetas_a[g], a_window(KV_VA, g, j)], axis=0)
        o = jnp.dot(jnp.concatenate(ps, axis=0), with_ones(vcat), preferred_element_type=f32)
        for t in range(a_tiles):
            oe = o[t * BLOCK:(t + 1) * BLOCK]
            oo = o[(a_tiles + t) * BLOCK:(a_tiles + t + 1) * BLOCK]
            den = (jnp.where(lo_half, oe[:, LANES:], oo[:, LANES:])
                   + jnp.where(lo_half, es[t], es[a_tiles + t]))
            oa_ref[j * BLOCK:(j + 1) * BLOCK, (a_tiles * g + t) * LANES:(a_tiles * g + t + 1) * LANES] = (
                jnp.where(lo_half, oe[:, :LANES], oo[:, :LANES]) / den)

    a_stages = [(a_scores, a_finish, (g, j)) for g in range(A_KV_HEADS) for j in range(BLOCKS_PER_CHUNK)]

    kmetas = [meta_tile(KV_KB + p * LANES) for p in range(B_HEADS // 2)]
    vmetas = [meta_tile(KV_VB + p * LANES) for p in range(B_HEADS // 2)]

    def unit_geometry(step):
        r0 = i * ROWS_PER_CHUNK + 2 * step
        rs0 = jnp.clip(r0 - NA_KH // 2, 0, n_rows - NA_KH)
        rs1 = jnp.clip(r0 + 1 - NA_KH // 2, 0, n_rows - NA_KH)
        shifts = (r0 - rs0, r0 + 1 - rs0)
        variant = 1 - (rs1 - rs0)
        woff = pl.multiple_of((rs0 - i * ROWS_PER_CHUNK) * GRID_W + HALO, BLOCK)
        return shifts, variant, woff

    geometry = [unit_geometry(step) for step in range(ROWS_PER_CHUNK // 2)]

    def b_scores(item):
        step, pairs = item
        _, variant, woff = geometry[step]
        out = []
        for p in pairs:
            kcat = jnp.concatenate(
                [win_ref[pl.ds(woff, B_WIN_KEYS), KV_KB + p * LANES:KV_KB + (p + 1) * LANES],
                 kmetas[p]], axis=0)
            k_aug = jnp.concatenate([kcat, brow_ref[variant]], axis=1)
            qs = split_heads(qb_ref[step * BLOCK:(step + 1) * BLOCK, p * LANES:(p + 1) * LANES])
            q_aug = jnp.concatenate([qs, bsel_ref[...]], axis=1)
            out.append(lax.dot_general(q_aug, k_aug, _NT, preferred_element_type=f32))
        return out

    def b_finish(item, scores):
        step, pairs = item
        shifts, _, woff = geometry[step]
        probs = []
        for p, s in zip(pairs, scores):
            bias = jnp.concatenate(
                [jnp.concatenate([bpair_ref[p, 2 * jj + NA_KH - 1 - shifts[u], hh]
                                  for jj in range(B_WIN_ROWS // 2)], axis=1)
                 for hh in range(2) for u in range(2)], axis=0)
            s_loc = s[:, :B_WIN_KEYS] + bias
            s_met = s[:, B_WIN_KEYS:]
            m = jnp.maximum(jnp.max(s_loc, axis=-1, keepdims=True),
                            jnp.max(s_met, axis=-1, keepdims=True))
            probs.append(jnp.concatenate([jnp.exp(s_loc - m), jnp.exp(s_met - m)],
                                         axis=1).astype(bf16))
        for p, pr in zip(pairs, probs):
            vcat = jnp.concatenate(
                [win_ref[pl.ds(woff, B_WIN_KEYS), KV_VB + p * LANES:KV_VB + (p + 1) * LANES],
                 vmetas[p]], axis=0)
            o = jnp.dot(pr, with_ones(vcat), preferred_element_type=f32)
            o = o[:, :LANES] / o[:, LANES:]
            ob_ref[step * BLOCK:(step + 1) * BLOCK, p * LANES:(p + 1) * LANES] = jnp.where(
                lo_half, o[:BLOCK], o[BLOCK:])

    b_stages = [(b_scores, b_finish, (step, tuple(range(p0, p0 + B_PAIRS_PER_ITEM))))
                for step in range(ROWS_PER_CHUNK // 2)
                for p0 in range(0, B_HEADS // 2, B_PAIRS_PER_ITEM)]
    assert len(a_stages) + len(b_stages) == len(FILL_PLAN) and sum(FILL_PLAN) <= len(pieces)
    pipelined(a_stages + b_stages, FILL_PLAN)

    while pieces:
        proj_piece(pieces.pop(0))

    def gated_branch(o_ref, z_col, w_ref, g_col):
        z = proj_ref[:, z_col:z_col + A_WIDTH]
        a = (o_ref[...] * (z * jax.nn.sigmoid(z))).astype(bf16)
        y = jnp.dot(a, w_ref[...], preferred_element_type=f32)
        return jax.nn.sigmoid(proj_ref[:, g_col:g_col + D_MODEL]) * y

    merged = gated_branch(oa_ref, P_ZA, wpa_ref, P_GA) + gated_branch(ob_ref, P_ZB, wpb_ref, P_GB)
    h = x + jnp.dot(merged.astype(bf16), wout_ref[...], preferred_element_type=f32)
    out_ref[0] = _rmsnorm(h, fgain_ref[...])


def _layer_call(sink, x, kv, kv_meta, rope_base, rope_off, gain, fgain, w, w_pa, w_pb, w_out,
                amasks, btables):
    bsz, seq, _ = x.shape
    n_chunks = seq // CHUNK
    halo_per = CHUNK // HALO
    const2 = lambda b, i: (0, 0)
    const3 = lambda b, i: (0, 0, 0)
    resident = dict(pipeline_mode=pl.Buffered(1))
    body = functools.partial(_layer_body, n_chunks=n_chunks, n_rows=seq // GRID_W)
    return pl.pallas_call(
        body,
        grid=(bsz, n_chunks),
        in_specs=[
            pl.BlockSpec(memory_space=pltpu.SMEM),
            pl.BlockSpec((1, CHUNK, D_MODEL), lambda b, i: (b, i, 0)),
            pl.BlockSpec((1, HALO, KV_COLS), lambda b, i: (b, jnp.maximum(i * halo_per - 1, 0), 0)),
            pl.BlockSpec((1, CHUNK, KV_COLS), lambda b, i: (b, i, 0)),
            pl.BlockSpec((1, HALO, KV_COLS),
                         lambda b, i: (b, jnp.minimum((i + 1) * halo_per, n_chunks * halo_per - 1), 0)),
            pl.BlockSpec((1, N_META, KV_COLS), const3, **resident),
            pl.BlockSpec((CHUNK, LANES), const2, **resident),
            pl.BlockSpec((CHUNK, LANES), const2, **resident),
            pl.BlockSpec((None, 2, LANES), lambda b, i: (i, 0, 0)),
            pl.BlockSpec((1, D_MODEL), const2, **resident),
            pl.BlockSpec((1, D_MODEL), const2, **resident),
            pl.BlockSpec((D_MODEL, IN_COLS), const2, **resident),
            pl.BlockSpec((A_WIDTH, D_MODEL), const2, **resident),
            pl.BlockSpec((B_WIDTH, D_MODEL), const2, **resident),
            pl.BlockSpec((D_MODEL, D_MODEL), const2, **resident),
            pl.BlockSpec((3, META_PAD + 3 * BLOCK, BLOCK), const3, **resident),
            pl.BlockSpec((A_GROUP * BLOCK, BLOCK), const2, **resident),
            pl.BlockSpec((B_HEADS // 2, 2 * NA_KH, 2, GRID_W, LANES), lambda b, i: (0, 0, 0, 0, 0),
                         **resident),
            pl.BlockSpec((2, B_WIN_KEYS + META_PAD, LANES), const3, **resident),
            pl.BlockSpec((2 * BLOCK, LANES), const2, **resident),
        ],
        out_specs=pl.BlockSpec((1, CHUNK, D_MODEL), lambda b, i: (b, i, 0)),
        out_shape=jax.ShapeDtypeStruct((bsz, seq, D_MODEL), jnp.float32),
        scratch_shapes=[
            pltpu.VMEM((CHUNK, D_MODEL), jnp.bfloat16),
            pltpu.VMEM((CHUNK + 2 * HALO, KV_COLS), jnp.bfloat16),
            pltpu.VMEM((CHUNK, A_WIDTH), jnp.bfloat16),
            pltpu.VMEM((CHUNK, B_WIDTH), jnp.bfloat16),
            pltpu.VMEM((CHUNK, A_WIDTH), jnp.float32),
            pltpu.VMEM((CHUNK, B_WIDTH), jnp.float32),
            pltpu.VMEM((CHUNK, P_COLS), jnp.float32),
        ],
        compiler_params=pltpu.CompilerParams(
            dimension_semantics=("parallel", "arbitrary"), vmem_limit_bytes=VMEM_LIMIT_BYTES),
        name="hybrid_layer",
    )(sink, x, kv, kv, kv, kv_meta, *rope_base, rope_off, gain, fgain, w, w_pa, w_pb, w_out,
      *amasks, *btables)


def _rope_tables(n_chunks):
    d = np.arange(LANES) % HEAD_DIM
    inv_freq = ROPE_THETA ** (-jnp.asarray(d % ROT_HALF, jnp.float32) / ROT_HALF)
    inv_lane = jnp.where(d < ROT_DIM, inv_freq, 0.0)[None, :]
    base = jnp.arange(CHUNK, dtype=jnp.float32)[:, None] * inv_lane
    off = (N_META + CHUNK * jnp.arange(n_chunks, dtype=jnp.float32))[:, None] * inv_lane
    return (jnp.cos(base), jnp.sin(base)), jnp.stack([jnp.cos(off), jnp.sin(off)], axis=1)


def _band_masks():
    qi = np.arange(BLOCK)[:, None]
    col = np.arange(META_PAD + 3 * BLOCK)[None, :]
    kj = col - META_PAD
    band = (col >= META_PAD) & (np.abs(kj - BLOCK - qi) <= WINDOW)
    meta = np.broadcast_to(col < N_META, band.shape)
    variants = [meta | band, meta | (band & (kj >= BLOCK)), meta | (band & (kj < 2 * BLOCK))]
    masks = np.where(np.stack(variants), 0.0, NEG_INF).astype(np.float32).transpose(0, 2, 1)
    eye = np.tile(np.eye(BLOCK, dtype=np.float32), (A_GROUP, 1))
    return jnp.asarray(masks, jnp.bfloat16), jnp.asarray(eye, jnp.bfloat16)


def _na_bias(rpb):
    cq = np.arange(GRID_W)[:, None]
    kc = np.arange(GRID_W)[None, :]
    start = np.clip(cq - NA_KW // 2, 0, GRID_W - NA_KW)
    valid = (kc >= start) & (kc < start + NA_KW)
    n_pairs, n_dr, n_dc = B_HEADS // 2, 2 * NA_KH - 1, 2 * NA_KW - 1
    onehot = ((kc - cq + NA_KW - 1)[..., None] == np.arange(n_dc)) & valid[..., None]
    select = np.zeros((GRID_W, 2, n_dc, 2, GRID_W), np.float32)
    for t in range(2):
        select[:, t, :, t, :] = onehot.transpose(0, 2, 1)
    rows = jnp.pad(rpb.reshape(n_pairs, 2, n_dr, n_dc), ((0, 0), (0, 0), (0, 2), (0, 0)))
    rows = jnp.stack([rows[:, :, :2 * NA_KH], rows[:, :, 1:]], axis=3)
    pair = jnp.einsum('phre,cen->prhcn', rows.reshape(n_pairs, 2, 2 * NA_KH, 2 * n_dc),
                      jnp.asarray(select.reshape(GRID_W, 2 * n_dc, LANES)),
                      precision=lax.Precision.HIGHEST)
    pair = jnp.where(np.tile(valid, (1, 2))[None, None, None], pair, NEG_INF)

    key_row = np.arange(B_WIN_KEYS + META_PAD) // GRID_W
    is_meta = (np.arange(B_WIN_KEYS + META_PAD) >= B_WIN_KEYS) & (
        np.arange(B_WIN_KEYS + META_PAD) < B_WIN_KEYS + N_META)
    rows = np.zeros((2, B_WIN_KEYS + META_PAD, LANES), np.float32)
    for variant, starts in enumerate(((0, 1), (0, 0))):
        for u, s0 in enumerate(starts):
            seen = ((key_row >= s0) & (key_row < s0 + NA_KH)) | is_meta
            rows[variant, :, u] = np.where(seen, 0.0, NEG_INF)
    sel = np.zeros((2, 2, GRID_W, LANES), np.float32)
    sel[:, 0, :, 0] = 1.0
    sel[:, 1, :, 1] = 1.0
    return (pair, jnp.asarray(rows, jnp.bfloat16),
            jnp.asarray(sel.reshape(2 * BLOCK, LANES), jnp.bfloat16))


def kernel(x, meta_tokens, norm_gain, w_in, sink_logits, rel_pos_bias, w_proj_a, w_proj_b, w_out,
           final_norm_gain):
    bsz, seq, _ = x.shape
    assert seq % CHUNK == 0 and seq // GRID_W >= NA_KH and norm_gain.shape[0] == 1
    bf16 = jnp.bfloat16
    w = w_in[0].astype(bf16)

    rope_base, rope_off = _rope_tables(seq // CHUNK)
    gain = norm_gain[0][None]
    kv, kv_meta = _kv_call(x, meta_tokens, gain, w, rope_base, rope_off)
    return _layer_call(sink_logits[0], x, kv, kv_meta, rope_base, rope_off, gain,
                       final_norm_gain[None], w, w_proj_a[0].astype(bf16),
                       w_proj_b[0].astype(bf16), w_out[0].astype(bf16), _band_masks(),
                       _na_bias(rel_pos_bias[0]))
```

```python
import functools

import numpy as np
import jax
import jax.numpy as jnp
from jax import lax
from jax.experimental import pallas as pl
from jax.experimental.pallas import tpu as pltpu

D_MODEL = 1024
N_META = 16
HEAD_DIM = 64
RMS_EPS = 1e-6
NEG_INF = -1e30
A_HEADS = 8
A_KV_HEADS = 2
A_GROUP = A_HEADS // A_KV_HEADS
A_WIDTH = A_HEADS * HEAD_DIM
A_KV_WIDTH = A_KV_HEADS * HEAD_DIM
WINDOW = 128
BLOCK = 128
ROT_DIM = HEAD_DIM // 4
ROT_HALF = ROT_DIM // 2
ROPE_THETA = 500000.0
B_HEADS = 8
B_WIDTH = B_HEADS * HEAD_DIM
GRID_W = 64
NA_KH = 8
NA_KW = 16
SPLIT_SIZES = (A_WIDTH, A_KV_WIDTH, A_KV_WIDTH, A_WIDTH, B_WIDTH, B_WIDTH, B_WIDTH, B_WIDTH,
               D_MODEL, D_MODEL)

LANES = 128
CHUNK = 512
ROWS_PER_CHUNK = CHUNK // GRID_W
BLOCKS_PER_CHUNK = CHUNK // BLOCK
KV_CHUNKS_PER_STEP = 2
HALO = CHUNK // 2
assert HALO >= BLOCK and HALO >= (NA_KH // 2) * GRID_W
META_PAD = LANES
B_WIN_ROWS = NA_KH + 2
B_WIN_KEYS = B_WIN_ROWS * GRID_W
B_PAIRS_PER_ITEM = 2
PIPELINE_DEPTH = 1
PROJ_PIECE = 256
FILL_PLAN = (1, 1, 1, 1, 1, 1, 1, 1, 1, 0, 1, 0, 1, 0, 1, 0)

KV_KA = 0
KV_VA = KV_KA + A_KV_HEADS * LANES
KV_KB = KV_VA + A_KV_HEADS * LANES
KV_VB = KV_KB + B_WIDTH
KV_COLS = KV_VB + B_WIDTH

(C_QA, C_KA, C_VA, C_ZA, C_QB, C_KB, C_VB, C_ZB, C_GA, C_GB, IN_COLS) = (
    int(c) for c in np.cumsum((0,) + SPLIT_SIZES))
assert A_KV_WIDTH == LANES and C_VA == C_KA + A_KV_WIDTH and C_VB == C_KB + B_WIDTH

P_ZA = 0
P_ZB = P_ZA + A_WIDTH
P_GA = P_ZB + B_WIDTH
P_GB = P_GA + D_MODEL
P_COLS = P_GB + D_MODEL
PROJ_SOURCES = tuple(
    src + off for src, width in ((C_ZA, A_WIDTH), (C_ZB, B_WIDTH), (C_GA, D_MODEL), (C_GB, D_MODEL))
    for off in range(0, width, PROJ_PIECE))

VMEM_LIMIT_BYTES = 58 * 1024 * 1024

_NT = (((1,), (1,)), ((), ()))


def _rmsnorm(x, gain):
    return x * lax.rsqrt(jnp.mean(x * x, axis=-1, keepdims=True) + RMS_EPS) * gain


def _lane_patterns():
    lane = lax.broadcasted_iota(jnp.int32, (1, LANES), 1)
    return lane < HEAD_DIM, (lane & (HEAD_DIM - 1)) < ROT_HALF


def _rope_coeffs(cb_ref, sb_ref, off_ref, first_half):
    cb, sb = cb_ref[...], sb_ref[...]
    ca, sa = off_ref[0:1, :], off_ref[1:2, :]
    c = cb * ca - sb * sa
    s = (sb * ca + cb * sa) * jnp.where(first_half, -1.0, 1.0)
    return c, s


def _rope_tile(t, c, s, first_half):
    partner = jnp.where(first_half, pltpu.roll(t, LANES - ROT_HALF, 1), pltpu.roll(t, ROT_HALF, 1))
    return t * c + partner * s


def _kv_body(x_ref, meta_ref, gain_ref, w_ref, cb_ref, sb_ref, off_ref, kv_ref, kvm_ref):
    f32, bf16 = jnp.float32, jnp.bfloat16
    lo_half, first_half = _lane_patterns()
    sign = jnp.where(first_half, -1.0, 1.0)

    def project(n):
        return (jnp.dot(n, w_ref[:, C_KA:C_KA + 2 * A_KV_WIDTH], preferred_element_type=f32),
                jnp.dot(n, w_ref[:, C_KB:C_KB + 2 * B_WIDTH], preferred_element_type=f32))

    def store(dst_ref, rows, kv_a, kv_b, c, s):
        ka = _rope_tile(kv_a[:, :LANES], c, s, first_half)
        va = kv_a[:, LANES:]
        for src, col in ((ka, KV_KA), (va, KV_VA)):
            swapped = pltpu.roll(src, HEAD_DIM, 1)
            dst_ref[0, rows, col:col + LANES] = jnp.where(lo_half, src, swapped).astype(bf16)
            dst_ref[0, rows, col + LANES:col + 2 * LANES] = jnp.where(lo_half, swapped, src).astype(bf16)
        dst_ref[0, rows, KV_KB:] = kv_b.astype(bf16)

    @pl.when(jnp.logical_and(pl.program_id(0) == 0, pl.program_id(1) == 0))
    def _():
        n = _rmsnorm(meta_ref[...], gain_ref[...]).astype(bf16)
        store(kvm_ref, slice(None), *project(n), cb_ref[0:N_META], sb_ref[0:N_META] * sign)

    chunks = [slice(h * CHUNK, (h + 1) * CHUNK) for h in range(KV_CHUNKS_PER_STEP)]
    ns = [_rmsnorm(x_ref[0, rows], gain_ref[...]).astype(bf16) for rows in chunks]
    kvs = [project(n) for n in ns]
    for h, rows in enumerate(chunks):
        c, s = _rope_coeffs(cb_ref, sb_ref, off_ref.at[h], first_half)
        store(kv_ref, rows, *kvs[h], c, s)


def _kv_call(x, meta, gain, w, rope_base, rope_off):
    bsz, seq, _ = x.shape
    rows = KV_CHUNKS_PER_STEP * CHUNK
    const = lambda b, i: (0, 0)
    return pl.pallas_call(
        _kv_body,
        grid=(bsz, seq // rows),
        in_specs=[
            pl.BlockSpec((1, rows, D_MODEL), lambda b, i: (b, i, 0)),
            pl.BlockSpec((N_META, D_MODEL), const, pipeline_mode=pl.Buffered(1)),
            pl.BlockSpec((1, D_MODEL), const),
            pl.BlockSpec((D_MODEL, IN_COLS), const, pipeline_mode=pl.Buffered(1)),
            pl.BlockSpec((CHUNK, LANES), const),
            pl.BlockSpec((CHUNK, LANES), const),
            pl.BlockSpec((KV_CHUNKS_PER_STEP, 2, LANES), lambda b, i: (i, 0, 0)),
        ],
        out_specs=[pl.BlockSpec((1, rows, KV_COLS), lambda b, i: (b, i, 0)),
                   pl.BlockSpec((1, N_META, KV_COLS), lambda b, i: (0, 0, 0))],
        out_shape=[jax.ShapeDtypeStruct((bsz, seq, KV_COLS), jnp.bfloat16),
                   jax.ShapeDtypeStruct((1, N_META, KV_COLS), jnp.bfloat16)],
        compiler_params=pltpu.CompilerParams(
            dimension_semantics=("arbitrary", "arbitrary"), vmem_limit_bytes=VMEM_LIMIT_BYTES),
        name="kv_proj",
    )(x, meta, gain, w, *rope_base, rope_off)


def _layer_body(sink_ref, x_ref, kvp_ref, kvc_ref, kvn_ref, kvm_ref, cb_ref, sb_ref, off_ref,
                gain_ref, fgain_ref, wr_ref, wpa_ref, wpb_ref, wout_ref, amask_ref, eye_ref, bpair_ref,
                brow_ref, bsel_ref,
                out_ref, n_ref, win_ref, qa_ref, qb_ref, oa_ref, ob_ref, proj_ref, *, n_chunks, n_rows):
    i = pl.program_id(1)
    f32, bf16 = jnp.float32, jnp.bfloat16
    scale = HEAD_DIM ** -0.5

    x = x_ref[0]
    n_ref[...] = _rmsnorm(x, gain_ref[...]).astype(bf16)

    win_ref[0:HALO] = kvp_ref[0]
    win_ref[HALO:HALO + CHUNK] = kvc_ref[0]
    win_ref[HALO + CHUNK:CHUNK + 2 * HALO] = kvn_ref[0]

    lo_half, first_half = _lane_patterns()
    c, s = _rope_coeffs(cb_ref, sb_ref, off_ref, first_half)
    qa = jnp.dot(n_ref[...], wr_ref[:, C_QA:C_QA + A_WIDTH], preferred_element_type=f32)
    for t in range(A_WIDTH // LANES):
        sl = slice(t * LANES, (t + 1) * LANES)
        qa_ref[:, sl] = (_rope_tile(qa[:, sl], c, s, first_half) * scale).astype(bf16)
    qb = jnp.dot(n_ref[...], wr_ref[:, C_QB:C_QB + B_WIDTH], preferred_element_type=f32)
    qb_ref[...] = (qb * scale).astype(bf16)

    zero_pad = jnp.zeros((META_PAD - N_META, LANES), bf16)

    def meta_tile(col):
        return jnp.concatenate([kvm_ref[0, :, col:col + LANES], zero_pad], axis=0)

    def split_heads(q):
        zeros = jnp.zeros_like(q)
        return jnp.concatenate([jnp.where(lo_half, q, zeros), jnp.where(lo_half, zeros, q)], axis=0)

    def with_ones(v):
        return jnp.concatenate([v, jnp.ones_like(v)], axis=1)

    def proj_piece(k):
        src = PROJ_SOURCES[k]
        proj_ref[:, k * PROJ_PIECE:(k + 1) * PROJ_PIECE] = jnp.dot(
            n_ref[...], wr_ref[:, src:src + PROJ_PIECE], preferred_element_type=f32)

    pieces = list(range(len(PROJ_SOURCES)))

    def pipelined(stages, fill_plan):
        pending = [stages[t][0](stages[t][2]) for t in range(PIPELINE_DEPTH)]
        for t, (_, finish, item) in enumerate(stages):
            ahead = t + PIPELINE_DEPTH
            if ahead < len(stages):
                pending.append(stages[ahead][0](stages[ahead][2]))
            for _ in range(fill_plan[t]):
                proj_piece(pieces.pop(0))
            finish(item, pending.pop(0))

    a_tiles = A_GROUP // 2
    kmetas_a = [meta_tile(KV_KA + g * LANES) for g in range(A_KV_HEADS)]
    vmetas_a = [meta_tile(KV_VA + g * LANES) for g in range(A_KV_HEADS)]

    def a_window(col, tile, j):
        w0 = HALO + j * BLOCK - BLOCK
        return win_ref[w0:w0 + 3 * BLOCK, col + tile * LANES:col + (tile + 1) * LANES]

    def a_scores(item):
        g, j = item
        first = jnp.logical_and(i == 0, j == 0)
        last = jnp.logical_and(i == n_chunks - 1, j == BLOCKS_PER_CHUNK - 1)
        mask_t = amask_ref[jnp.where(first, 1, jnp.where(last, 2, 0))]
        kcat = jnp.concatenate([kmetas_a[g], a_window(KV_KA, g, j)], axis=0)
        k_aug = jnp.concatenate([kcat, mask_t], axis=1)
        q_tiles = [qa_ref[j * BLOCK:(j + 1) * BLOCK, (a_tiles * g + t) * LANES:(a_tiles * g + t + 1) * LANES]
                   for t in range(a_tiles)]
        zeros = jnp.zeros_like(q_tiles[0])
        qs = jnp.concatenate([jnp.where(lo_half, q, zeros) for q in q_tiles]
                             + [jnp.where(lo_half, zeros, q) for q in q_tiles], axis=0)
        q_aug = jnp.concatenate([qs, eye_ref[...]], axis=1)
        return lax.dot_general(q_aug, k_aug, _NT, preferred_element_type=f32)

    def a_finish(item, s):
        g, j = item
        ps, es = [], []
        for rb in range(A_GROUP):
            sink = sink_ref[A_GROUP * g + 2 * (rb % a_tiles) + rb // a_tiles]
            sh = s[rb * BLOCK:(rb + 1) * BLOCK]
            m = jnp.maximum(jnp.max(sh, axis=-1, keepdims=True), sink)
            ps.append(jnp.exp(sh - m).astype(bf16))
            es.append(jnp.exp(sink - m))
        vcat = jnp.concatenate([vmetas_a[g], a_window(KV_VA, g, j)], axis=0)
        o = jnp.dot(jnp.concatenate(ps, axis=0), with_ones(vcat), preferred_element_type=f32)
        for t in range(a_tiles):
            oe = o[t * BLOCK:(t + 1) * BLOCK]
            oo = o[(a_tiles + t) * BLOCK:(a_tiles + t + 1) * BLOCK]
            den = (jnp.where(lo_half, oe[:, LANES:], oo[:, LANES:])
                   + jnp.where(lo_half, es[t], es[a_tiles + t]))
            oa_ref[j * BLOCK:(j + 1) * BLOCK, (a_tiles * g + t) * LANES:(a_tiles * g + t + 1) * LANES] = (
                jnp.where(lo_half, oe[:, :LANES], oo[:, :LANES]) / den)

    a_stages = [(a_scores, a_finish, (g, j)) for g in range(A_KV_HEADS) for j in range(BLOCKS_PER_CHUNK)]

    kmetas = [meta_tile(KV_KB + p * LANES) for p in range(B_HEADS // 2)]
    vmetas = [meta_tile(KV_VB + p * LANES) for p in range(B_HEADS // 2)]

    def unit_geometry(step):
        r0 = i * ROWS_PER_CHUNK + 2 * step
        rs0 = jnp.clip(r0 - NA_KH // 2, 0, n_rows - NA_KH)
        rs1 = jnp.clip(r0 + 1 - NA_KH // 2, 0, n_rows - NA_KH)
        shifts = (r0 - rs0, r0 + 1 - rs0)
        variant = 1 - (rs1 - rs0)
        woff = pl.multiple_of((rs0 - i * ROWS_PER_CHUNK) * GRID_W + HALO, BLOCK)
        return shifts, variant, woff

    geometry = [unit_geometry(step) for step in range(ROWS_PER_CHUNK // 2)]

    def b_scores(item):
        step, pairs = item
        _, variant, woff = geometry[step]
        out = []
        for p in pairs:
            kcat = jnp.concatenate(
                [win_ref[pl.ds(woff, B_WIN_KEYS), KV_KB + p * LANES:KV_KB + (p + 1) * LANES],
                 kmetas[p]], axis=0)
            k_aug = jnp.concatenate([kcat, brow_ref[variant]], axis=1)
            qs = split_heads(qb_ref[step * BLOCK:(step + 1) * BLOCK, p * LANES:(p + 1) * LANES])
            q_aug = jnp.concatenate([qs, bsel_ref[...]], axis=1)
            out.append(lax.dot_general(q_aug, k_aug, _NT, preferred_element_type=f32))
        return out

    def b_finish(item, scores):
        step, pairs = item
        shifts, _, woff = geometry[step]
        probs = []
        for p, s in zip(pairs, scores):
            bias = jnp.concatenate(
                [jnp.concatenate([bpair_ref[p, 2 * jj + NA_KH - 1 - shifts[u], hh]
                                  for jj in range(B_WIN_ROWS // 2)], axis=1)
                 for hh in range(2) for u in range(2)], axis=0)
            s_loc = s[:, :B_WIN_KEYS] + bias
            s_met = s[:, B_WIN_KEYS:]
            m = jnp.maximum(jnp.max(s_loc, axis=-1, keepdims=True),
                            jnp.max(s_met, axis=-1, keepdims=True))
            probs.append(jnp.concatenate([jnp.exp(s_loc - m), jnp.exp(s_met - m)],
                                         axis=1).astype(bf16))
        for p, pr in zip(pairs, probs):
            vcat = jnp.concatenate(
                [win_ref[pl.ds(woff, B_WIN_KEYS), KV_VB + p * LANES:KV_VB + (p + 1) * LANES],
                 vmetas[p]], axis=0)
            o = jnp.dot(pr, with_ones(vcat), preferred_element_type=f32)
            o = o[:, :LANES] / o[:, LANES:]
            ob_ref[step * BLOCK:(step + 1) * BLOCK, p * LANES:(p + 1) * LANES] = jnp.where(
                lo_half, o[:BLOCK], o[BLOCK:])

    b_stages = [(b_scores, b_finish, (step, tuple(range(p0, p0 + B_PAIRS_PER_ITEM))))
                for step in range(ROWS_PER_CHUNK // 2)
                for p0 in range(0, B_HEADS // 2, B_PAIRS_PER_ITEM)]
    assert len(a_stages) + len(b_stages) == len(FILL_PLAN) and sum(FILL_PLAN) <= len(pieces)
    pipelined(a_stages + b_stages, FILL_PLAN)

    while pieces:
        proj_piece(pieces.pop(0))

    def gated_branch(o_ref, z_col, w_ref, g_col):
        z = proj_ref[:, z_col:z_col + A_WIDTH]
        a = (o_ref[...] * (z * jax.nn.sigmoid(z))).astype(bf16)
        y = jnp.dot(a, w_ref[...], preferred_element_type=f32)
        return jax.nn.sigmoid(proj_ref[:, g_col:g_col + D_MODEL]) * y

    merged = gated_branch(oa_ref, P_ZA, wpa_ref, P_GA) + gated_branch(ob_ref, P_ZB, wpb_ref, P_GB)
    h = x + jnp.dot(merged.astype(bf16), wout_ref[...], preferred_element_type=f32)
    out_ref[0] = _rmsnorm(h, fgain_ref[...])


def _layer_call(sink, x, kv, kv_meta, rope_base, rope_off, gain, fgain, w, w_pa, w_pb, w_out,
                amasks, btables):
    bsz, seq, _ = x.shape
    n_chunks = seq // CHUNK
    halo_per = CHUNK // HALO
    const2 = lambda b, i: (0, 0)
    const3 = lambda b, i: (0, 0, 0)
    resident = dict(pipeline_mode=pl.Buffered(1))
    body = functools.partial(_layer_body, n_chunks=n_chunks, n_rows=seq // GRID_W)
    return pl.pallas_call(
        body,
        grid=(bsz, n_chunks),
        in_specs=[
            pl.BlockSpec(memory_space=pltpu.SMEM),
            pl.BlockSpec((1, CHUNK, D_MODEL), lambda b, i: (b, i, 0)),
            pl.BlockSpec((1, HALO, KV_COLS), lambda b, i: (b, jnp.maximum(i * halo_per - 1, 0), 0)),
            pl.BlockSpec((1, CHUNK, KV_COLS), lambda b, i: (b, i, 0)),
            pl.BlockSpec((1, HALO, KV_COLS),
                         lambda b, i: (b, jnp.minimum((i + 1) * halo_per, n_chunks * halo_per - 1), 0)),
            pl.BlockSpec((1, N_META, KV_COLS), const3, **resident),
            pl.BlockSpec((CHUNK, LANES), const2, **resident),
            pl.BlockSpec((CHUNK, LANES), const2, **resident),
            pl.BlockSpec((None, 2, LANES), lambda b, i: (i, 0, 0)),
            pl.BlockSpec((1, D_MODEL), const2, **resident),
            pl.BlockSpec((1, D_MODEL), const2, **resident),
            pl.BlockSpec((D_MODEL, IN_COLS), const2, **resident),
            pl.BlockSpec((A_WIDTH, D_MODEL), const2, **resident),
            pl.BlockSpec((B_WIDTH, D_MODEL), const2, **resident),
            pl.BlockSpec((D_MODEL, D_MODEL), const2, **resident),
            pl.BlockSpec((3, META_PAD + 3 * BLOCK, BLOCK), const3, **resident),
            pl.BlockSpec((A_GROUP * BLOCK, BLOCK), const2, **resident),
            pl.BlockSpec((B_HEADS // 2, 2 * NA_KH, 2, GRID_W, LANES), lambda b, i: (0, 0, 0, 0, 0),
                         **resident),
            pl.BlockSpec((2, B_WIN_KEYS + META_PAD, LANES), const3, **resident),
            pl.BlockSpec((2 * BLOCK, LANES), const2, **resident),
        ],
        out_specs=pl.BlockSpec((1, CHUNK, D_MODEL), lambda b, i: (b, i, 0)),
        out_shape=jax.ShapeDtypeStruct((bsz, seq, D_MODEL), jnp.float32),
        scratch_shapes=[
            pltpu.VMEM((CHUNK, D_MODEL), jnp.bfloat16),
            pltpu.VMEM((CHUNK + 2 * HALO, KV_COLS), jnp.bfloat16),
            pltpu.VMEM((CHUNK, A_WIDTH), jnp.bfloat16),
            pltpu.VMEM((CHUNK, B_WIDTH), jnp.bfloat16),
            pltpu.VMEM((CHUNK, A_WIDTH), jnp.float32),
            pltpu.VMEM((CHUNK, B_WIDTH), jnp.float32),
            pltpu.VMEM((CHUNK, P_COLS), jnp.float32),
        ],
        compiler_params=pltpu.CompilerParams(
            dimension_semantics=("parallel", "arbitrary"), vmem_limit_bytes=VMEM_LIMIT_BYTES),
        name="hybrid_layer",
    )(sink, x, kv, kv, kv, kv_meta, *rope_base, rope_off, gain, fgain, w, w_pa, w_pb, w_out,
      *amasks, *btables)


def _rope_tables(n_chunks):
    d = np.arange(LANES) % HEAD_DIM
    inv_freq = ROPE_THETA ** (-jnp.asarray(d % ROT_HALF, jnp.float32) / ROT_HALF)
    inv_lane = jnp.where(d < ROT_DIM, inv_freq, 0.0)[None, :]
    base = jnp.arange(CHUNK, dtype=jnp.float32)[:, None] * inv_lane
    off = (N_META + CHUNK * jnp.arange(n_chunks, dtype=jnp.float32))[:, None] * inv_lane
    return (jnp.cos(base), jnp.sin(base)), jnp.stack([jnp.cos(off), jnp.sin(off)], axis=1)


def _band_masks():
    qi = np.arange(BLOCK)[:, None]
    col = np.arange(META_PAD + 3 * BLOCK)[None, :]
    kj = col - META_PAD
    band = (col >= META_PAD) & (np.abs(kj - BLOCK - qi) <= WINDOW)
    meta = np.broadcast_to(col < N_META, band.shape)
    variants = [meta | band, meta | (band & (kj >= BLOCK)), meta | (band & (kj < 2 * BLOCK))]
    masks = np.where(np.stack(variants), 0.0, NEG_INF).astype(np.float32).transpose(0, 2, 1)
    eye = np.tile(np.eye(BLOCK, dtype=np.float32), (A_GROUP, 1))
    return jnp.asarray(masks, jnp.bfloat16), jnp.asarray(eye, jnp.bfloat16)


def _na_bias(rpb):
    cq = np.arange(GRID_W)[:, None]
    kc = np.arange(GRID_W)[None, :]
    start = np.clip(cq - NA_KW // 2, 0, GRID_W - NA_KW)
    valid = (kc >= start) & (kc < start + NA_KW)
    n_pairs, n_dr, n_dc = B_HEADS // 2, 2 * NA_KH - 1, 2 * NA_KW - 1
    onehot = ((kc - cq + NA_KW - 1)[..., None] == np.arange(n_dc)) & valid[..., None]
    select = np.zeros((GRID_W, 2, n_dc, 2, GRID_W), np.float32)
    for t in range(2):
        select[:, t, :, t, :] = onehot.transpose(0, 2, 1)
    mask = np.where(np.tile(valid, (1, 2)), 0.0, NEG_INF).astype(np.float32)
    select = np.concatenate([select.reshape(GRID_W, 2 * n_dc, LANES), mask[:, None, :]], axis=1)
    rows = jnp.pad(rpb.reshape(n_pairs, 2, n_dr, n_dc), ((0, 0), (0, 0), (0, 2), (0, 0)))
    rows = jnp.stack([rows[:, :, :2 * NA_KH], rows[:, :, 1:]], axis=3)
    rows = jnp.concatenate([rows.reshape(n_pairs, 2, 2 * NA_KH, 2 * n_dc),
                            jnp.ones((n_pairs, 2, 2 * NA_KH, 1), jnp.float32)], axis=-1)
    pair = jnp.einsum('phre,cen->prhcn', rows, jnp.asarray(select), precision=lax.Precision.HIGHEST)

    key_row = np.arange(B_WIN_KEYS + META_PAD) // GRID_W
    is_meta = (np.arange(B_WIN_KEYS + META_PAD) >= B_WIN_KEYS) & (
        np.arange(B_WIN_KEYS + META_PAD) < B_WIN_KEYS + N_META)
    rows = np.zeros((2, B_WIN_KEYS + META_PAD, LANES), np.float32)
    for variant, starts in enumerate(((0, 1), (0, 0))):
        for u, s0 in enumerate(starts):
            seen = ((key_row >= s0) & (key_row < s0 + NA_KH)) | is_meta
            rows[variant, :, u] = np.where(seen, 0.0, NEG_INF)
    sel = np.zeros((2, 2, GRID_W, LANES), np.float32)
    sel[:, 0, :, 0] = 1.0
    sel[:, 1, :, 1] = 1.0
    return (pair, jnp.asarray(rows, jnp.bfloat16),
            jnp.asarray(sel.reshape(2 * BLOCK, LANES), jnp.bfloat16))


def kernel(x, meta_tokens, norm_gain, w_in, sink_logits, rel_pos_bias, w_proj_a, w_proj_b, w_out,
           final_norm_gain):
    bsz, seq, _ = x.shape
    assert seq % CHUNK == 0 and seq // GRID_W >= NA_KH and norm_gain.shape[0] == 1
    bf16 = jnp.bfloat16
    w = w_in[0].astype(bf16)

    rope_base, rope_off = _rope_tables(seq // CHUNK)
    gain = norm_gain[0][None]
    kv, kv_meta = _kv_call(x, meta_tokens, gain, w, rope_base, rope_off)
    return _layer_call(sink_logits[0], x, kv, kv_meta, rope_base, rope_off, gain,
                       final_norm_gain[None], w, w_proj_a[0].astype(bf16),
                       w_proj_b[0].astype(bf16), w_out[0].astype(bf16), _band_masks(),
                       _na_bias(rel_pos_bias[0]))
```

```python
import functools

import numpy as np
import jax
import jax.numpy as jnp
from jax import lax
from jax.experimental import pallas as pl
from jax.experimental.pallas import tpu as pltpu

D_MODEL = 1024
N_META = 16
HEAD_DIM = 64
RMS_EPS = 1e-6
NEG_INF = -1e30
A_HEADS = 8
A_KV_HEADS = 2
A_GROUP = A_HEADS // A_KV_HEADS
A_WIDTH = A_HEADS * HEAD_DIM
A_KV_WIDTH = A_KV_HEADS * HEAD_DIM
WINDOW = 128
BLOCK = 128
ROT_DIM = HEAD_DIM // 4
ROT_HALF = ROT_DIM // 2
ROPE_THETA = 500000.0
B_HEADS = 8
B_WIDTH = B_HEADS * HEAD_DIM
GRID_W = 64
NA_KH = 8
NA_KW = 16
SPLIT_SIZES = (A_WIDTH, A_KV_WIDTH, A_KV_WIDTH, A_WIDTH, B_WIDTH, B_WIDTH, B_WIDTH, B_WIDTH,
               D_MODEL, D_MODEL)

LANES = 128
CHUNK = 512
ROWS_PER_CHUNK = CHUNK // GRID_W
BLOCKS_PER_CHUNK = CHUNK // BLOCK
KV_CHUNKS_PER_STEP = 2
HALO = CHUNK // 2
assert HALO >= BLOCK and HALO >= (NA_KH // 2) * GRID_W
META_PAD = LANES
B_WIN_ROWS = NA_KH + 2
B_WIN_KEYS = B_WIN_ROWS * GRID_W
B_PAIRS_PER_ITEM = 2
PIPELINE_DEPTH = 1
PROJ_PIECE = 256
FILL_PLAN = (1, 1, 1, 1, 0, 0, 0, 0, 1, 1, 1, 1, 1, 1, 1, 1)

KV_KA = 0
KV_VA = KV_KA + A_KV_HEADS * LANES
KV_KB = KV_VA + A_KV_HEADS * LANES
KV_VB = KV_KB + B_WIDTH
KV_COLS = KV_VB + B_WIDTH

(C_QA, C_KA, C_VA, C_ZA, C_QB, C_KB, C_VB, C_ZB, C_GA, C_GB, IN_COLS) = (
    int(c) for c in np.cumsum((0,) + SPLIT_SIZES))
assert A_KV_WIDTH == LANES and C_VA == C_KA + A_KV_WIDTH and C_VB == C_KB + B_WIDTH

P_ZA = 0
P_ZB = P_ZA + A_WIDTH
P_GA = P_ZB + B_WIDTH
P_GB = P_GA + D_MODEL
P_COLS = P_GB + D_MODEL
PROJ_SOURCES = tuple(
    src + off for src, width in ((C_ZA, A_WIDTH), (C_ZB, B_WIDTH), (C_GA, D_MODEL), (C_GB, D_MODEL))
    for off in range(0, width, PROJ_PIECE))

VMEM_LIMIT_BYTES = 58 * 1024 * 1024

_NT = (((1,), (1,)), ((), ()))


def _rmsnorm(x, gain):
    return x * lax.rsqrt(jnp.mean(x * x, axis=-1, keepdims=True) + RMS_EPS) * gain


def _lane_patterns():
    lane = lax.broadcasted_iota(jnp.int32, (1, LANES), 1)
    return lane < HEAD_DIM, (lane & (HEAD_DIM - 1)) < ROT_HALF


def _rope_coeffs(cb_ref, sb_ref, off_ref, first_half):
    cb, sb = cb_ref[...], sb_ref[...]
    ca, sa = off_ref[0:1, :], off_ref[1:2, :]
    c = cb * ca - sb * sa
    s = (sb * ca + cb * sa) * jnp.where(first_half, -1.0, 1.0)
    return c, s


def _rope_tile(t, c, s, first_half):
    partner = jnp.where(first_half, pltpu.roll(t, LANES - ROT_HALF, 1), pltpu.roll(t, ROT_HALF, 1))
    return t * c + partner * s


def _kv_body(x_ref, meta_ref, gain_ref, w_ref, cb_ref, sb_ref, off_ref, kv_ref, kvm_ref):
    f32, bf16 = jnp.float32, jnp.bfloat16
    lo_half, first_half = _lane_patterns()
    sign = jnp.where(first_half, -1.0, 1.0)

    def project(n):
        return (jnp.dot(n, w_ref[:, C_KA:C_KA + 2 * A_KV_WIDTH], preferred_element_type=f32),
                jnp.dot(n, w_ref[:, C_KB:C_KB + 2 * B_WIDTH], preferred_element_type=f32))

    def store(dst_ref, rows, kv_a, kv_b, c, s):
        ka = _rope_tile(kv_a[:, :LANES], c, s, first_half)
        va = kv_a[:, LANES:]
        for src, col in ((ka, KV_KA), (va, KV_VA)):
            swapped = pltpu.roll(src, HEAD_DIM, 1)
            dst_ref[0, rows, col:col + LANES] = jnp.where(lo_half, src, swapped).astype(bf16)
            dst_ref[0, rows, col + LANES:col + 2 * LANES] = jnp.where(lo_half, swapped, src).astype(bf16)
        dst_ref[0, rows, KV_KB:] = kv_b.astype(bf16)

    @pl.when(jnp.logical_and(pl.program_id(0) == 0, pl.program_id(1) == 0))
    def _():
        n = _rmsnorm(meta_ref[...], gain_ref[...]).astype(bf16)
        store(kvm_ref, slice(None), *project(n), cb_ref[0:N_META], sb_ref[0:N_META] * sign)

    chunks = [slice(h * CHUNK, (h + 1) * CHUNK) for h in range(KV_CHUNKS_PER_STEP)]
    ns = [_rmsnorm(x_ref[0, rows], gain_ref[...]).astype(bf16) for rows in chunks]
    kvs = [project(n) for n in ns]
    for h, rows in enumerate(chunks):
        c, s = _rope_coeffs(cb_ref, sb_ref, off_ref.at[h], first_half)
        store(kv_ref, rows, *kvs[h], c, s)


def _kv_call(x, meta, gain, w, rope_base, rope_off):
    bsz, seq, _ = x.shape
    rows = KV_CHUNKS_PER_STEP * CHUNK
    const = lambda b, i: (0, 0)
    return pl.pallas_call(
        _kv_body,
        grid=(bsz, seq // rows),
        in_specs=[
            pl.BlockSpec((1, rows, D_MODEL), lambda b, i: (b, i, 0)),
            pl.BlockSpec((N_META, D_MODEL), const, pipeline_mode=pl.Buffered(1)),
            pl.BlockSpec((1, D_MODEL), const),
            pl.BlockSpec((D_MODEL, IN_COLS), const, pipeline_mode=pl.Buffered(1)),
            pl.BlockSpec((CHUNK, LANES), const),
            pl.BlockSpec((CHUNK, LANES), const),
            pl.BlockSpec((KV_CHUNKS_PER_STEP, 2, LANES), lambda b, i: (i, 0, 0)),
        ],
        out_specs=[pl.BlockSpec((1, rows, KV_COLS), lambda b, i: (b, i, 0)),
                   pl.BlockSpec((1, N_META, KV_COLS), lambda b, i: (0, 0, 0))],
        out_shape=[jax.ShapeDtypeStruct((bsz, seq, KV_COLS), jnp.bfloat16),
                   jax.ShapeDtypeStruct((1, N_META, KV_COLS), jnp.bfloat16)],
        compiler_params=pltpu.CompilerParams(
            dimension_semantics=("arbitrary", "arbitrary"), vmem_limit_bytes=VMEM_LIMIT_BYTES),
        name="kv_proj",
    )(x, meta, gain, w, *rope_base, rope_off)


def _layer_body(sink_ref, x_ref, kvp_ref, kvc_ref, kvn_ref, kvm_ref, cb_ref, sb_ref, off_ref,
                gain_ref, fgain_ref, wr_ref, wpa_ref, wpb_ref, wout_ref, amask_ref, eye_ref, bpair_ref,
                brow_ref, bsel_ref,
                out_ref, n_ref, win_ref, qa_ref, qb_ref, oa_ref, ob_ref, proj_ref, *, n_chunks, n_rows):
    i = pl.program_id(1)
    f32, bf16 = jnp.float32, jnp.bfloat16
    scale = HEAD_DIM ** -0.5

    x = x_ref[0]
    n_ref[...] = _rmsnorm(x, gain_ref[...]).astype(bf16)

    win_ref[0:HALO] = kvp_ref[0]
    win_ref[HALO:HALO + CHUNK] = kvc_ref[0]
    win_ref[HALO + CHUNK:CHUNK + 2 * HALO] = kvn_ref[0]

    lo_half, first_half = _lane_patterns()
    c, s = _rope_coeffs(cb_ref, sb_ref, off_ref, first_half)
    qa = jnp.dot(n_ref[...], wr_ref[:, C_QA:C_QA + A_WIDTH], preferred_element_type=f32)
    for t in range(A_WIDTH // LANES):
        sl = slice(t * LANES, (t + 1) * LANES)
        qa_ref[:, sl] = (_rope_tile(qa[:, sl], c, s, first_half) * scale).astype(bf16)
    qb = jnp.dot(n_ref[...], wr_ref[:, C_QB:C_QB + B_WIDTH], preferred_element_type=f32)
    qb_ref[...] = (qb * scale).astype(bf16)

    zero_pad = jnp.zeros((META_PAD - N_META, LANES), bf16)

    def meta_tile(col):
        return jnp.concatenate([kvm_ref[0, :, col:col + LANES], zero_pad], axis=0)

    def split_heads(q):
        zeros = jnp.zeros_like(q)
        return jnp.concatenate([jnp.where(lo_half, q, zeros), jnp.where(lo_half, zeros, q)], axis=0)

    def with_ones(v):
        return jnp.concatenate([v, jnp.ones_like(v)], axis=1)

    def proj_piece(k):
        src = PROJ_SOURCES[k]
        proj_ref[:, k * PROJ_PIECE:(k + 1) * PROJ_PIECE] = jnp.dot(
            n_ref[...], wr_ref[:, src:src + PROJ_PIECE], preferred_element_type=f32)

    pieces = list(range(len(PROJ_SOURCES)))

    def pipelined(stages, fill_plan):
        pending = [stages[t][0](stages[t][2]) for t in range(PIPELINE_DEPTH)]
        for t, (_, finish, item) in enumerate(stages):
            ahead = t + PIPELINE_DEPTH
            if ahead < len(stages):
                pending.append(stages[ahead][0](stages[ahead][2]))
            for _ in range(fill_plan[t]):
                proj_piece(pieces.pop(0))
            finish(item, pending.pop(0))

    a_tiles = A_GROUP // 2
    kmetas_a = [meta_tile(KV_KA + g * LANES) for g in range(A_KV_HEADS)]
    vmetas_a = [meta_tile(KV_VA + g * LANES) for g in range(A_KV_HEADS)]

    def a_window(col, tile, j):
        w0 = HALO + j * BLOCK - BLOCK
        return win_ref[w0:w0 + 3 * BLOCK, col + tile * LANES:col + (tile + 1) * LANES]

    def a_scores(item):
        g, j = item
        first = jnp.logical_and(i == 0, j == 0)
        last = jnp.logical_and(i == n_chunks - 1, j == BLOCKS_PER_CHUNK - 1)
        mask_t = amask_ref[jnp.where(first, 1, jnp.where(last, 2, 0))]
        kcat = jnp.concatenate([kmetas_a[g], a_window(KV_KA, g, j)], axis=0)
        k_aug = jnp.concatenate([kcat, mask_t], axis=1)
        q_tiles = [qa_ref[j * BLOCK:(j + 1) * BLOCK, (a_tiles * g + t) * LANES:(a_tiles * g + t + 1) * LANES]
                   for t in range(a_tiles)]
        zeros = jnp.zeros_like(q_tiles[0])
        qs = jnp.concatenate([jnp.where(lo_half, q, zeros) for q in q_tiles]
                             + [jnp.where(lo_half, zeros, q) for q in q_tiles], axis=0)
        q_aug = jnp.concatenate([qs, eye_ref[...]], axis=1)
        return lax.dot_general(q_aug, k_aug, _NT, preferred_element_type=f32)

    def a_finish(item, s):
        g, j = item
        ps, es = [], []
        for rb in range(A_GROUP):
            sink = sink_ref[A_GROUP * g + 2 * (rb % a_tiles) + rb // a_tiles]
            sh = s[rb * BLOCK:(rb + 1) * BLOCK]
            m = jnp.maximum(jnp.max(sh, axis=-1, keepdims=True), sink)
            ps.append(jnp.exp(sh - m).astype(bf16))
            es.append(jnp.exp(sink - m))
        vcat = jnp.concatenate([vmetas_a[g], a_window(KV_VA, g, j)], axis=0)
        o = jnp.dot(jnp.concatenate(ps, axis=0), with_ones(vcat), preferred_element_type=f32)
        for t in range(a_tiles):
            oe = o[t * BLOCK:(t + 1) * BLOCK]
            oo = o[(a_tiles + t) * BLOCK:(a_tiles + t + 1) * BLOCK]
            den = (jnp.where(lo_half, oe[:, LANES:], oo[:, LANES:])
                   + jnp.where(lo_half, es[t], es[a_tiles + t]))
            oa_ref[j * BLOCK:(j + 1) * BLOCK, (a_tiles * g + t) * LANES:(a_tiles * g + t + 1) * LANES] = (
                jnp.where(lo_half, oe[:, :LANES], oo[:, :LANES]) / den)

    a_stages = [(a_scores, a_finish, (g, j)) for g in range(A_KV_HEADS) for j in range(BLOCKS_PER_CHUNK)]

    kmetas = [meta_tile(KV_KB + p * LANES) for p in range(B_HEADS // 2)]
    vmetas = [meta_tile(KV_VB + p * LANES) for p in range(B_HEADS // 2)]

    def unit_geometry(step):
        r0 = i * ROWS_PER_CHUNK + 2 * step
        rs0 = jnp.clip(r0 - NA_KH // 2, 0, n_rows - NA_KH)
        rs1 = jnp.clip(r0 + 1 - NA_KH // 2, 0, n_rows - NA_KH)
        shifts = (r0 - rs0, r0 + 1 - rs0)
        variant = 1 - (rs1 - rs0)
        woff = pl.multiple_of((rs0 - i * ROWS_PER_CHUNK) * GRID_W + HALO, BLOCK)
        return shifts, variant, woff

    geometry = [unit_geometry(step) for step in range(ROWS_PER_CHUNK // 2)]

    def b_scores(item):
        step, pairs = item
        _, variant, woff = geometry[step]
        out = []
        for p in pairs:
            kcat = jnp.concatenate(
                [win_ref[pl.ds(woff, B_WIN_KEYS), KV_KB + p * LANES:KV_KB + (p + 1) * LANES],
                 kmetas[p]], axis=0)
            k_aug = jnp.concatenate([kcat, brow_ref[variant]], axis=1)
            qs = split_heads(qb_ref[step * BLOCK:(step + 1) * BLOCK, p * LANES:(p + 1) * LANES])
            q_aug = jnp.concatenate([qs, bsel_ref[...]], axis=1)
            out.append(lax.dot_general(q_aug, k_aug, _NT, preferred_element_type=f32))
        return out

    def b_finish(item, scores):
        step, pairs = item
        shifts, _, woff = geometry[step]
        probs = []
        for p, s in zip(pairs, scores):
            bias = jnp.concatenate(
                [jnp.concatenate([bpair_ref[p, 2 * jj + NA_KH - 1 - shifts[u], hh]
                                  for jj in range(B_WIN_ROWS // 2)], axis=1)
                 for hh in range(2) for u in range(2)], axis=0)
            s_loc = s[:, :B_WIN_KEYS] + bias
            s_met = s[:, B_WIN_KEYS:]
            m = jnp.maximum(jnp.max(s_loc, axis=-1, keepdims=True),
                            jnp.max(s_met, axis=-1, keepdims=True))
            probs.append(jnp.concatenate([jnp.exp(s_loc - m), jnp.exp(s_met - m)],
                                         axis=1).astype(bf16))
        for p, pr in zip(pairs, probs):
            vcat = jnp.concatenate(
                [win_ref[pl.ds(woff, B_WIN_KEYS), KV_VB + p * LANES:KV_VB + (p + 1) * LANES],
                 vmetas[p]], axis=0)
            o = jnp.dot(pr, with_ones(vcat), preferred_element_type=f32)
            o = o[:, :LANES] / o[:, LANES:]
            ob_ref[step * BLOCK:(step + 1) * BLOCK, p * LANES:(p + 1) * LANES] = jnp.where(
                lo_half, o[:BLOCK], o[BLOCK:])

    b_stages = [(b_scores, b_finish, (step, tuple(range(p0, p0 + B_PAIRS_PER_ITEM))))
                for step in range(ROWS_PER_CHUNK // 2)
                for p0 in range(0, B_HEADS // 2, B_PAIRS_PER_ITEM)]
    assert len(a_stages) + len(b_stages) == len(FILL_PLAN) and sum(FILL_PLAN) <= len(pieces)
    pipelined(a_stages + b_stages, FILL_PLAN)

    while pieces:
        proj_piece(pieces.pop(0))

    def gated_branch(o_ref, z_col, w_ref, g_col):
        z = proj_ref[:, z_col:z_col + A_WIDTH]
        a = (o_ref[...] * (z * jax.nn.sigmoid(z))).astype(bf16)
        y = jnp.dot(a, w_ref[...], preferred_element_type=f32)
        return jax.nn.sigmoid(proj_ref[:, g_col:g_col + D_MODEL]) * y

    merged = gated_branch(oa_ref, P_ZA, wpa_ref, P_GA) + gated_branch(ob_ref, P_ZB, wpb_ref, P_GB)
    h = x + jnp.dot(merged.astype(bf16), wout_ref[...], preferred_element_type=f32)
    out_ref[0] = _rmsnorm(h, fgain_ref[...])


def _layer_call(sink, x, kv, kv_meta, rope_base, rope_off, gain, fgain, w, w_pa, w_pb, w_out,
                amasks, btables):
    bsz, seq, _ = x.shape
    n_chunks = seq // CHUNK
    halo_per = CHUNK // HALO
    const2 = lambda b, i: (0, 0)
    const3 = lambda b, i: (0, 0, 0)
    resident = dict(pipeline_mode=pl.Buffered(1))
    body = functools.partial(_layer_body, n_chunks=n_chunks, n_rows=seq // GRID_W)
    return pl.pallas_call(
        body,
        grid=(bsz, n_chunks),
        in_specs=[
            pl.BlockSpec(memory_space=pltpu.SMEM),
            pl.BlockSpec((1, CHUNK, D_MODEL), lambda b, i: (b, i, 0)),
            pl.BlockSpec((1, HALO, KV_COLS), lambda b, i: (b, jnp.maximum(i * halo_per - 1, 0), 0)),
            pl.BlockSpec((1, CHUNK, KV_COLS), lambda b, i: (b, i, 0)),
            pl.BlockSpec((1, HALO, KV_COLS),
                         lambda b, i: (b, jnp.minimum((i + 1) * halo_per, n_chunks * halo_per - 1), 0)),
            pl.BlockSpec((1, N_META, KV_COLS), const3, **resident),
            pl.BlockSpec((CHUNK, LANES), const2, **resident),
            pl.BlockSpec((CHUNK, LANES), const2, **resident),
            pl.BlockSpec((None, 2, LANES), lambda b, i: (i, 0, 0)),
            pl.BlockSpec((1, D_MODEL), const2, **resident),
            pl.BlockSpec((1, D_MODEL), const2, **resident),
            pl.BlockSpec((D_MODEL, IN_COLS), const2, **resident),
            pl.BlockSpec((A_WIDTH, D_MODEL), const2, **resident),
            pl.BlockSpec((B_WIDTH, D_MODEL), const2, **resident),
            pl.BlockSpec((D_MODEL, D_MODEL), const2, **resident),
            pl.BlockSpec((3, META_PAD + 3 * BLOCK, BLOCK), const3, **resident),
            pl.BlockSpec((A_GROUP * BLOCK, BLOCK), const2, **resident),
            pl.BlockSpec((B_HEADS // 2, 2 * NA_KH, 2, GRID_W, LANES), lambda b, i: (0, 0, 0, 0, 0),
                         **resident),
            pl.BlockSpec((2, B_WIN_KEYS + META_PAD, LANES), const3, **resident),
            pl.BlockSpec((2 * BLOCK, LANES), const2, **resident),
        ],
        out_specs=pl.BlockSpec((1, CHUNK, D_MODEL), lambda b, i: (b, i, 0)),
        out_shape=jax.ShapeDtypeStruct((bsz, seq, D_MODEL), jnp.float32),
        scratch_shapes=[
            pltpu.VMEM((CHUNK, D_MODEL), jnp.bfloat16),
            pltpu.VMEM((CHUNK + 2 * HALO, KV_COLS), jnp.bfloat16),
            pltpu.VMEM((CHUNK, A_WIDTH), jnp.bfloat16),
            pltpu.VMEM((CHUNK, B_WIDTH), jnp.bfloat16),
            pltpu.VMEM((CHUNK, A_WIDTH), jnp.float32),
            pltpu.VMEM((CHUNK, B_WIDTH), jnp.float32),
            pltpu.VMEM((CHUNK, P_COLS), jnp.float32),
        ],
        compiler_params=pltpu.CompilerParams(
            dimension_semantics=("parallel", "arbitrary"), vmem_limit_bytes=VMEM_LIMIT_BYTES),
        name="hybrid_layer",
    )(sink, x, kv, kv, kv, kv_meta, *rope_base, rope_off, gain, fgain, w, w_pa, w_pb, w_out,
      *amasks, *btables)


def _rope_tables(n_chunks):
    d = np.arange(LANES) % HEAD_DIM
    inv_freq = ROPE_THETA ** (-jnp.asarray(d % ROT_HALF, jnp.float32) / ROT_HALF)
    inv_lane = jnp.where(d < ROT_DIM, inv_freq, 0.0)[None, :]
    base = jnp.arange(CHUNK, dtype=jnp.float32)[:, None] * inv_lane
    off = (N_META + CHUNK * jnp.arange(n_chunks, dtype=jnp.float32))[:, None] * inv_lane
    return (jnp.cos(base), jnp.sin(base)), jnp.stack([jnp.cos(off), jnp.sin(off)], axis=1)


def _band_masks():
    qi = np.arange(BLOCK)[:, None]
    col = np.arange(META_PAD + 3 * BLOCK)[None, :]
    kj = col - META_PAD
    band = (col >= META_PAD) & (np.abs(kj - BLOCK - qi) <= WINDOW)
    meta = np.broadcast_to(col < N_META, band.shape)
    variants = [meta | band, meta | (band & (kj >= BLOCK)), meta | (band & (kj < 2 * BLOCK))]
    masks = np.where(np.stack(variants), 0.0, NEG_INF).astype(np.float32).transpose(0, 2, 1)
    eye = np.tile(np.eye(BLOCK, dtype=np.float32), (A_GROUP, 1))
    return jnp.asarray(masks, jnp.bfloat16), jnp.asarray(eye, jnp.bfloat16)


def _na_bias(rpb):
    cq = np.arange(GRID_W)[:, None]
    kc = np.arange(GRID_W)[None, :]
    start = np.clip(cq - NA_KW // 2, 0, GRID_W - NA_KW)
    valid = (kc >= start) & (kc < start + NA_KW)
    n_pairs, n_dr, n_dc = B_HEADS // 2, 2 * NA_KH - 1, 2 * NA_KW - 1
    onehot = ((kc - cq + NA_KW - 1)[..., None] == np.arange(n_dc)) & valid[..., None]
    select = np.zeros((GRID_W, 2, n_dc, 2, GRID_W), np.float32)
    for t in range(2):
        select[:, t, :, t, :] = onehot.transpose(0, 2, 1)
    mask = np.where(np.tile(valid, (1, 2)), 0.0, NEG_INF).astype(np.float32)
    select = np.concatenate([select.reshape(GRID_W, 2 * n_dc, LANES), mask[:, None, :]], axis=1)
    rows = jnp.pad(rpb.reshape(n_pairs, 2, n_dr, n_dc), ((0, 0), (0, 0), (0, 2), (0, 0)))
    rows = jnp.stack([rows[:, :, :2 * NA_KH], rows[:, :, 1:]], axis=3)
    rows = jnp.concatenate([rows.reshape(n_pairs, 2, 2 * NA_KH, 2 * n_dc),
                            jnp.ones((n_pairs, 2, 2 * NA_KH, 1), jnp.float32)], axis=-1)
    pair = jnp.einsum('prhe,cen->prhcn', jnp.transpose(rows, (0, 2, 1, 3)), jnp.asarray(select),
                      precision=lax.Precision.HIGHEST)

    key_row = np.arange(B_WIN_KEYS + META_PAD) // GRID_W
    is_meta = (np.arange(B_WIN_KEYS + META_PAD) >= B_WIN_KEYS) & (
        np.arange(B_WIN_KEYS + META_PAD) < B_WIN_KEYS + N_META)
    rows = np.zeros((2, B_WIN_KEYS + META_PAD, LANES), np.float32)
    for variant, starts in enumerate(((0, 1), (0, 0))):
        for u, s0 in enumerate(starts):
            seen = ((key_row >= s0) & (key_row < s0 + NA_KH)) | is_meta
            rows[variant, :, u] = np.where(seen, 0.0, NEG_INF)
    sel = np.zeros((2, 2, GRID_W, LANES), np.float32)
    sel[:, 0, :, 0] = 1.0
    sel[:, 1, :, 1] = 1.0
    return (pair, jnp.asarray(rows, jnp.bfloat16),
            jnp.asarray(sel.reshape(2 * BLOCK, LANES), jnp.bfloat16))


def kernel(x, meta_tokens, norm_gain, w_in, sink_logits, rel_pos_bias, w_proj_a, w_proj_b, w_out,
           final_norm_gain):
    bsz, seq, _ = x.shape
    assert seq % CHUNK == 0 and seq // GRID_W >= NA_KH and norm_gain.shape[0] == 1
    bf16 = jnp.bfloat16
    w = w_in[0].astype(bf16)

    rope_base, rope_off = _rope_tables(seq // CHUNK)
    gain = norm_gain[0][None]
    kv, kv_meta = _kv_call(x, meta_tokens, gain, w, rope_base, rope_off)
    return _layer_call(sink_logits[0], x, kv, kv_meta, rope_base, rope_off, gain,
                       final_norm_gain[None], w, w_proj_a[0].astype(bf16),
                       w_proj_b[0].astype(bf16), w_out[0].astype(bf16), _band_masks(),
                       _na_bias(rel_pos_bias[0]))
```

```python
import functools

import numpy as np
import jax
import jax.numpy as jnp
from jax import lax
from jax.experimental import pallas as pl
from jax.experimental.pallas import tpu as pltpu

D_MODEL = 1024
N_META = 16
HEAD_DIM = 64
RMS_EPS = 1e-6
NEG_INF = -1e30
A_HEADS = 8
A_KV_HEADS = 2
A_GROUP = A_HEADS // A_KV_HEADS
A_WIDTH = A_HEADS * HEAD_DIM
A_KV_WIDTH = A_KV_HEADS * HEAD_DIM
WINDOW = 128
BLOCK = 128
ROT_DIM = HEAD_DIM // 4
ROT_HALF = ROT_DIM // 2
ROPE_THETA = 500000.0
B_HEADS = 8
B_WIDTH = B_HEADS * HEAD_DIM
GRID_W = 64
NA_KH = 8
NA_KW = 16
SPLIT_SIZES = (A_WIDTH, A_KV_WIDTH, A_KV_WIDTH, A_WIDTH, B_WIDTH, B_WIDTH, B_WIDTH, B_WIDTH,
               D_MODEL, D_MODEL)

LANES = 128
CHUNK = 512
ROWS_PER_CHUNK = CHUNK // GRID_W
BLOCKS_PER_CHUNK = CHUNK // BLOCK
HALO = CHUNK // 2
assert HALO >= BLOCK and HALO >= (NA_KH // 2) * GRID_W
META_PAD = LANES
B_WIN_ROWS = NA_KH + 2
B_WIN_KEYS = B_WIN_ROWS * GRID_W
B_PAIRS_PER_ITEM = 2
PROJ_PIECE = 256
FILL_PLAN = (2, 1, 1, 1, 2, 1, 1, 1, 1, 1, 1, 1, 1, 1, 1, 1)

KV_KA = 0
KV_VA = KV_KA + A_KV_HEADS * LANES
KV_KB = KV_VA + A_KV_HEADS * LANES
KV_VB = KV_KB + B_WIDTH
KV_COLS = KV_VB + B_WIDTH

(C_QA, C_KA, C_VA, C_ZA, C_QB, C_KB, C_VB, C_ZB, C_GA, C_GB, IN_COLS) = (
    int(c) for c in np.cumsum((0,) + SPLIT_SIZES))
assert A_KV_WIDTH == LANES and C_VA == C_KA + A_KV_WIDTH and C_VB == C_KB + B_WIDTH

P_ZA = 0
P_ZB = P_ZA + A_WIDTH
P_GA = P_ZB + B_WIDTH
P_GB = P_GA + D_MODEL
P_COLS = P_GB + D_MODEL
PROJ_SOURCES = tuple(
    src + off for src, width in ((C_ZA, A_WIDTH), (C_ZB, B_WIDTH), (C_GA, D_MODEL), (C_GB, D_MODEL))
    for off in range(0, width, PROJ_PIECE))

VMEM_LIMIT_BYTES = 58 * 1024 * 1024

_NT = (((1,), (1,)), ((), ()))


def _rmsnorm(x, gain):
    return x * lax.rsqrt(jnp.mean(x * x, axis=-1, keepdims=True) + RMS_EPS) * gain


def _lane_patterns():
    lane = lax.broadcasted_iota(jnp.int32, (1, LANES), 1)
    return lane < HEAD_DIM, (lane & (HEAD_DIM - 1)) < ROT_HALF


def _rope_coeffs(cb_ref, sb_ref, off_ref, first_half):
    cb, sb = cb_ref[...], sb_ref[...]
    ca, sa = off_ref[0:1, :], off_ref[1:2, :]
    c = cb * ca - sb * sa
    s = (sb * ca + cb * sa) * jnp.where(first_half, -1.0, 1.0)
    return c, s


def _rope_tile(t, c, s, first_half):
    partner = jnp.where(first_half, pltpu.roll(t, LANES - ROT_HALF, 1), pltpu.roll(t, ROT_HALF, 1))
    return t * c + partner * s


def _layer_body(sink_ref, x_ref, xn_ref, meta_ref, cb_ref, sb_ref, off_ref, offn_ref,
                gain_ref, fgain_ref, wr_ref, wpa_ref, wpb_ref, wout_ref, amask_ref, eye_ref, bpair_ref,
                brow_ref, bsel_ref,
                out_ref, n_ref, ring_ref, kvm_ref, win_ref, qa_ref, qb_ref, oa_ref, ob_ref, proj_ref,
                *, n_chunks, n_rows):
    step = pl.program_id(0)
    i = step % n_chunks
    cur, nxt = step % 2, (step + 1) % 2
    slot_prev, slot_cur, slot_nxt = (step + 2) % 3, step % 3, (step + 1) % 3
    f32, bf16 = jnp.float32, jnp.bfloat16
    scale = HEAD_DIM ** -0.5
    lo_half, first_half = _lane_patterns()

    def kv_a_piece(n, c, s, store):
        kv_a = jnp.dot(n, wr_ref[:, C_KA:C_KA + 2 * A_KV_WIDTH], preferred_element_type=f32)
        ka = _rope_tile(kv_a[:, :LANES], c, s, first_half)
        va = kv_a[:, LANES:]
        for src, col in ((ka, KV_KA), (va, KV_VA)):
            swapped = pltpu.roll(src, HEAD_DIM, 1)
            store(col, jnp.where(lo_half, src, swapped).astype(bf16))
            store(col + LANES, jnp.where(lo_half, swapped, src).astype(bf16))

    def kv_b_piece(n, k, store):
        src = C_KB + k * PROJ_PIECE
        store(KV_KB + k * PROJ_PIECE,
              jnp.dot(n, wr_ref[:, src:src + PROJ_PIECE], preferred_element_type=f32).astype(bf16))

    n_kv_b = 2 * B_WIDTH // PROJ_PIECE

    def ring_store(slot):
        def store(col, value):
            ring_ref[slot, :, col:col + value.shape[1]] = value
        return store

    def meta_store(col, value):
        kvm_ref[:, col:col + value.shape[1]] = value

    @pl.when(step == 0)
    def _():
        ring_ref[...] = jnp.zeros(ring_ref.shape, bf16)
        n_ref[0] = _rmsnorm(x_ref[0], gain_ref[...]).astype(bf16)
        c0, s0 = _rope_coeffs(cb_ref, sb_ref, off_ref, first_half)
        kv_a_piece(n_ref[0], c0, s0, ring_store(0))
        for k in range(n_kv_b):
            kv_b_piece(n_ref[0], k, ring_store(0))
        n_meta = _rmsnorm(meta_ref[...], gain_ref[...]).astype(bf16)
        sign = jnp.where(first_half, -1.0, 1.0)
        kv_a_piece(n_meta, cb_ref[0:N_META], sb_ref[0:N_META] * sign, meta_store)
        for k in range(n_kv_b):
            kv_b_piece(n_meta, k, meta_store)

    x = x_ref[0]
    n_ref[nxt] = _rmsnorm(xn_ref[0], gain_ref[...]).astype(bf16)

    win_ref[0:HALO] = ring_ref[slot_prev, CHUNK - HALO:CHUNK]
    win_ref[HALO:HALO + CHUNK] = ring_ref[slot_cur]

    def next_halo():
        win_ref[HALO + CHUNK:CHUNK + 2 * HALO] = ring_ref[slot_nxt, 0:HALO]

    c, s = _rope_coeffs(cb_ref, sb_ref, off_ref, first_half)
    qa = jnp.dot(n_ref[cur], wr_ref[:, C_QA:C_QA + A_WIDTH], preferred_element_type=f32)
    for t in range(A_WIDTH // LANES):
        sl = slice(t * LANES, (t + 1) * LANES)
        qa_ref[:, sl] = (_rope_tile(qa[:, sl], c, s, first_half) * scale).astype(bf16)
    qb = jnp.dot(n_ref[cur], wr_ref[:, C_QB:C_QB + B_WIDTH], preferred_element_type=f32)
    qb_ref[...] = (qb * scale).astype(bf16)

    zero_pad = jnp.zeros((META_PAD - N_META, LANES), bf16)

    def meta_tile(col):
        return jnp.concatenate([kvm_ref[:, col:col + LANES], zero_pad], axis=0)

    def split_heads(q):
        zeros = jnp.zeros_like(q)
        return jnp.concatenate([jnp.where(lo_half, q, zeros), jnp.where(lo_half, zeros, q)], axis=0)

    def with_ones(v):
        return jnp.concatenate([v, jnp.ones_like(v)], axis=1)

    def proj_piece(k):
        src = PROJ_SOURCES[k]
        proj_ref[:, k * PROJ_PIECE:(k + 1) * PROJ_PIECE] = jnp.dot(
            n_ref[cur], wr_ref[:, src:src + PROJ_PIECE], preferred_element_type=f32)

    def next_kv_a():
        cn, sn = _rope_coeffs(cb_ref, sb_ref, offn_ref, first_half)
        kv_a_piece(n_ref[nxt], cn, sn, ring_store(slot_nxt))

    def next_kv_b(k):
        kv_b_piece(n_ref[nxt], k, ring_store(slot_nxt))

    fills = ([next_kv_a] + [functools.partial(next_kv_b, k) for k in range(n_kv_b)] + [next_halo]
             + [functools.partial(proj_piece, k) for k in range(len(PROJ_SOURCES))])

    def pipelined(stages, fill_plan):
        pending = stages[0][0](stages[0][2])
        for t, (_, finish, item) in enumerate(stages):
            nxt_scores = stages[t + 1][0](stages[t + 1][2]) if t + 1 < len(stages) else None
            for _ in range(fill_plan[t]):
                fills.pop(0)()
            finish(item, pending)
            pending = nxt_scores

    a_tiles = A_GROUP // 2
    kmetas_a = [meta_tile(KV_KA + g * LANES) for g in range(A_KV_HEADS)]
    vmetas_a = [meta_tile(KV_VA + g * LANES) for g in range(A_KV_HEADS)]

    def a_window(col, tile, j):
        w0 = HALO + j * BLOCK - BLOCK
        return win_ref[w0:w0 + 3 * BLOCK, col + tile * LANES:col + (tile + 1) * LANES]

    def a_scores(item):
        g, j = item
        first = jnp.logical_and(i == 0, j == 0)
        last = jnp.logical_and(i == n_chunks - 1, j == BLOCKS_PER_CHUNK - 1)
        mask_t = amask_ref[jnp.where(first, 1, jnp.where(last, 2, 0))]
        kcat = jnp.concatenate([kmetas_a[g], a_window(KV_KA, g, j)], axis=0)
        k_aug = jnp.concatenate([kcat, mask_t], axis=1)
        q_tiles = [qa_ref[j * BLOCK:(j + 1) * BLOCK, (a_tiles * g + t) * LANES:(a_tiles * g + t + 1) * LANES]
                   for t in range(a_tiles)]
        zeros = jnp.zeros_like(q_tiles[0])
        qs = jnp.concatenate([jnp.where(lo_half, q, zeros) for q in q_tiles]
                             + [jnp.where(lo_half, zeros, q) for q in q_tiles], axis=0)
        q_aug = jnp.concatenate([qs, eye_ref[...]], axis=1)
        return lax.dot_general(q_aug, k_aug, _NT, preferred_element_type=f32)

    def a_finish(item, s):
        g, j = item
        ps, es = [], []
        for rb in range(A_GROUP):
            sink = sink_ref[A_GROUP * g + 2 * (rb % a_tiles) + rb // a_tiles]
            sh = s[rb * BLOCK:(rb + 1) * BLOCK]
            m = jnp.maximum(jnp.max(sh, axis=-1, keepdims=True), sink)
            ps.append(jnp.exp(sh - m).astype(bf16))
            es.append(jnp.exp(sink - m))
        vcat = jnp.concatenate([vmetas_a[g], a_window(KV_VA, g, j)], axis=0)
        o = jnp.dot(jnp.concatenate(ps, axis=0), with_ones(vcat), preferred_element_type=f32)
        for t in range(a_tiles):
            oe = o[t * BLOCK:(t + 1) * BLOCK]
            oo = o[(a_tiles + t) * BLOCK:(a_tiles + t + 1) * BLOCK]
            den = (jnp.where(lo_half, oe[:, LANES:], oo[:, LANES:])
                   + jnp.where(lo_half, es[t], es[a_tiles + t]))
            oa_ref[j * BLOCK:(j + 1) * BLOCK, (a_tiles * g + t) * LANES:(a_tiles * g + t + 1) * LANES] = (
                jnp.where(lo_half, oe[:, :LANES], oo[:, :LANES]) / den)

    a_stages = [(a_scores, a_finish, (g, j)) for last in (False, True) for g in range(A_KV_HEADS)
                for j in range(BLOCKS_PER_CHUNK) if (j == BLOCKS_PER_CHUNK - 1) == last]

    kmetas = [meta_tile(KV_KB + p * LANES) for p in range(B_HEADS // 2)]
    vmetas = [meta_tile(KV_VB + p * LANES) for p in range(B_HEADS // 2)]

    def unit_geometry(step):
        r0 = i * ROWS_PER_CHUNK + 2 * step
        rs0 = jnp.clip(r0 - NA_KH // 2, 0, n_rows - NA_KH)
        rs1 = jnp.clip(r0 + 1 - NA_KH // 2, 0, n_rows - NA_KH)
        shifts = (r0 - rs0, r0 + 1 - rs0)
        variant = 1 - (rs1 - rs0)
        woff = pl.multiple_of((rs0 - i * ROWS_PER_CHUNK) * GRID_W + HALO, BLOCK)
        return shifts, variant, woff

    geometry = [unit_geometry(step) for step in range(ROWS_PER_CHUNK // 2)]

    def b_scores(item):
        step, pairs = item
        _, variant, woff = geometry[step]
        out = []
        for p in pairs:
            kcat = jnp.concatenate(
                [win_ref[pl.ds(woff, B_WIN_KEYS), KV_KB + p * LANES:KV_KB + (p + 1) * LANES],
                 kmetas[p]], axis=0)
            k_aug = jnp.concatenate([kcat, brow_ref[variant]], axis=1)
            qs = split_heads(qb_ref[step * BLOCK:(step + 1) * BLOCK, p * LANES:(p + 1) * LANES])
            q_aug = jnp.concatenate([qs, bsel_ref[...]], axis=1)
            out.append(lax.dot_general(q_aug, k_aug, _NT, preferred_element_type=f32))
        return out

    def b_finish(item, scores):
        step, pairs = item
        shifts, _, woff = geometry[step]
        probs = []
        for p, s in zip(pairs, scores):
            bias = jnp.concatenate(
                [jnp.concatenate([bpair_ref[p, 2 * jj + NA_KH - 1 - shifts[u], hh]
                                  for jj in range(B_WIN_ROWS // 2)], axis=1)
                 for hh in range(2) for u in range(2)], axis=0)
            s_loc = s[:, :B_WIN_KEYS] + bias
            s_met = s[:, B_WIN_KEYS:]
            m = jnp.maximum(jnp.max(s_loc, axis=-1, keepdims=True),
                            jnp.max(s_met, axis=-1, keepdims=True))
            probs.append(jnp.concatenate([jnp.exp(s_loc - m), jnp.exp(s_met - m)],
                                         axis=1).astype(bf16))
        for p, pr in zip(pairs, probs):
            vcat = jnp.concatenate(
                [win_ref[pl.ds(woff, B_WIN_KEYS), KV_VB + p * LANES:KV_VB + (p + 1) * LANES],
                 vmetas[p]], axis=0)
            o = jnp.dot(pr, with_ones(vcat), preferred_element_type=f32)
            o = o[:, :LANES] / o[:, LANES:]
            ob_ref[step * BLOCK:(step + 1) * BLOCK, p * LANES:(p + 1) * LANES] = jnp.where(
                lo_half, o[:BLOCK], o[BLOCK:])

    b_stages = [(b_scores, b_finish, (step, tuple(range(p0, p0 + B_PAIRS_PER_ITEM))))
                for step in range(ROWS_PER_CHUNK // 2)
                for p0 in range(0, B_HEADS // 2, B_PAIRS_PER_ITEM)]
    first_reader = next(t for t, (_, _, (g, j)) in enumerate(a_stages) if j == BLOCKS_PER_CHUNK - 1)
    assert sum(FILL_PLAN[:first_reader - 1]) >= 1 + n_kv_b + 1
    assert len(a_stages) + len(b_stages) == len(FILL_PLAN) and sum(FILL_PLAN) == len(fills)
    pipelined(a_stages + b_stages, FILL_PLAN)


    def gated_branch(o_ref, z_col, w_ref, g_col):
        z = proj_ref[:, z_col:z_col + A_WIDTH]
        a = (o_ref[...] * (z * jax.nn.sigmoid(z))).astype(bf16)
        y = jnp.dot(a, w_ref[...], preferred_element_type=f32)
        return jax.nn.sigmoid(proj_ref[:, g_col:g_col + D_MODEL]) * y

    merged = gated_branch(oa_ref, P_ZA, wpa_ref, P_GA) + gated_branch(ob_ref, P_ZB, wpb_ref, P_GB)
    h = x + jnp.dot(merged.astype(bf16), wout_ref[...], preferred_element_type=f32)
    out_ref[0] = _rmsnorm(h, fgain_ref[...])


def _layer_call(sink, x, meta, rope_base, rope_off, gain, fgain, w, w_pa, w_pb, w_out,
                amasks, btables):
    bsz, seq, _ = x.shape
    n_chunks = seq // CHUNK
    n_steps = bsz * n_chunks
    const2 = lambda t: (0, 0)
    const3 = lambda t: (0, 0, 0)
    resident = dict(pipeline_mode=pl.Buffered(1))
    body = functools.partial(_layer_body, n_chunks=n_chunks, n_rows=seq // GRID_W)

    def chunk_of(t):
        return t // n_chunks, t % n_chunks

    def next_chunk(t):
        return chunk_of(jnp.minimum(t + 1, n_steps - 1))

    return pl.pallas_call(
        body,
        grid=(n_steps,),
        in_specs=[
            pl.BlockSpec(memory_space=pltpu.SMEM),
            pl.BlockSpec((1, CHUNK, D_MODEL), lambda t: (*chunk_of(t), 0)),
            pl.BlockSpec((1, CHUNK, D_MODEL), lambda t: (*next_chunk(t), 0)),
            pl.BlockSpec((N_META, D_MODEL), const2, **resident),
            pl.BlockSpec((CHUNK, LANES), const2, **resident),
            pl.BlockSpec((CHUNK, LANES), const2, **resident),
            pl.BlockSpec((None, 2, LANES), lambda t: (chunk_of(t)[1], 0, 0)),
            pl.BlockSpec((None, 2, LANES), lambda t: (next_chunk(t)[1], 0, 0)),
            pl.BlockSpec((1, D_MODEL), const2, **resident),
            pl.BlockSpec((1, D_MODEL), const2, **resident),
            pl.BlockSpec((D_MODEL, IN_COLS), const2, **resident),
            pl.BlockSpec((A_WIDTH, D_MODEL), const2, **resident),
            pl.BlockSpec((B_WIDTH, D_MODEL), const2, **resident),
            pl.BlockSpec((D_MODEL, D_MODEL), const2, **resident),
            pl.BlockSpec((3, META_PAD + 3 * BLOCK, BLOCK), const3, **resident),
            pl.BlockSpec((A_GROUP * BLOCK, BLOCK), const2, **resident),
            pl.BlockSpec((B_HEADS // 2, 2 * NA_KH, 2, GRID_W, LANES), lambda t: (0, 0, 0, 0, 0),
                         **resident),
            pl.BlockSpec((2, B_WIN_KEYS + META_PAD, LANES), const3, **resident),
            pl.BlockSpec((2 * BLOCK, LANES), const2, **resident),
        ],
        out_specs=pl.BlockSpec((1, CHUNK, D_MODEL), lambda t: (*chunk_of(t), 0)),
        out_shape=jax.ShapeDtypeStruct((bsz, seq, D_MODEL), jnp.float32),
        scratch_shapes=[
            pltpu.VMEM((2, CHUNK, D_MODEL), jnp.bfloat16),
            pltpu.VMEM((3, CHUNK, KV_COLS), jnp.bfloat16),
            pltpu.VMEM((N_META, KV_COLS), jnp.bfloat16),
            pltpu.VMEM((CHUNK + 2 * HALO, KV_COLS), jnp.bfloat16),
            pltpu.VMEM((CHUNK, A_WIDTH), jnp.bfloat16),
            pltpu.VMEM((CHUNK, B_WIDTH), jnp.bfloat16),
            pltpu.VMEM((CHUNK, A_WIDTH), jnp.float32),
            pltpu.VMEM((CHUNK, B_WIDTH), jnp.float32),
            pltpu.VMEM((CHUNK, P_COLS), jnp.float32),
        ],
        compiler_params=pltpu.CompilerParams(
            dimension_semantics=("arbitrary",), vmem_limit_bytes=VMEM_LIMIT_BYTES),
        name="hybrid_layer",
    )(sink, x, x, meta, *rope_base, rope_off, rope_off, gain, fgain, w, w_pa, w_pb, w_out,
      *amasks, *btables)


def _rope_tables(n_chunks):
    d = np.arange(LANES) % HEAD_DIM
    inv_freq = ROPE_THETA ** (-jnp.asarray(d % ROT_HALF, jnp.float32) / ROT_HALF)
    inv_lane = jnp.where(d < ROT_DIM, inv_freq, 0.0)[None, :]
    base = jnp.arange(CHUNK, dtype=jnp.float32)[:, None] * inv_lane
    off = (N_META + CHUNK * jnp.arange(n_chunks, dtype=jnp.float32))[:, None] * inv_lane
    return (jnp.cos(base), jnp.sin(base)), jnp.stack([jnp.cos(off), jnp.sin(off)], axis=1)


def _band_masks():
    qi = np.arange(BLOCK)[:, None]
    col = np.arange(META_PAD + 3 * BLOCK)[None, :]
    kj = col - META_PAD
    band = (col >= META_PAD) & (np.abs(kj - BLOCK - qi) <= WINDOW)
    meta = np.broadcast_to(col < N_META, band.shape)
    variants = [meta | band, meta | (band & (kj >= BLOCK)), meta | (band & (kj < 2 * BLOCK))]
    masks = np.where(np.stack(variants), 0.0, NEG_INF).astype(np.float32).transpose(0, 2, 1)
    eye = np.tile(np.eye(BLOCK, dtype=np.float32), (A_GROUP, 1))
    return jnp.asarray(masks, jnp.bfloat16), jnp.asarray(eye, jnp.bfloat16)


def _na_bias(rpb):
    cq = np.arange(GRID_W)[:, None]
    kc = np.arange(GRID_W)[None, :]
    start = np.clip(cq - NA_KW // 2, 0, GRID_W - NA_KW)
    valid = (kc >= start) & (kc < start + NA_KW)
    n_pairs, n_dr, n_dc = B_HEADS // 2, 2 * NA_KH - 1, 2 * NA_KW - 1
    onehot = ((kc - cq + NA_KW - 1)[..., None] == np.arange(n_dc)) & valid[..., None]
    select = np.zeros((GRID_W, 2, n_dc, 2, GRID_W), np.float32)
    for t in range(2):
        select[:, t, :, t, :] = onehot.transpose(0, 2, 1)
    mask = np.where(np.tile(valid, (1, 2)), 0.0, NEG_INF).astype(np.float32)
    select = np.concatenate([select.reshape(GRID_W, 2 * n_dc, LANES), mask[:, None, :]], axis=1)
    rows = jnp.pad(rpb.reshape(n_pairs, 2, n_dr, n_dc), ((0, 0), (0, 0), (0, 2), (0, 0)))
    rows = jnp.stack([rows[:, :, :2 * NA_KH], rows[:, :, 1:]], axis=3)
    rows = jnp.concatenate([rows.reshape(n_pairs, 2, 2 * NA_KH, 2 * n_dc),
                            jnp.ones((n_pairs, 2, 2 * NA_KH, 1), jnp.float32)], axis=-1)
    pair = jnp.einsum('prhe,cen->prhcn', jnp.transpose(rows, (0, 2, 1, 3)), jnp.asarray(select),
                      precision=lax.Precision.HIGHEST)

    key_row = np.arange(B_WIN_KEYS + META_PAD) // GRID_W
    is_meta = (np.arange(B_WIN_KEYS + META_PAD) >= B_WIN_KEYS) & (
        np.arange(B_WIN_KEYS + META_PAD) < B_WIN_KEYS + N_META)
    rows = np.zeros((2, B_WIN_KEYS + META_PAD, LANES), np.float32)
    for variant, starts in enumerate(((0, 1), (0, 0))):
        for u, s0 in enumerate(starts):
            seen = ((key_row >= s0) & (key_row < s0 + NA_KH)) | is_meta
            rows[variant, :, u] = np.where(seen, 0.0, NEG_INF)
    sel = np.zeros((2, 2, GRID_W, LANES), np.float32)
    sel[:, 0, :, 0] = 1.0
    sel[:, 1, :, 1] = 1.0
    return (pair, jnp.asarray(rows, jnp.bfloat16),
            jnp.asarray(sel.reshape(2 * BLOCK, LANES), jnp.bfloat16))


def kernel(x, meta_tokens, norm_gain, w_in, sink_logits, rel_pos_bias, w_proj_a, w_proj_b, w_out,
           final_norm_gain):
    bsz, seq, _ = x.shape
    assert seq % CHUNK == 0 and seq // GRID_W >= NA_KH and norm_gain.shape[0] == 1
    bf16 = jnp.bfloat16
    w = w_in[0].astype(bf16)

    rope_base, rope_off = _rope_tables(seq // CHUNK)
    gain = norm_gain[0][None]
    return _layer_call(sink_logits[0], x, meta_tokens, rope_base, rope_off, gain,
                       final_norm_gain[None], w, w_proj_a[0].astype(bf16),
                       w_proj_b[0].astype(bf16), w_out[0].astype(bf16), _band_masks(),
                       _na_bias(rel_pos_bias[0]))
```

```python
import functools

import numpy as np
import jax
import jax.numpy as jnp
from jax import lax
from jax.experimental import pallas as pl
from jax.experimental.pallas import tpu as pltpu

D_MODEL = 1024
N_META = 16
HEAD_DIM = 64
RMS_EPS = 1e-6
NEG_INF = -1e30
A_HEADS = 8
A_KV_HEADS = 2
A_GROUP = A_HEADS // A_KV_HEADS
A_WIDTH = A_HEADS * HEAD_DIM
A_KV_WIDTH = A_KV_HEADS * HEAD_DIM
WINDOW = 128
BLOCK = 128
ROT_DIM = HEAD_DIM // 4
ROT_HALF = ROT_DIM // 2
ROPE_THETA = 500000.0
B_HEADS = 8
B_WIDTH = B_HEADS * HEAD_DIM
GRID_W = 64
NA_KH = 8
NA_KW = 16
SPLIT_SIZES = (A_WIDTH, A_KV_WIDTH, A_KV_WIDTH, A_WIDTH, B_WIDTH, B_WIDTH, B_WIDTH, B_WIDTH,
               D_MODEL, D_MODEL)

LANES = 128
CHUNK = 512
ROWS_PER_CHUNK = CHUNK // GRID_W
BLOCKS_PER_CHUNK = CHUNK // BLOCK
HALO = CHUNK // 2
assert HALO >= BLOCK and HALO >= (NA_KH // 2) * GRID_W
META_PAD = LANES
B_WIN_ROWS = NA_KH + 2
B_WIN_KEYS = B_WIN_ROWS * GRID_W
B_PAIRS_PER_ITEM = 2
PROJ_PIECE = 256
FILL_PLAN = (2, 1, 1, 1, 2, 1, 1, 1, 1, 1, 1, 1, 1, 1, 1, 1)

KV_KA = 0
KV_VA = KV_KA + A_KV_HEADS * LANES
KV_KB = KV_VA + A_KV_HEADS * LANES
KV_VB = KV_KB + B_WIDTH
KV_COLS = KV_VB + B_WIDTH

(C_QA, C_KA, C_VA, C_ZA, C_QB, C_KB, C_VB, C_ZB, C_GA, C_GB, IN_COLS) = (
    int(c) for c in np.cumsum((0,) + SPLIT_SIZES))
assert A_KV_WIDTH == LANES and C_VA == C_KA + A_KV_WIDTH and C_VB == C_KB + B_WIDTH

P_ZA = 0
P_ZB = P_ZA + A_WIDTH
P_GA = P_ZB + B_WIDTH
P_GB = P_GA + D_MODEL
P_COLS = P_GB + D_MODEL
PROJ_SOURCES = tuple(
    src + off for src, width in ((C_ZA, A_WIDTH), (C_ZB, B_WIDTH), (C_GA, D_MODEL), (C_GB, D_MODEL))
    for off in range(0, width, PROJ_PIECE))

VMEM_LIMIT_BYTES = 58 * 1024 * 1024

_NT = (((1,), (1,)), ((), ()))


def _rmsnorm(x, gain):
    return x * lax.rsqrt(jnp.mean(x * x, axis=-1, keepdims=True) + RMS_EPS) * gain


def _lane_patterns():
    lane = lax.broadcasted_iota(jnp.int32, (1, LANES), 1)
    return lane < HEAD_DIM, (lane & (HEAD_DIM - 1)) < ROT_HALF


def _rope_coeffs(cb_ref, sb_ref, off_ref, first_half):
    cb, sb = cb_ref[...], sb_ref[...]
    ca, sa = off_ref[0:1, :], off_ref[1:2, :]
    c = cb * ca - sb * sa
    s = (sb * ca + cb * sa) * jnp.where(first_half, -1.0, 1.0)
    return c, s


def _rope_tile(t, c, s, first_half):
    partner = jnp.where(first_half, pltpu.roll(t, LANES - ROT_HALF, 1), pltpu.roll(t, ROT_HALF, 1))
    return t * c + partner * s


def _layer_body(sink_ref, x_ref, xn_ref, meta_ref, cb_ref, sb_ref, off_ref, offn_ref,
                gain_ref, fgain_ref, wr_ref, wpa_ref, wpb_ref, wout_ref, amask_ref, eye_ref, bpair_ref,
                brow_ref, bsel_ref,
                out_ref, n_ref, ring_ref, kvm_ref, win_ref, qa_ref, qb_ref, oa_ref, ob_ref, proj_ref,
                *, n_chunks, n_rows):
    step = pl.program_id(0)
    i = step % n_chunks
    cur, nxt = step % 2, (step + 1) % 2
    slot_prev, slot_cur, slot_nxt = (step + 2) % 3, step % 3, (step + 1) % 3
    f32, bf16 = jnp.float32, jnp.bfloat16
    scale = HEAD_DIM ** -0.5
    lo_half, first_half = _lane_patterns()

    def kv_a_piece(n, c, s, store):
        kv_a = jnp.dot(n, wr_ref[:, C_KA:C_KA + 2 * A_KV_WIDTH], preferred_element_type=f32)
        ka = _rope_tile(kv_a[:, :LANES], c, s, first_half)
        va = kv_a[:, LANES:]
        for src, col in ((ka, KV_KA), (va, KV_VA)):
            swapped = pltpu.roll(src, HEAD_DIM, 1)
            store(col, jnp.where(lo_half, src, swapped).astype(bf16))
            store(col + LANES, jnp.where(lo_half, swapped, src).astype(bf16))

    def kv_b_piece(n, k, store):
        src = C_KB + k * PROJ_PIECE
        store(KV_KB + k * PROJ_PIECE,
              jnp.dot(n, wr_ref[:, src:src + PROJ_PIECE], preferred_element_type=f32).astype(bf16))

    n_kv_b = 2 * B_WIDTH // PROJ_PIECE

    def ring_store(slot):
        def store(col, value):
            ring_ref[slot, :, col:col + value.shape[1]] = value
        return store

    def meta_store(col, value):
        kvm_ref[:, col:col + value.shape[1]] = value

    def project_qa(slot, chunk_off_ref):
        c, s = _rope_coeffs(cb_ref, sb_ref, chunk_off_ref, first_half)
        qa = jnp.dot(n_ref[slot], wr_ref[:, C_QA:C_QA + A_WIDTH], preferred_element_type=f32)
        for t in range(A_WIDTH // LANES):
            sl = slice(t * LANES, (t + 1) * LANES)
            qa_ref[:, sl] = (_rope_tile(qa[:, sl], c, s, first_half) * scale).astype(bf16)

    def project_qb(slot):
        qb = jnp.dot(n_ref[slot], wr_ref[:, C_QB:C_QB + B_WIDTH], preferred_element_type=f32)
        qb_ref[...] = (qb * scale).astype(bf16)

    @pl.when(step == 0)
    def _():
        ring_ref[...] = jnp.zeros(ring_ref.shape, bf16)
        n_ref[0] = _rmsnorm(x_ref[0], gain_ref[...]).astype(bf16)
        project_qa(0, off_ref)
        project_qb(0)
        c0, s0 = _rope_coeffs(cb_ref, sb_ref, off_ref, first_half)
        kv_a_piece(n_ref[0], c0, s0, ring_store(0))
        for k in range(n_kv_b):
            kv_b_piece(n_ref[0], k, ring_store(0))
        n_meta = _rmsnorm(meta_ref[...], gain_ref[...]).astype(bf16)
        sign = jnp.where(first_half, -1.0, 1.0)
        kv_a_piece(n_meta, cb_ref[0:N_META], sb_ref[0:N_META] * sign, meta_store)
        for k in range(n_kv_b):
            kv_b_piece(n_meta, k, meta_store)

    x = x_ref[0]
    n_ref[nxt] = _rmsnorm(xn_ref[0], gain_ref[...]).astype(bf16)

    win_ref[0:HALO] = ring_ref[slot_prev, CHUNK - HALO:CHUNK]
    win_ref[HALO:HALO + CHUNK] = ring_ref[slot_cur]

    def next_halo():
        win_ref[HALO + CHUNK:CHUNK + 2 * HALO] = ring_ref[slot_nxt, 0:HALO]


    zero_pad = jnp.zeros((META_PAD - N_META, LANES), bf16)

    def meta_tile(col):
        return jnp.concatenate([kvm_ref[:, col:col + LANES], zero_pad], axis=0)

    def split_heads(q):
        zeros = jnp.zeros_like(q)
        return jnp.concatenate([jnp.where(lo_half, q, zeros), jnp.where(lo_half, zeros, q)], axis=0)

    def with_ones(v):
        return jnp.concatenate([v, jnp.ones_like(v)], axis=1)

    def proj_piece(k):
        src = PROJ_SOURCES[k]
        proj_ref[:, k * PROJ_PIECE:(k + 1) * PROJ_PIECE] = jnp.dot(
            n_ref[cur], wr_ref[:, src:src + PROJ_PIECE], preferred_element_type=f32)

    def next_kv_a():
        cn, sn = _rope_coeffs(cb_ref, sb_ref, offn_ref, first_half)
        kv_a_piece(n_ref[nxt], cn, sn, ring_store(slot_nxt))

    def next_kv_b(k):
        kv_b_piece(n_ref[nxt], k, ring_store(slot_nxt))

    fills = ([next_kv_a] + [functools.partial(next_kv_b, k) for k in range(n_kv_b)] + [next_halo]
             + [functools.partial(proj_piece, k) for k in range(len(PROJ_SOURCES))])

    def pipelined(stages, fill_plan):
        pending = stages[0][0](stages[0][2])
        for t, (_, finish, item) in enumerate(stages):
            nxt_scores = stages[t + 1][0](stages[t + 1][2]) if t + 1 < len(stages) else None
            for _ in range(fill_plan[t]):
                fills.pop(0)()
            finish(item, pending)
            pending = nxt_scores

    a_tiles = A_GROUP // 2
    kmetas_a = [meta_tile(KV_KA + g * LANES) for g in range(A_KV_HEADS)]
    vmetas_a = [meta_tile(KV_VA + g * LANES) for g in range(A_KV_HEADS)]

    def a_window(col, tile, j):
        w0 = HALO + j * BLOCK - BLOCK
        return win_ref[w0:w0 + 3 * BLOCK, col + tile * LANES:col + (tile + 1) * LANES]

    def a_scores(item):
        g, j = item
        first = jnp.logical_and(i == 0, j == 0)
        last = jnp.logical_and(i == n_chunks - 1, j == BLOCKS_PER_CHUNK - 1)
        mask_t = amask_ref[jnp.where(first, 1, jnp.where(last, 2, 0))]
        kcat = jnp.concatenate([kmetas_a[g], a_window(KV_KA, g, j)], axis=0)
        k_aug = jnp.concatenate([kcat, mask_t], axis=1)
        q_tiles = [qa_ref[j * BLOCK:(j + 1) * BLOCK, (a_tiles * g + t) * LANES:(a_tiles * g + t + 1) * LANES]
                   for t in range(a_tiles)]
        zeros = jnp.zeros_like(q_tiles[0])
        qs = jnp.concatenate([jnp.where(lo_half, q, zeros) for q in q_tiles]
                             + [jnp.where(lo_half, zeros, q) for q in q_tiles], axis=0)
        q_aug = jnp.concatenate([qs, eye_ref[...]], axis=1)
        return lax.dot_general(q_aug, k_aug, _NT, preferred_element_type=f32)

    def a_finish(item, s):
        g, j = item
        ps, es = [], []
        for rb in range(A_GROUP):
            sink = sink_ref[A_GROUP * g + 2 * (rb % a_tiles) + rb // a_tiles]
            sh = s[rb * BLOCK:(rb + 1) * BLOCK]
            m = jnp.maximum(jnp.max(sh, axis=-1, keepdims=True), sink)
            ps.append(jnp.exp(sh - m).astype(bf16))
            es.append(jnp.exp(sink - m))
        vcat = jnp.concatenate([vmetas_a[g], a_window(KV_VA, g, j)], axis=0)
        o = jnp.dot(jnp.concatenate(ps, axis=0), with_ones(vcat), preferred_element_type=f32)
        for t in range(a_tiles):
            oe = o[t * BLOCK:(t + 1) * BLOCK]
            oo = o[(a_tiles + t) * BLOCK:(a_tiles + t + 1) * BLOCK]
            den = (jnp.where(lo_half, oe[:, LANES:], oo[:, LANES:])
                   + jnp.where(lo_half, es[t], es[a_tiles + t]))
            oa_ref[j * BLOCK:(j + 1) * BLOCK, (a_tiles * g + t) * LANES:(a_tiles * g + t + 1) * LANES] = (
                jnp.where(lo_half, oe[:, :LANES], oo[:, :LANES]) / den)

    a_stages = [(a_scores, a_finish, (g, j)) for last in (False, True) for g in range(A_KV_HEADS)
                for j in range(BLOCKS_PER_CHUNK) if (j == BLOCKS_PER_CHUNK - 1) == last]

    kmetas = [meta_tile(KV_KB + p * LANES) for p in range(B_HEADS // 2)]
    vmetas = [meta_tile(KV_VB + p * LANES) for p in range(B_HEADS // 2)]

    def unit_geometry(step):
        r0 = i * ROWS_PER_CHUNK + 2 * step
        rs0 = jnp.clip(r0 - NA_KH // 2, 0, n_rows - NA_KH)
        rs1 = jnp.clip(r0 + 1 - NA_KH // 2, 0, n_rows - NA_KH)
        shifts = (r0 - rs0, r0 + 1 - rs0)
        variant = 1 - (rs1 - rs0)
        woff = pl.multiple_of((rs0 - i * ROWS_PER_CHUNK) * GRID_W + HALO, BLOCK)
        return shifts, variant, woff

    geometry = [unit_geometry(step) for step in range(ROWS_PER_CHUNK // 2)]

    def b_scores(item):
        step, pairs = item
        _, variant, woff = geometry[step]
        out = []
        for p in pairs:
            kcat = jnp.concatenate(
                [win_ref[pl.ds(woff, B_WIN_KEYS), KV_KB + p * LANES:KV_KB + (p + 1) * LANES],
                 kmetas[p]], axis=0)
            k_aug = jnp.concatenate([kcat, brow_ref[variant]], axis=1)
            qs = split_heads(qb_ref[step * BLOCK:(step + 1) * BLOCK, p * LANES:(p + 1) * LANES])
            q_aug = jnp.concatenate([qs, bsel_ref[...]], axis=1)
            out.append(lax.dot_general(q_aug, k_aug, _NT, preferred_element_type=f32))
        return out

    def b_finish(item, scores):
        step, pairs = item
        shifts, _, woff = geometry[step]
        probs = []
        for p, s in zip(pairs, scores):
            bias = jnp.concatenate(
                [jnp.concatenate([bpair_ref[p, 2 * jj + NA_KH - 1 - shifts[u], hh]
                                  for jj in range(B_WIN_ROWS // 2)], axis=1)
                 for hh in range(2) for u in range(2)], axis=0)
            s_loc = s[:, :B_WIN_KEYS] + bias
            s_met = s[:, B_WIN_KEYS:]
            m = jnp.maximum(jnp.max(s_loc, axis=-1, keepdims=True),
                            jnp.max(s_met, axis=-1, keepdims=True))
            probs.append(jnp.concatenate([jnp.exp(s_loc - m), jnp.exp(s_met - m)],
                                         axis=1).astype(bf16))
        for p, pr in zip(pairs, probs):
            vcat = jnp.concatenate(
                [win_ref[pl.ds(woff, B_WIN_KEYS), KV_VB + p * LANES:KV_VB + (p + 1) * LANES],
                 vmetas[p]], axis=0)
            o = jnp.dot(pr, with_ones(vcat), preferred_element_type=f32)
            o = o[:, :LANES] / o[:, LANES:]
            ob_ref[step * BLOCK:(step + 1) * BLOCK, p * LANES:(p + 1) * LANES] = jnp.where(
                lo_half, o[:BLOCK], o[BLOCK:])

    b_stages = [(b_scores, b_finish, (step, tuple(range(p0, p0 + B_PAIRS_PER_ITEM))))
                for step in range(ROWS_PER_CHUNK // 2)
                for p0 in range(0, B_HEADS // 2, B_PAIRS_PER_ITEM)]
    first_reader = next(t for t, (_, _, (g, j)) in enumerate(a_stages) if j == BLOCKS_PER_CHUNK - 1)
    assert sum(FILL_PLAN[:first_reader - 1]) >= 1 + n_kv_b + 1
    assert len(a_stages) + len(b_stages) == len(FILL_PLAN) and sum(FILL_PLAN) == len(fills)
    pipelined(a_stages + b_stages, FILL_PLAN)


    def gated_branch(o_ref, z_col, w_ref, g_col):
        z = proj_ref[:, z_col:z_col + A_WIDTH]
        a = (o_ref[...] * (z * jax.nn.sigmoid(z))).astype(bf16)
        y = jnp.dot(a, w_ref[...], preferred_element_type=f32)
        return jax.nn.sigmoid(proj_ref[:, g_col:g_col + D_MODEL]) * y

    merged = gated_branch(oa_ref, P_ZA, wpa_ref, P_GA) + gated_branch(ob_ref, P_ZB, wpb_ref, P_GB)
    project_qa(nxt, offn_ref)
    h = x + jnp.dot(merged.astype(bf16), wout_ref[...], preferred_element_type=f32)
    project_qb(nxt)
    out_ref[0] = _rmsnorm(h, fgain_ref[...])


def _layer_call(sink, x, meta, rope_base, rope_off, gain, fgain, w, w_pa, w_pb, w_out,
                amasks, btables):
    bsz, seq, _ = x.shape
    n_chunks = seq // CHUNK
    n_steps = bsz * n_chunks
    const2 = lambda t: (0, 0)
    const3 = lambda t: (0, 0, 0)
    resident = dict(pipeline_mode=pl.Buffered(1))
    body = functools.partial(_layer_body, n_chunks=n_chunks, n_rows=seq // GRID_W)

    def chunk_of(t):
        return t // n_chunks, t % n_chunks

    def next_chunk(t):
        return chunk_of(jnp.minimum(t + 1, n_steps - 1))

    return pl.pallas_call(
        body,
        grid=(n_steps,),
        in_specs=[
            pl.BlockSpec(memory_space=pltpu.SMEM),
            pl.BlockSpec((1, CHUNK, D_MODEL), lambda t: (*chunk_of(t), 0)),
            pl.BlockSpec((1, CHUNK, D_MODEL), lambda t: (*next_chunk(t), 0)),
            pl.BlockSpec((N_META, D_MODEL), const2, **resident),
            pl.BlockSpec((CHUNK, LANES), const2, **resident),
            pl.BlockSpec((CHUNK, LANES), const2, **resident),
            pl.BlockSpec((None, 2, LANES), lambda t: (chunk_of(t)[1], 0, 0)),
            pl.BlockSpec((None, 2, LANES), lambda t: (next_chunk(t)[1], 0, 0)),
            pl.BlockSpec((1, D_MODEL), const2, **resident),
            pl.BlockSpec((1, D_MODEL), const2, **resident),
            pl.BlockSpec((D_MODEL, IN_COLS), const2, **resident),
            pl.BlockSpec((A_WIDTH, D_MODEL), const2, **resident),
            pl.BlockSpec((B_WIDTH, D_MODEL), const2, **resident),
            pl.BlockSpec((D_MODEL, D_MODEL), const2, **resident),
            pl.BlockSpec((3, META_PAD + 3 * BLOCK, BLOCK), const3, **resident),
            pl.BlockSpec((A_GROUP * BLOCK, BLOCK), const2, **resident),
            pl.BlockSpec((B_HEADS // 2, 2 * NA_KH, 2, GRID_W, LANES), lambda t: (0, 0, 0, 0, 0),
                         **resident),
            pl.BlockSpec((2, B_WIN_KEYS + META_PAD, LANES), const3, **resident),
            pl.BlockSpec((2 * BLOCK, LANES), const2, **resident),
        ],
        out_specs=pl.BlockSpec((1, CHUNK, D_MODEL), lambda t: (*chunk_of(t), 0)),
        out_shape=jax.ShapeDtypeStruct((bsz, seq, D_MODEL), jnp.float32),
        scratch_shapes=[
            pltpu.VMEM((2, CHUNK, D_MODEL), jnp.bfloat16),
            pltpu.VMEM((3, CHUNK, KV_COLS), jnp.bfloat16),
            pltpu.VMEM((N_META, KV_COLS), jnp.bfloat16),
            pltpu.VMEM((CHUNK + 2 * HALO, KV_COLS), jnp.bfloat16),
            pltpu.VMEM((CHUNK, A_WIDTH), jnp.bfloat16),
            pltpu.VMEM((CHUNK, B_WIDTH), jnp.bfloat16),
            pltpu.VMEM((CHUNK, A_WIDTH), jnp.float32),
            pltpu.VMEM((CHUNK, B_WIDTH), jnp.float32),
            pltpu.VMEM((CHUNK, P_COLS), jnp.float32),
        ],
        compiler_params=pltpu.CompilerParams(
            dimension_semantics=("arbitrary",), vmem_limit_bytes=VMEM_LIMIT_BYTES),
        name="hybrid_layer",
    )(sink, x, x, meta, *rope_base, rope_off, rope_off, gain, fgain, w, w_pa, w_pb, w_out,
      *amasks, *btables)


def _rope_tables(n_chunks):
    d = np.arange(LANES) % HEAD_DIM
    inv_freq = ROPE_THETA ** (-jnp.asarray(d % ROT_HALF, jnp.float32) / ROT_HALF)
    inv_lane = jnp.where(d < ROT_DIM, inv_freq, 0.0)[None, :]
    base = jnp.arange(CHUNK, dtype=jnp.float32)[:, None] * inv_lane
    off = (N_META + CHUNK * jnp.arange(n_chunks, dtype=jnp.float32))[:, None] * inv_lane
    return (jnp.cos(base), jnp.sin(base)), jnp.stack([jnp.cos(off), jnp.sin(off)], axis=1)


def _band_masks():
    qi = np.arange(BLOCK)[:, None]
    col = np.arange(META_PAD + 3 * BLOCK)[None, :]
    kj = col - META_PAD
    band = (col >= META_PAD) & (np.abs(kj - BLOCK - qi) <= WINDOW)
    meta = np.broadcast_to(col < N_META, band.shape)
    variants = [meta | band, meta | (band & (kj >= BLOCK)), meta | (band & (kj < 2 * BLOCK))]
    masks = np.where(np.stack(variants), 0.0, NEG_INF).astype(np.float32).transpose(0, 2, 1)
    eye = np.tile(np.eye(BLOCK, dtype=np.float32), (A_GROUP, 1))
    return jnp.asarray(masks, jnp.bfloat16), jnp.asarray(eye, jnp.bfloat16)


def _na_bias(rpb):
    cq = np.arange(GRID_W)[:, None]
    kc = np.arange(GRID_W)[None, :]
    start = np.clip(cq - NA_KW // 2, 0, GRID_W - NA_KW)
    valid = (kc >= start) & (kc < start + NA_KW)
    n_pairs, n_dr, n_dc = B_HEADS // 2, 2 * NA_KH - 1, 2 * NA_KW - 1
    onehot = ((kc - cq + NA_KW - 1)[..., None] == np.arange(n_dc)) & valid[..., None]
    select = np.zeros((GRID_W, 2, n_dc, 2, GRID_W), np.float32)
    for t in range(2):
        select[:, t, :, t, :] = onehot.transpose(0, 2, 1)
    mask = np.where(np.tile(valid, (1, 2)), 0.0, NEG_INF).astype(np.float32)
    select = np.concatenate([select.reshape(GRID_W, 2 * n_dc, LANES), mask[:, None, :]], axis=1)
    rows = jnp.pad(rpb.reshape(n_pairs, 2, n_dr, n_dc), ((0, 0), (0, 0), (0, 2), (0, 0)))
    rows = jnp.stack([rows[:, :, :2 * NA_KH], rows[:, :, 1:]], axis=3)
    rows = jnp.concatenate([rows.reshape(n_pairs, 2, 2 * NA_KH, 2 * n_dc),
                            jnp.ones((n_pairs, 2, 2 * NA_KH, 1), jnp.float32)], axis=-1)
    pair = jnp.einsum('prhe,cen->prhcn', jnp.transpose(rows, (0, 2, 1, 3)), jnp.asarray(select),
                      precision=lax.Precision.HIGHEST)

    key_row = np.arange(B_WIN_KEYS + META_PAD) // GRID_W
    is_meta = (np.arange(B_WIN_KEYS + META_PAD) >= B_WIN_KEYS) & (
        np.arange(B_WIN_KEYS + META_PAD) < B_WIN_KEYS + N_META)
    rows = np.zeros((2, B_WIN_KEYS + META_PAD, LANES), np.float32)
    for variant, starts in enumerate(((0, 1), (0, 0))):
        for u, s0 in enumerate(starts):
            seen = ((key_row >= s0) & (key_row < s0 + NA_KH)) | is_meta
            rows[variant, :, u] = np.where(seen, 0.0, NEG_INF)
    sel = np.zeros((2, 2, GRID_W, LANES), np.float32)
    sel[:, 0, :, 0] = 1.0
    sel[:, 1, :, 1] = 1.0
    return (pair, jnp.asarray(rows, jnp.bfloat16),
            jnp.asarray(sel.reshape(2 * BLOCK, LANES), jnp.bfloat16))


def kernel(x, meta_tokens, norm_gain, w_in, sink_logits, rel_pos_bias, w_proj_a, w_proj_b, w_out,
           final_norm_gain):
    bsz, seq, _ = x.shape
    assert seq % CHUNK == 0 and seq // GRID_W >= NA_KH and norm_gain.shape[0] == 1
    bf16 = jnp.bfloat16
    w = w_in[0].astype(bf16)

    rope_base, rope_off = _rope_tables(seq // CHUNK)
    gain = norm_gain[0][None]
    return _layer_call(sink_logits[0], x, meta_tokens, rope_base, rope_off, gain,
                       final_norm_gain[None], w, w_proj_a[0].astype(bf16),
                       w_proj_b[0].astype(bf16), w_out[0].astype(bf16), _band_masks(),
                       _na_bias(rel_pos_bias[0]))
```

```python
import functools

import numpy as np
import jax
import jax.numpy as jnp
from jax import lax
from jax.experimental import pallas as pl
from jax.experimental.pallas import tpu as pltpu

D_MODEL = 1024
N_META = 16
HEAD_DIM = 64
RMS_EPS = 1e-6
NEG_INF = -1e30
A_HEADS = 8
A_KV_HEADS = 2
A_GROUP = A_HEADS // A_KV_HEADS
A_WIDTH = A_HEADS * HEAD_DIM
A_KV_WIDTH = A_KV_HEADS * HEAD_DIM
WINDOW = 128
BLOCK = 128
ROT_DIM = HEAD_DIM // 4
ROT_HALF = ROT_DIM // 2
ROPE_THETA = 500000.0
B_HEADS = 8
B_WIDTH = B_HEADS * HEAD_DIM
GRID_W = 64
NA_KH = 8
NA_KW = 16
SPLIT_SIZES = (A_WIDTH, A_KV_WIDTH, A_KV_WIDTH, A_WIDTH, B_WIDTH, B_WIDTH, B_WIDTH, B_WIDTH,
               D_MODEL, D_MODEL)

LANES = 128
CHUNK = 512
ROWS_PER_CHUNK = CHUNK // GRID_W
BLOCKS_PER_CHUNK = CHUNK // BLOCK
HALO = CHUNK // 2
assert HALO >= BLOCK and HALO >= (NA_KH // 2) * GRID_W
META_PAD = LANES
B_WIN_ROWS = NA_KH + 2
B_WIN_KEYS = B_WIN_ROWS * GRID_W
B_PAIRS_PER_ITEM = 2
PROJ_PIECE = 256
FILL_PLAN = (2, 2, 1, 1, 2, 1, 1, 1, 1, 1, 1, 1, 1, 1, 1, 1)

KV_KA = 0
KV_VA = KV_KA + A_KV_HEADS * LANES
KV_KB = KV_VA + A_KV_HEADS * LANES
KV_VB = KV_KB + B_WIDTH
KV_COLS = KV_VB + B_WIDTH

(C_QA, C_KA, C_VA, C_ZA, C_QB, C_KB, C_VB, C_ZB, C_GA, C_GB, IN_COLS) = (
    int(c) for c in np.cumsum((0,) + SPLIT_SIZES))
assert A_KV_WIDTH == LANES and C_VA == C_KA + A_KV_WIDTH and C_VB == C_KB + B_WIDTH

P_ZA = 0
P_ZB = P_ZA + A_WIDTH
P_GA = P_ZB + B_WIDTH
P_GB = P_GA + D_MODEL
P_COLS = P_GB + D_MODEL
PROJ_SOURCES = tuple(
    src + off for src, width in ((C_ZA, A_WIDTH), (C_ZB, B_WIDTH), (C_GA, D_MODEL), (C_GB, D_MODEL))
    for off in range(0, width, PROJ_PIECE))

VMEM_LIMIT_BYTES = 58 * 1024 * 1024

_NT = (((1,), (1,)), ((), ()))


def _rmsnorm(x, gain):
    return x * lax.rsqrt(jnp.mean(x * x, axis=-1, keepdims=True) + RMS_EPS) * gain


def _lane_patterns():
    lane = lax.broadcasted_iota(jnp.int32, (1, LANES), 1)
    return lane < HEAD_DIM, (lane & (HEAD_DIM - 1)) < ROT_HALF


def _rope_coeffs(cb_ref, sb_ref, off_ref, first_half):
    cb, sb = cb_ref[...], sb_ref[...]
    ca, sa = off_ref[0:1, :], off_ref[1:2, :]
    c = cb * ca - sb * sa
    s = (sb * ca + cb * sa) * jnp.where(first_half, -1.0, 1.0)
    return c, s


def _rope_tile(t, c, s, first_half):
    partner = jnp.where(first_half, pltpu.roll(t, LANES - ROT_HALF, 1), pltpu.roll(t, ROT_HALF, 1))
    return t * c + partner * s


def _layer_body(sink_ref, x_ref, xn_ref, meta_ref, cb_ref, sb_ref, off_ref, offn_ref,
                gain_ref, fgain_ref, wr_ref, wt_ref, wpa_ref, wpb_ref, wout_ref, amask_ref, eye_ref,
                bpair_ref, brow_ref, bsel_ref,
                out_ref, n_ref, ring_ref, kvm_ref, win_ref, qa_ref, qb_ref, oa_ref, ob_ref, proj_ref,
                vat_ring_ref, vat_meta_ref, vat_win_ref, zat_ref, *, n_chunks, n_rows):
    step = pl.program_id(0)
    i = step % n_chunks
    cur, nxt = step % 2, (step + 1) % 2
    slot_prev, slot_cur, slot_nxt = (step + 2) % 3, step % 3, (step + 1) % 3
    f32, bf16 = jnp.float32, jnp.bfloat16
    scale = HEAD_DIM ** -0.5
    lo_half, first_half = _lane_patterns()

    def kv_a_piece(n, c, s, store):
        kv_a = jnp.dot(n, wr_ref[:, C_KA:C_KA + 2 * A_KV_WIDTH], preferred_element_type=f32)
        ka = _rope_tile(kv_a[:, :LANES], c, s, first_half)
        va = kv_a[:, LANES:]
        for src, col in ((ka, KV_KA), (va, KV_VA)):
            swapped = pltpu.roll(src, HEAD_DIM, 1)
            store(col, jnp.where(lo_half, src, swapped).astype(bf16))
            store(col + LANES, jnp.where(lo_half, swapped, src).astype(bf16))

    def kv_b_piece(n, k, store):
        src = C_KB + k * PROJ_PIECE
        store(KV_KB + k * PROJ_PIECE,
              jnp.dot(n, wr_ref[:, src:src + PROJ_PIECE], preferred_element_type=f32).astype(bf16))

    n_kv_b = 2 * B_WIDTH // PROJ_PIECE

    def va_transposed(n):
        return lax.dot_general(wt_ref[0:A_KV_WIDTH], n, _NT, preferred_element_type=f32).astype(bf16)

    def ring_store(slot):
        def store(col, value):
            ring_ref[slot, :, col:col + value.shape[1]] = value
        return store

    def meta_store(col, value):
        kvm_ref[:, col:col + value.shape[1]] = value

    @pl.when(step == 0)
    def _():
        ring_ref[...] = jnp.zeros(ring_ref.shape, bf16)
        n_ref[0] = _rmsnorm(x_ref[0], gain_ref[...]).astype(bf16)
        c0, s0 = _rope_coeffs(cb_ref, sb_ref, off_ref, first_half)
        kv_a_piece(n_ref[0], c0, s0, ring_store(0))
        for k in range(n_kv_b):
            kv_b_piece(n_ref[0], k, ring_store(0))
        n_meta = _rmsnorm(meta_ref[...], gain_ref[...]).astype(bf16)
        sign = jnp.where(first_half, -1.0, 1.0)
        kv_a_piece(n_meta, cb_ref[0:N_META], sb_ref[0:N_META] * sign, meta_store)
        for k in range(n_kv_b):
            kv_b_piece(n_meta, k, meta_store)
        vat_ring_ref[...] = jnp.zeros(vat_ring_ref.shape, bf16)
        vat_ring_ref[0] = va_transposed(n_ref[0])
        n_meta_pad = jnp.concatenate([n_meta, jnp.zeros((META_PAD - N_META, D_MODEL), bf16)], axis=0)
        vat_meta_ref[...] = va_transposed(n_meta_pad)

    x = x_ref[0]
    n_ref[nxt] = _rmsnorm(xn_ref[0], gain_ref[...]).astype(bf16)

    win_ref[0:HALO] = ring_ref[slot_prev, CHUNK - HALO:CHUNK]
    win_ref[HALO:HALO + CHUNK] = ring_ref[slot_cur]
    vat_win_ref[:, 0:HALO] = vat_ring_ref[slot_prev, :, CHUNK - HALO:CHUNK]
    vat_win_ref[:, HALO:HALO + CHUNK] = vat_ring_ref[slot_cur]

    def next_halo():
        win_ref[HALO + CHUNK:CHUNK + 2 * HALO] = ring_ref[slot_nxt, 0:HALO]
        vat_win_ref[:, HALO + CHUNK:CHUNK + 2 * HALO] = vat_ring_ref[slot_nxt, :, 0:HALO]

    c, s = _rope_coeffs(cb_ref, sb_ref, off_ref, first_half)
    qa = jnp.dot(n_ref[cur], wr_ref[:, C_QA:C_QA + A_WIDTH], preferred_element_type=f32)
    for t in range(A_WIDTH // LANES):
        sl = slice(t * LANES, (t + 1) * LANES)
        qa_ref[:, sl] = (_rope_tile(qa[:, sl], c, s, first_half) * scale).astype(bf16)
    qb = jnp.dot(n_ref[cur], wr_ref[:, C_QB:C_QB + B_WIDTH], preferred_element_type=f32)
    qb_ref[...] = (qb * scale).astype(bf16)

    zero_pad = jnp.zeros((META_PAD - N_META, LANES), bf16)

    def meta_tile(col):
        return jnp.concatenate([kvm_ref[:, col:col + LANES], zero_pad], axis=0)

    def split_heads(q):
        zeros = jnp.zeros_like(q)
        return jnp.concatenate([jnp.where(lo_half, q, zeros), jnp.where(lo_half, zeros, q)], axis=0)

    def with_ones(v):
        return jnp.concatenate([v, jnp.ones_like(v)], axis=1)

    def proj_piece(k):
        src = PROJ_SOURCES[k]
        proj_ref[:, k * PROJ_PIECE:(k + 1) * PROJ_PIECE] = jnp.dot(
            n_ref[cur], wr_ref[:, src:src + PROJ_PIECE], preferred_element_type=f32)

    def za_transposed_piece(k):
        rows = slice(k * PROJ_PIECE, (k + 1) * PROJ_PIECE)
        zat_ref[rows] = lax.dot_general(wt_ref[A_KV_WIDTH + k * PROJ_PIECE:A_KV_WIDTH + (k + 1) * PROJ_PIECE],
                                        n_ref[cur], _NT, preferred_element_type=f32)

    def next_kv_a():
        cn, sn = _rope_coeffs(cb_ref, sb_ref, offn_ref, first_half)
        kv_a_piece(n_ref[nxt], cn, sn, ring_store(slot_nxt))

    def next_va_transposed():
        vat_ring_ref[slot_nxt] = va_transposed(n_ref[nxt])

    def next_kv_b(k):
        kv_b_piece(n_ref[nxt], k, ring_store(slot_nxt))

    n_za = A_WIDTH // PROJ_PIECE
    fills = ([next_kv_a, next_va_transposed] + [functools.partial(next_kv_b, k) for k in range(n_kv_b)]
             + [next_halo] + [functools.partial(za_transposed_piece, k) for k in range(n_za)]
             + [functools.partial(proj_piece, k) for k in range(n_za, len(PROJ_SOURCES))])

    def pipelined(stages, fill_plan):
        pending = stages[0][0](stages[0][2])
        for t, (_, finish, item) in enumerate(stages):
            nxt_scores = stages[t + 1][0](stages[t + 1][2]) if t + 1 < len(stages) else None
            for _ in range(fill_plan[t]):
                fills.pop(0)()
            finish(item, pending)
            pending = nxt_scores

    a_tiles = A_GROUP // 2
    kmetas_a = [meta_tile(KV_KA + g * LANES) for g in range(A_KV_HEADS)]
    a_keys = META_PAD + 3 * BLOCK
    key_slot = lax.broadcasted_iota(jnp.int32, (1, a_keys), 1)
    ones_rows = jnp.where(lax.broadcasted_iota(jnp.int32, (16, a_keys), 0) == 0, 1.0, 0.0).astype(bf16)

    def a_window(col, tile, j):
        w0 = HALO + j * BLOCK - BLOCK
        return win_ref[w0:w0 + 3 * BLOCK, col + tile * LANES:col + (tile + 1) * LANES]

    def a_scores(item):
        g, j = item
        first = jnp.logical_and(i == 0, j == 0)
        last = jnp.logical_and(i == n_chunks - 1, j == BLOCKS_PER_CHUNK - 1)
        mask_t = amask_ref[jnp.where(first, 1, jnp.where(last, 2, 0))]
        kcat = jnp.concatenate([kmetas_a[g], a_window(KV_KA, g, j)], axis=0)
        k_aug = jnp.concatenate([kcat, mask_t], axis=1)
        q_tiles = [qa_ref[j * BLOCK:(j + 1) * BLOCK, (a_tiles * g + t) * LANES:(a_tiles * g + t + 1) * LANES]
                   for t in range(a_tiles)]
        zeros = jnp.zeros_like(q_tiles[0])
        qs = jnp.concatenate([jnp.where(lo_half, q, zeros) for q in q_tiles]
                             + [jnp.where(lo_half, zeros, q) for q in q_tiles], axis=0)
        q_aug = jnp.concatenate([qs, eye_ref[...]], axis=1)
        return lax.dot_general(q_aug, k_aug, _NT, preferred_element_type=f32)

    def a_finish(item, s):
        g, j = item
        ps = []
        for rb in range(A_GROUP):
            sink = sink_ref[A_GROUP * g + 2 * (rb % a_tiles) + rb // a_tiles]
            sh = s[rb * BLOCK:(rb + 1) * BLOCK]
            m = jnp.maximum(jnp.max(sh, axis=-1, keepdims=True), sink)
            ps.append(jnp.where(key_slot == N_META, jnp.exp(sink - m), jnp.exp(sh - m)).astype(bf16))
        w0 = HALO + j * BLOCK - BLOCK
        v_t = jnp.concatenate([vat_meta_ref[g * HEAD_DIM:(g + 1) * HEAD_DIM, :],
                               vat_win_ref[g * HEAD_DIM:(g + 1) * HEAD_DIM, w0:w0 + 3 * BLOCK]], axis=1)
        o_t = lax.dot_general(jnp.concatenate([v_t, ones_rows], axis=0), jnp.concatenate(ps, axis=0), _NT,
                              preferred_element_type=f32)
        o_t = o_t[:HEAD_DIM] / o_t[HEAD_DIM:HEAD_DIM + 1]
        for rb in range(A_GROUP):
            head = A_GROUP * g + 2 * (rb % a_tiles) + rb // a_tiles
            oa_ref[head * HEAD_DIM:(head + 1) * HEAD_DIM, j * BLOCK:(j + 1) * BLOCK] = (
                o_t[:, rb * BLOCK:(rb + 1) * BLOCK])

    a_stages = [(a_scores, a_finish, (g, j)) for last in (False, True) for g in range(A_KV_HEADS)
                for j in range(BLOCKS_PER_CHUNK) if (j == BLOCKS_PER_CHUNK - 1) == last]

    kmetas = [meta_tile(KV_KB + p * LANES) for p in range(B_HEADS // 2)]
    vmetas = [meta_tile(KV_VB + p * LANES) for p in range(B_HEADS // 2)]

    def unit_geometry(step):
        r0 = i * ROWS_PER_CHUNK + 2 * step
        rs0 = jnp.clip(r0 - NA_KH // 2, 0, n_rows - NA_KH)
        rs1 = jnp.clip(r0 + 1 - NA_KH // 2, 0, n_rows - NA_KH)
        shifts = (r0 - rs0, r0 + 1 - rs0)
        variant = 1 - (rs1 - rs0)
        woff = pl.multiple_of((rs0 - i * ROWS_PER_CHUNK) * GRID_W + HALO, BLOCK)
        return shifts, variant, woff

    geometry = [unit_geometry(step) for step in range(ROWS_PER_CHUNK // 2)]

    def b_scores(item):
        step, pairs = item
        _, variant, woff = geometry[step]
        out = []
        for p in pairs:
            kcat = jnp.concatenate(
                [win_ref[pl.ds(woff, B_WIN_KEYS), KV_KB + p * LANES:KV_KB + (p + 1) * LANES],
                 kmetas[p]], axis=0)
            k_aug = jnp.concatenate([kcat, brow_ref[variant]], axis=1)
            qs = split_heads(qb_ref[step * BLOCK:(step + 1) * BLOCK, p * LANES:(p + 1) * LANES])
            q_aug = jnp.concatenate([qs, bsel_ref[...]], axis=1)
            out.append(lax.dot_general(q_aug, k_aug, _NT, preferred_element_type=f32))
        return out

    def b_finish(item, scores):
        step, pairs = item
        shifts, _, woff = geometry[step]
        probs = []
        for p, s in zip(pairs, scores):
            bias = jnp.concatenate(
                [jnp.concatenate([bpair_ref[p, 2 * jj + NA_KH - 1 - shifts[u], hh]
                                  for jj in range(B_WIN_ROWS // 2)], axis=1)
                 for hh in range(2) for u in range(2)], axis=0)
            s_loc = s[:, :B_WIN_KEYS] + bias
            s_met = s[:, B_WIN_KEYS:]
            m = jnp.maximum(jnp.max(s_loc, axis=-1, keepdims=True),
                            jnp.max(s_met, axis=-1, keepdims=True))
            probs.append(jnp.concatenate([jnp.exp(s_loc - m), jnp.exp(s_met - m)],
                                         axis=1).astype(bf16))
        for p, pr in zip(pairs, probs):
            vcat = jnp.concatenate(
                [win_ref[pl.ds(woff, B_WIN_KEYS), KV_VB + p * LANES:KV_VB + (p + 1) * LANES],
                 vmetas[p]], axis=0)
            o = jnp.dot(pr, with_ones(vcat), preferred_element_type=f32)
            o = o[:, :LANES] / o[:, LANES:]
            ob_ref[step * BLOCK:(step + 1) * BLOCK, p * LANES:(p + 1) * LANES] = jnp.where(
                lo_half, o[:BLOCK], o[BLOCK:])

    b_stages = [(b_scores, b_finish, (step, tuple(range(p0, p0 + B_PAIRS_PER_ITEM))))
                for step in range(ROWS_PER_CHUNK // 2)
                for p0 in range(0, B_HEADS // 2, B_PAIRS_PER_ITEM)]
    first_reader = next(t for t, (_, _, (g, j)) in enumerate(a_stages) if j == BLOCKS_PER_CHUNK - 1)
    assert sum(FILL_PLAN[:first_reader - 1]) >= 2 + n_kv_b + 1
    assert len(a_stages) + len(b_stages) == len(FILL_PLAN) and sum(FILL_PLAN) == len(fills)
    pipelined(a_stages + b_stages, FILL_PLAN)


    def gated_branch(o_ref, z_col, w_ref, g_col):
        z = proj_ref[:, z_col:z_col + A_WIDTH]
        a = (o_ref[...] * (z * jax.nn.sigmoid(z))).astype(bf16)
        y = jnp.dot(a, w_ref[...], preferred_element_type=f32)
        return jax.nn.sigmoid(proj_ref[:, g_col:g_col + D_MODEL]) * y

    za_t = zat_ref[...]
    a_t = (oa_ref[...] * (za_t * jax.nn.sigmoid(za_t))).astype(bf16)
    y_a = lax.dot_general(a_t, wpa_ref[...], (((0,), (0,)), ((), ())), preferred_element_type=f32)
    merged = (jax.nn.sigmoid(proj_ref[:, P_GA:P_GA + D_MODEL]) * y_a
              + gated_branch(ob_ref, P_ZB, wpb_ref, P_GB))
    h = x + jnp.dot(merged.astype(bf16), wout_ref[...], preferred_element_type=f32)
    out_ref[0] = _rmsnorm(h, fgain_ref[...])


def _layer_call(sink, x, meta, rope_base, rope_off, gain, fgain, w, w_t, w_pa, w_pb, w_out,
                amasks, btables):
    bsz, seq, _ = x.shape
    n_chunks = seq // CHUNK
    n_steps = bsz * n_chunks
    const2 = lambda t: (0, 0)
    const3 = lambda t: (0, 0, 0)
    resident = dict(pipeline_mode=pl.Buffered(1))
    body = functools.partial(_layer_body, n_chunks=n_chunks, n_rows=seq // GRID_W)

    def chunk_of(t):
        return t // n_chunks, t % n_chunks

    def next_chunk(t):
        return chunk_of(jnp.minimum(t + 1, n_steps - 1))

    return pl.pallas_call(
        body,
        grid=(n_steps,),
        in_specs=[
            pl.BlockSpec(memory_space=pltpu.SMEM),
            pl.BlockSpec((1, CHUNK, D_MODEL), lambda t: (*chunk_of(t), 0)),
            pl.BlockSpec((1, CHUNK, D_MODEL), lambda t: (*next_chunk(t), 0)),
            pl.BlockSpec((N_META, D_MODEL), const2, **resident),
            pl.BlockSpec((CHUNK, LANES), const2, **resident),
            pl.BlockSpec((CHUNK, LANES), const2, **resident),
            pl.BlockSpec((None, 2, LANES), lambda t: (chunk_of(t)[1], 0, 0)),
            pl.BlockSpec((None, 2, LANES), lambda t: (next_chunk(t)[1], 0, 0)),
            pl.BlockSpec((1, D_MODEL), const2, **resident),
            pl.BlockSpec((1, D_MODEL), const2, **resident),
            pl.BlockSpec((D_MODEL, IN_COLS), const2, **resident),
            pl.BlockSpec((A_KV_WIDTH + A_WIDTH, D_MODEL), const2, **resident),
            pl.BlockSpec((A_WIDTH, D_MODEL), const2, **resident),
            pl.BlockSpec((B_WIDTH, D_MODEL), const2, **resident),
            pl.BlockSpec((D_MODEL, D_MODEL), const2, **resident),
            pl.BlockSpec((3, META_PAD + 3 * BLOCK, BLOCK), const3, **resident),
            pl.BlockSpec((A_GROUP * BLOCK, BLOCK), const2, **resident),
            pl.BlockSpec((B_HEADS // 2, 2 * NA_KH, 2, GRID_W, LANES), lambda t: (0, 0, 0, 0, 0),
                         **resident),
            pl.BlockSpec((2, B_WIN_KEYS + META_PAD, LANES), const3, **resident),
            pl.BlockSpec((2 * BLOCK, LANES), const2, **resident),
        ],
        out_specs=pl.BlockSpec((1, CHUNK, D_MODEL), lambda t: (*chunk_of(t), 0)),
        out_shape=jax.ShapeDtypeStruct((bsz, seq, D_MODEL), jnp.float32),
        scratch_shapes=[
            pltpu.VMEM((2, CHUNK, D_MODEL), jnp.bfloat16),
            pltpu.VMEM((3, CHUNK, KV_COLS), jnp.bfloat16),
            pltpu.VMEM((N_META, KV_COLS), jnp.bfloat16),
            pltpu.VMEM((CHUNK + 2 * HALO, KV_COLS), jnp.bfloat16),
            pltpu.VMEM((CHUNK, A_WIDTH), jnp.bfloat16),
            pltpu.VMEM((CHUNK, B_WIDTH), jnp.bfloat16),
            pltpu.VMEM((CHUNK, A_WIDTH), jnp.float32),
            pltpu.VMEM((CHUNK, B_WIDTH), jnp.float32),
            pltpu.VMEM((CHUNK, P_COLS), jnp.float32),
            pltpu.VMEM((3, A_KV_WIDTH, CHUNK), jnp.bfloat16),
            pltpu.VMEM((A_KV_WIDTH, META_PAD), jnp.bfloat16),
            pltpu.VMEM((A_KV_WIDTH, CHUNK + 2 * HALO), jnp.bfloat16),
            pltpu.VMEM((A_WIDTH, CHUNK), jnp.float32),
        ],
        compiler_params=pltpu.CompilerParams(
            dimension_semantics=("arbitrary",), vmem_limit_bytes=VMEM_LIMIT_BYTES),
        name="hybrid_layer",
    )(sink, x, x, meta, *rope_base, rope_off, rope_off, gain, fgain, w, w_t, w_pa, w_pb, w_out,
      *amasks, *btables)


def _rope_tables(n_chunks):
    d = np.arange(LANES) % HEAD_DIM
    inv_freq = ROPE_THETA ** (-jnp.asarray(d % ROT_HALF, jnp.float32) / ROT_HALF)
    inv_lane = jnp.where(d < ROT_DIM, inv_freq, 0.0)[None, :]
    base = jnp.arange(CHUNK, dtype=jnp.float32)[:, None] * inv_lane
    off = (N_META + CHUNK * jnp.arange(n_chunks, dtype=jnp.float32))[:, None] * inv_lane
    return (jnp.cos(base), jnp.sin(base)), jnp.stack([jnp.cos(off), jnp.sin(off)], axis=1)


def _band_masks():
    qi = np.arange(BLOCK)[:, None]
    col = np.arange(META_PAD + 3 * BLOCK)[None, :]
    kj = col - META_PAD
    band = (col >= META_PAD) & (np.abs(kj - BLOCK - qi) <= WINDOW)
    meta = np.broadcast_to(col < N_META, band.shape)
    variants = [meta | band, meta | (band & (kj >= BLOCK)), meta | (band & (kj < 2 * BLOCK))]
    masks = np.where(np.stack(variants), 0.0, NEG_INF).astype(np.float32).transpose(0, 2, 1)
    eye = np.tile(np.eye(BLOCK, dtype=np.float32), (A_GROUP, 1))
    return jnp.asarray(masks, jnp.bfloat16), jnp.asarray(eye, jnp.bfloat16)


def _na_bias(rpb):
    cq = np.arange(GRID_W)[:, None]
    kc = np.arange(GRID_W)[None, :]
    start = np.clip(cq - NA_KW // 2, 0, GRID_W - NA_KW)
    valid = (kc >= start) & (kc < start + NA_KW)
    n_pairs, n_dr, n_dc = B_HEADS // 2, 2 * NA_KH - 1, 2 * NA_KW - 1
    onehot = ((kc - cq + NA_KW - 1)[..., None] == np.arange(n_dc)) & valid[..., None]
    select = np.zeros((GRID_W, 2, n_dc, 2, GRID_W), np.float32)
    for t in range(2):
        select[:, t, :, t, :] = onehot.transpose(0, 2, 1)
    mask = np.where(np.tile(valid, (1, 2)), 0.0, NEG_INF).astype(np.float32)
    select = np.concatenate([select.reshape(GRID_W, 2 * n_dc, LANES), mask[:, None, :]], axis=1)
    rows = jnp.pad(rpb.reshape(n_pairs, 2, n_dr, n_dc), ((0, 0), (0, 0), (0, 2), (0, 0)))
    rows = jnp.stack([rows[:, :, :2 * NA_KH], rows[:, :, 1:]], axis=3)
    rows = jnp.concatenate([rows.reshape(n_pairs, 2, 2 * NA_KH, 2 * n_dc),
                            jnp.ones((n_pairs, 2, 2 * NA_KH, 1), jnp.float32)], axis=-1)
    pair = jnp.einsum('prhe,cen->prhcn', jnp.transpose(rows, (0, 2, 1, 3)), jnp.asarray(select),
                      precision=lax.Precision.HIGHEST)

    key_row = np.arange(B_WIN_KEYS + META_PAD) // GRID_W
    is_meta = (np.arange(B_WIN_KEYS + META_PAD) >= B_WIN_KEYS) & (
        np.arange(B_WIN_KEYS + META_PAD) < B_WIN_KEYS + N_META)
    rows = np.zeros((2, B_WIN_KEYS + META_PAD, LANES), np.float32)
    for variant, starts in enumerate(((0, 1), (0, 0))):
        for u, s0 in enumerate(starts):
            seen = ((key_row >= s0) & (key_row < s0 + NA_KH)) | is_meta
            rows[variant, :, u] = np.where(seen, 0.0, NEG_INF)
    sel = np.zeros((2, 2, GRID_W, LANES), np.float32)
    sel[:, 0, :, 0] = 1.0
    sel[:, 1, :, 1] = 1.0
    return (pair, jnp.asarray(rows, jnp.bfloat16),
            jnp.asarray(sel.reshape(2 * BLOCK, LANES), jnp.bfloat16))


def kernel(x, meta_tokens, norm_gain, w_in, sink_logits, rel_pos_bias, w_proj_a, w_proj_b, w_out,
           final_norm_gain):
    bsz, seq, _ = x.shape
    assert seq % CHUNK == 0 and seq // GRID_W >= NA_KH and norm_gain.shape[0] == 1
    bf16 = jnp.bfloat16
    w = w_in[0].astype(bf16)
    w_t = jnp.concatenate([w[:, C_VA:C_VA + A_KV_WIDTH], w[:, C_ZA:C_ZA + A_WIDTH]], axis=1).T

    rope_base, rope_off = _rope_tables(seq // CHUNK)
    gain = norm_gain[0][None]
    return _layer_call(sink_logits[0], x, meta_tokens, rope_base, rope_off, gain,
                       final_norm_gain[None], w, w_t, w_proj_a[0].astype(bf16),
                       w_proj_b[0].astype(bf16), w_out[0].astype(bf16), _band_masks(),
                       _na_bias(rel_pos_bias[0]))
```

```python
import functools

import numpy as np
import jax
import jax.numpy as jnp
from jax import lax
from jax.experimental import pallas as pl
from jax.experimental.pallas import tpu as pltpu

D_MODEL = 1024
N_META = 16
HEAD_DIM = 64
RMS_EPS = 1e-6
NEG_INF = -1e30
A_HEADS = 8
A_KV_HEADS = 2
A_GROUP = A_HEADS // A_KV_HEADS
A_WIDTH = A_HEADS * HEAD_DIM
A_KV_WIDTH = A_KV_HEADS * HEAD_DIM
WINDOW = 128
BLOCK = 128
ROT_DIM = HEAD_DIM // 4
ROT_HALF = ROT_DIM // 2
ROPE_THETA = 500000.0
B_HEADS = 8
B_WIDTH = B_HEADS * HEAD_DIM
GRID_W = 64
NA_KH = 8
NA_KW = 16
SPLIT_SIZES = (A_WIDTH, A_KV_WIDTH, A_KV_WIDTH, A_WIDTH, B_WIDTH, B_WIDTH, B_WIDTH, B_WIDTH,
               D_MODEL, D_MODEL)

LANES = 128
CHUNK = 512
ROWS_PER_CHUNK = CHUNK // GRID_W
BLOCKS_PER_CHUNK = CHUNK // BLOCK
HALO = CHUNK // 2
assert HALO >= BLOCK and HALO >= (NA_KH // 2) * GRID_W
META_PAD = LANES
B_WIN_ROWS = NA_KH + 2
B_WIN_KEYS = B_WIN_ROWS * GRID_W
B_PAIRS_PER_ITEM = 2
PROJ_PIECE = 256
FILL_PLAN = (2, 1, 1, 1, 2, 1, 1, 1, 1, 1, 1, 1, 1, 1, 1, 1)

KV_KA = 0
KV_VA = KV_KA + A_KV_HEADS * LANES
KV_KB = KV_VA + A_KV_HEADS * LANES
KV_VB = KV_KB + B_WIDTH
KV_COLS = KV_VB + B_WIDTH

(C_QA, C_KA, C_VA, C_ZA, C_QB, C_KB, C_VB, C_ZB, C_GA, C_GB, IN_COLS) = (
    int(c) for c in np.cumsum((0,) + SPLIT_SIZES))
assert A_KV_WIDTH == LANES and C_VA == C_KA + A_KV_WIDTH and C_VB == C_KB + B_WIDTH

P_ZA = 0
P_ZB = P_ZA + A_WIDTH
P_GA = P_ZB + B_WIDTH
P_GB = P_GA + D_MODEL
P_COLS = P_GB + D_MODEL
PROJ_SOURCES = tuple(
    src + off for src, width in ((C_ZA, A_WIDTH), (C_ZB, B_WIDTH), (C_GA, D_MODEL), (C_GB, D_MODEL))
    for off in range(0, width, PROJ_PIECE))

VMEM_LIMIT_BYTES = 58 * 1024 * 1024

_NT = (((1,), (1,)), ((), ()))


def _rmsnorm(x, gain):
    return x * lax.rsqrt(jnp.mean(x * x, axis=-1, keepdims=True) + RMS_EPS) * gain


def _lane_patterns():
    lane = lax.broadcasted_iota(jnp.int32, (1, LANES), 1)
    return lane < HEAD_DIM, (lane & (HEAD_DIM - 1)) < ROT_HALF


def _rope_coeffs(cb_ref, sb_ref, off_ref, first_half):
    cb, sb = cb_ref[...], sb_ref[...]
    ca, sa = off_ref[0:1, :], off_ref[1:2, :]
    c = cb * ca - sb * sa
    s = (sb * ca + cb * sa) * jnp.where(first_half, -1.0, 1.0)
    return c, s


def _rope_tile(t, c, s, first_half):
    partner = jnp.where(first_half, pltpu.roll(t, LANES - ROT_HALF, 1), pltpu.roll(t, ROT_HALF, 1))
    return t * c + partner * s


def _layer_body(sink_ref, x_ref, xn_ref, meta_ref, cb_ref, sb_ref, off_ref, offn_ref,
                gain_ref, fgain_ref, wr_ref, wpa_ref, wpb_ref, wout_ref, amask_ref, eye_ref, bpair_ref,
                brow_ref, bsel_ref,
                out_ref, n_ref, ring_ref, kvm_ref, win_ref, qa_ref, qb_ref, oa_ref, ob_ref, proj_ref,
                *, n_chunks, n_rows):
    step = pl.program_id(0)
    i = step % n_chunks
    cur, nxt = step % 2, (step + 1) % 2
    slot_prev, slot_cur, slot_nxt = (step + 2) % 3, step % 3, (step + 1) % 3
    f32, bf16 = jnp.float32, jnp.bfloat16
    scale = HEAD_DIM ** -0.5
    lo_half, first_half = _lane_patterns()

    def kv_a_piece(n, c, s, store):
        kv_a = jnp.dot(n, wr_ref[:, C_KA:C_KA + 2 * A_KV_WIDTH], preferred_element_type=f32)
        ka = _rope_tile(kv_a[:, :LANES], c, s, first_half)
        va = kv_a[:, LANES:]
        for src, col in ((ka, KV_KA), (va, KV_VA)):
            swapped = pltpu.roll(src, HEAD_DIM, 1)
            store(col, jnp.where(lo_half, src, swapped).astype(bf16))
            store(col + LANES, jnp.where(lo_half, swapped, src).astype(bf16))

    def kv_b_piece(n, k, store):
        src = C_KB + k * PROJ_PIECE
        store(KV_KB + k * PROJ_PIECE,
              jnp.dot(n, wr_ref[:, src:src + PROJ_PIECE], preferred_element_type=f32).astype(bf16))

    n_kv_b = 2 * B_WIDTH // PROJ_PIECE

    def ring_store(slot):
        def store(col, value):
            ring_ref[slot, :, col:col + value.shape[1]] = value
        return store

    def meta_store(col, value):
        kvm_ref[:, col:col + value.shape[1]] = value

    @pl.when(step == 0)
    def _():
        ring_ref[...] = jnp.zeros(ring_ref.shape, bf16)
        n_ref[0] = _rmsnorm(x_ref[0], gain_ref[...]).astype(bf16)
        c0, s0 = _rope_coeffs(cb_ref, sb_ref, off_ref, first_half)
        kv_a_piece(n_ref[0], c0, s0, ring_store(0))
        for k in range(n_kv_b):
            kv_b_piece(n_ref[0], k, ring_store(0))
        n_meta = _rmsnorm(meta_ref[...], gain_ref[...]).astype(bf16)
        sign = jnp.where(first_half, -1.0, 1.0)
        kv_a_piece(n_meta, cb_ref[0:N_META], sb_ref[0:N_META] * sign, meta_store)
        for k in range(n_kv_b):
            kv_b_piece(n_meta, k, meta_store)

    x = x_ref[0]
    n_ref[nxt] = _rmsnorm(xn_ref[0], gain_ref[...]).astype(bf16)

    win_ref[0:HALO] = ring_ref[slot_prev, CHUNK - HALO:CHUNK]
    win_ref[HALO:HALO + CHUNK] = ring_ref[slot_cur]

    def next_halo():
        win_ref[HALO + CHUNK:CHUNK + 2 * HALO] = ring_ref[slot_nxt, 0:HALO]

    c, s = _rope_coeffs(cb_ref, sb_ref, off_ref, first_half)
    qa = jnp.dot(n_ref[cur], wr_ref[:, C_QA:C_QA + A_WIDTH], preferred_element_type=f32)
    for t in range(A_WIDTH // LANES):
        sl = slice(t * LANES, (t + 1) * LANES)
        qa_ref[:, sl] = (_rope_tile(qa[:, sl], c, s, first_half) * scale).astype(bf16)
    qb = jnp.dot(n_ref[cur], wr_ref[:, C_QB:C_QB + B_WIDTH], preferred_element_type=f32)
    qb_ref[...] = (qb * scale).astype(bf16)

    zero_pad = jnp.zeros((META_PAD - N_META, LANES), bf16)

    def meta_tile(col):
        return jnp.concatenate([kvm_ref[:, col:col + LANES], zero_pad], axis=0)

    def split_heads(q):
        zeros = jnp.zeros_like(q)
        return jnp.concatenate([jnp.where(lo_half, q, zeros), jnp.where(lo_half, zeros, q)], axis=0)

    def with_ones(v):
        return jnp.concatenate([v, jnp.ones_like(v)], axis=1)

    def proj_piece(k):
        src = PROJ_SOURCES[k]
        proj_ref[:, k * PROJ_PIECE:(k + 1) * PROJ_PIECE] = jnp.dot(
            n_ref[cur], wr_ref[:, src:src + PROJ_PIECE], preferred_element_type=f32)

    def next_kv_a():
        cn, sn = _rope_coeffs(cb_ref, sb_ref, offn_ref, first_half)
        kv_a_piece(n_ref[nxt], cn, sn, ring_store(slot_nxt))

    def next_kv_b(k):
        kv_b_piece(n_ref[nxt], k, ring_store(slot_nxt))

    fills = ([next_kv_a] + [functools.partial(next_kv_b, k) for k in range(n_kv_b)] + [next_halo]
             + [functools.partial(proj_piece, k) for k in range(len(PROJ_SOURCES))])

    def pipelined(stages, fill_plan):
        pending = stages[0][0](stages[0][2])
        for t, (_, finish, item) in enumerate(stages):
            nxt_scores = stages[t + 1][0](stages[t + 1][2]) if t + 1 < len(stages) else None
            for _ in range(fill_plan[t]):
                fills.pop(0)()
            finish(item, pending)
            pending = nxt_scores

    a_tiles = A_GROUP // 2
    kmetas_a = [meta_tile(KV_KA + g * LANES) for g in range(A_KV_HEADS)]
    vmetas_a = [meta_tile(KV_VA + g * LANES) for g in range(A_KV_HEADS)]

    def a_window(col, tile, j):
        w0 = HALO + j * BLOCK - BLOCK
        return win_ref[w0:w0 + 3 * BLOCK, col + tile * LANES:col + (tile + 1) * LANES]

    def a_scores(item):
        g, j = item
        first = jnp.logical_and(i == 0, j == 0)
        last = jnp.logical_and(i == n_chunks - 1, j == BLOCKS_PER_CHUNK - 1)
        mask_t = amask_ref[jnp.where(first, 1, jnp.where(last, 2, 0))]
        kcat = jnp.concatenate([kmetas_a[g], a_window(KV_KA, g, j)], axis=0)
        k_aug = jnp.concatenate([kcat, mask_t], axis=1)
        q_tiles = [qa_ref[j * BLOCK:(j + 1) * BLOCK, (a_tiles * g + t) * LANES:(a_tiles * g + t + 1) * LANES]
                   for t in range(a_tiles)]
        zeros = jnp.zeros_like(q_tiles[0])
        qs = jnp.concatenate([jnp.where(lo_half, q, zeros) for q in q_tiles]
                             + [jnp.where(lo_half, zeros, q) for q in q_tiles], axis=0)
        q_aug = jnp.concatenate([qs, eye_ref[...]], axis=1)
        return lax.dot_general(q_aug, k_aug, _NT, preferred_element_type=f32)

    def a_finish(item, s):
        g, j = item
        ps, es = [], []
        for rb in range(A_GROUP):
            sink = sink_ref[A_GROUP * g + 2 * (rb % a_tiles) + rb // a_tiles]
            sh = s[rb * BLOCK:(rb + 1) * BLOCK]
            m = jnp.maximum(jnp.max(sh, axis=-1, keepdims=True), sink)
            ps.append(jnp.exp(sh - m).astype(bf16))
            es.append(jnp.exp(sink - m))
        vcat = jnp.concatenate([vmetas_a[g], a_window(KV_VA, g, j)], axis=0)
        o = jnp.dot(jnp.concatenate(ps, axis=0), with_ones(vcat), preferred_element_type=f32)
        for t in range(a_tiles):
            oe = o[t * BLOCK:(t + 1) * BLOCK]
            oo = o[(a_tiles + t) * BLOCK:(a_tiles + t + 1) * BLOCK]
            den = (jnp.where(lo_half, oe[:, LANES:], oo[:, LANES:])
                   + jnp.where(lo_half, es[t], es[a_tiles + t]))
            oa_ref[j * BLOCK:(j + 1) * BLOCK, (a_tiles * g + t) * LANES:(a_tiles * g + t + 1) * LANES] = (
                jnp.where(lo_half, oe[:, :LANES], oo[:, :LANES]) / den)

    a_stages = [(a_scores, a_finish, (g, j)) for last in (False, True) for g in range(A_KV_HEADS)
                for j in range(BLOCKS_PER_CHUNK) if (j == BLOCKS_PER_CHUNK - 1) == last]

    kmetas = [meta_tile(KV_KB + p * LANES) for p in range(B_HEADS // 2)]
    vmetas = [meta_tile(KV_VB + p * LANES) for p in range(B_HEADS // 2)]

    def unit_geometry(step):
        r0 = i * ROWS_PER_CHUNK + 2 * step
        rs0 = jnp.clip(r0 - NA_KH // 2, 0, n_rows - NA_KH)
        rs1 = jnp.clip(r0 + 1 - NA_KH // 2, 0, n_rows - NA_KH)
        shifts = (r0 - rs0, r0 + 1 - rs0)
        variant = 1 - (rs1 - rs0)
        woff = pl.multiple_of((rs0 - i * ROWS_PER_CHUNK) * GRID_W + HALO, BLOCK)
        return shifts, variant, woff

    geometry = [unit_geometry(step) for step in range(ROWS_PER_CHUNK // 2)]

    def b_scores(item):
        step, pairs = item
        _, variant, woff = geometry[step]
        out = []
        for p in pairs:
            kcat = jnp.concatenate(
                [win_ref[pl.ds(woff, B_WIN_KEYS), KV_KB + p * LANES:KV_KB + (p + 1) * LANES],
                 kmetas[p]], axis=0)
            k_aug = jnp.concatenate([kcat, brow_ref[variant]], axis=1)
            qs = split_heads(qb_ref[step * BLOCK:(step + 1) * BLOCK, p * LANES:(p + 1) * LANES])
            q_aug = jnp.concatenate([qs, bsel_ref[...]], axis=1)
            out.append(lax.dot_general(q_aug, k_aug, _NT, preferred_element_type=f32))
        return out

    def b_finish(item, scores):
        step, pairs = item
        shifts, _, woff = geometry[step]
        probs = []
        for p, s in zip(pairs, scores):
            bias = jnp.concatenate(
                [jnp.concatenate([bpair_ref[p, 2 * jj + NA_KH - 1 - shifts[u], hh]
                                  for jj in range(B_WIN_ROWS // 2)], axis=1)
                 for hh in range(2) for u in range(2)], axis=0)
            s_loc = s[:, :B_WIN_KEYS] + bias
            s_met = s[:, B_WIN_KEYS:]
            m = jnp.maximum(jnp.max(s_loc, axis=-1, keepdims=True),
                            jnp.max(s_met, axis=-1, keepdims=True))
            probs.append(jnp.concatenate([jnp.exp(s_loc - m), jnp.exp(s_met - m)],
                                         axis=1).astype(bf16))
        for p, pr in zip(pairs, probs):
            vcat = jnp.concatenate(
                [win_ref[pl.ds(woff, B_WIN_KEYS), KV_VB + p * LANES:KV_VB + (p + 1) * LANES],
                 vmetas[p]], axis=0)
            o = jnp.dot(pr, with_ones(vcat), preferred_element_type=f32)
            o = o[:, :LANES] / o[:, LANES:]
            ob_ref[step * BLOCK:(step + 1) * BLOCK, p * LANES:(p + 1) * LANES] = jnp.where(
                lo_half, o[:BLOCK], o[BLOCK:])

    b_stages = [(b_scores, b_finish, (step, tuple(range(p0, p0 + B_PAIRS_PER_ITEM))))
                for step in range(ROWS_PER_CHUNK // 2)
                for p0 in range(0, B_HEADS // 2, B_PAIRS_PER_ITEM)]
    first_reader = next(t for t, (_, _, (g, j)) in enumerate(a_stages) if j == BLOCKS_PER_CHUNK - 1)
    assert sum(FILL_PLAN[:first_reader - 1]) >= 1 + n_kv_b + 1
    assert len(a_stages) + len(b_stages) == len(FILL_PLAN) and sum(FILL_PLAN) == len(fills)
    pipelined(a_stages + b_stages, FILL_PLAN)


    def gated_branch(o_ref, z_col, w_ref, g_col):
        z = proj_ref[:, z_col:z_col + A_WIDTH]
        a = (o_ref[...] * (z * jax.nn.sigmoid(z))).astype(bf16)
        y = jnp.dot(a, w_ref[...], preferred_element_type=f32)
        return jax.nn.sigmoid(proj_ref[:, g_col:g_col + D_MODEL]) * y

    merged = gated_branch(oa_ref, P_ZA, wpa_ref, P_GA) + gated_branch(ob_ref, P_ZB, wpb_ref, P_GB)
    h = x + jnp.dot(merged.astype(bf16), wout_ref[...], preferred_element_type=f32)
    out_ref[0] = _rmsnorm(h, fgain_ref[...])


def _layer_call(sink, x, meta, rope_base, rope_off, gain, fgain, w, w_pa, w_pb, w_out,
                amasks, btables):
    bsz, seq, _ = x.shape
    n_chunks = seq // CHUNK
    n_steps = bsz * n_chunks
    const2 = lambda t: (0, 0)
    const3 = lambda t: (0, 0, 0)
    resident = dict(pipeline_mode=pl.Buffered(1))
    body = functools.partial(_layer_body, n_chunks=n_chunks, n_rows=seq // GRID_W)

    def chunk_of(t):
        return t // n_chunks, t % n_chunks

    def next_chunk(t):
        return chunk_of(jnp.minimum(t + 1, n_steps - 1))

    return pl.pallas_call(
        body,
        grid=(n_steps,),
        in_specs=[
            pl.BlockSpec(memory_space=pltpu.SMEM),
            pl.BlockSpec((1, CHUNK, D_MODEL), lambda t: (*chunk_of(t), 0)),
            pl.BlockSpec((1, CHUNK, D_MODEL), lambda t: (*next_chunk(t), 0)),
            pl.BlockSpec((N_META, D_MODEL), const2, **resident),
            pl.BlockSpec((CHUNK, LANES), const2, **resident),
            pl.BlockSpec((CHUNK, LANES), const2, **resident),
            pl.BlockSpec((None, 2, LANES), lambda t: (chunk_of(t)[1], 0, 0)),
            pl.BlockSpec((None, 2, LANES), lambda t: (next_chunk(t)[1], 0, 0)),
            pl.BlockSpec((1, D_MODEL), const2, **resident),
            pl.BlockSpec((1, D_MODEL), const2, **resident),
            pl.BlockSpec((D_MODEL, IN_COLS), const2, **resident),
            pl.BlockSpec((A_WIDTH, D_MODEL), const2, **resident),
            pl.BlockSpec((B_WIDTH, D_MODEL), const2, **resident),
            pl.BlockSpec((D_MODEL, D_MODEL), const2, **resident),
            pl.BlockSpec((3, META_PAD + 3 * BLOCK, BLOCK), const3, **resident),
            pl.BlockSpec((A_GROUP * BLOCK, BLOCK), const2, **resident),
            pl.BlockSpec((B_HEADS // 2, 2 * NA_KH, 2, GRID_W, LANES), lambda t: (0, 0, 0, 0, 0),
                         **resident),
            pl.BlockSpec((2, B_WIN_KEYS + META_PAD, LANES), const3, **resident),
            pl.BlockSpec((2 * BLOCK, LANES), const2, **resident),
        ],
        out_specs=pl.BlockSpec((1, CHUNK, D_MODEL), lambda t: (*chunk_of(t), 0)),
        out_shape=jax.ShapeDtypeStruct((bsz, seq, D_MODEL), jnp.float32),
        scratch_shapes=[
            pltpu.VMEM((2, CHUNK, D_MODEL), jnp.bfloat16),
            pltpu.VMEM((3, CHUNK, KV_COLS), jnp.bfloat16),
            pltpu.VMEM((N_META, KV_COLS), jnp.bfloat16),
            pltpu.VMEM((CHUNK + 2 * HALO, KV_COLS), jnp.bfloat16),
            pltpu.VMEM((CHUNK, A_WIDTH), jnp.bfloat16),
            pltpu.VMEM((CHUNK, B_WIDTH), jnp.bfloat16),
            pltpu.VMEM((CHUNK, A_WIDTH), jnp.float32),
            pltpu.VMEM((CHUNK, B_WIDTH), jnp.float32),
            pltpu.VMEM((CHUNK, P_COLS), jnp.float32),
        ],
        compiler_params=pltpu.CompilerParams(
            dimension_semantics=("arbitrary",), vmem_limit_bytes=VMEM_LIMIT_BYTES),
        name="hybrid_layer",
    )(sink, x, x, meta, *rope_base, rope_off, rope_off, gain, fgain, w, w_pa, w_pb, w_out,
      *amasks, *btables)


def _rope_tables(n_chunks):
    d = np.arange(LANES) % HEAD_DIM
    inv_freq = ROPE_THETA ** (-jnp.asarray(d % ROT_HALF, jnp.float32) / ROT_HALF)
    inv_lane = jnp.where(d < ROT_DIM, inv_freq, 0.0)[None, :]
    base = jnp.arange(CHUNK, dtype=jnp.float32)[:, None] * inv_lane
    off = (N_META + CHUNK * jnp.arange(n_chunks, dtype=jnp.float32))[:, None] * inv_lane
    return (jnp.cos(base), jnp.sin(base)), jnp.stack([jnp.cos(off), jnp.sin(off)], axis=1)


def _band_masks():
    qi = np.arange(BLOCK)[:, None]
    col = np.arange(META_PAD + 3 * BLOCK)[None, :]
    kj = col - META_PAD
    band = (col >= META_PAD) & (np.abs(kj - BLOCK - qi) <= WINDOW)
    meta = np.broadcast_to(col < N_META, band.shape)
    variants = [meta | band, meta | (band & (kj >= BLOCK)), meta | (band & (kj < 2 * BLOCK))]
    masks = np.where(np.stack(variants), 0.0, NEG_INF).astype(np.float32).transpose(0, 2, 1)
    eye = np.tile(np.eye(BLOCK, dtype=np.float32), (A_GROUP, 1))
    return jnp.asarray(masks, jnp.bfloat16), jnp.asarray(eye, jnp.bfloat16)


def _na_bias(rpb):
    cq = np.arange(GRID_W)[:, None]
    kc = np.arange(GRID_W)[None, :]
    start = np.clip(cq - NA_KW // 2, 0, GRID_W - NA_KW)
    valid = (kc >= start) & (kc < start + NA_KW)
    n_pairs, n_dr, n_dc = B_HEADS // 2, 2 * NA_KH - 1, 2 * NA_KW - 1
    onehot = ((kc - cq + NA_KW - 1)[..., None] == np.arange(n_dc)) & valid[..., None]
    select = np.zeros((GRID_W, 2, n_dc, 2, GRID_W), np.float32)
    for t in range(2):
        select[:, t, :, t, :] = onehot.transpose(0, 2, 1)
    mask = np.where(np.tile(valid, (1, 2)), 0.0, NEG_INF).astype(np.float32)
    select = np.concatenate([select.reshape(GRID_W, 2 * n_dc, LANES), mask[:, None, :]], axis=1)
    rows = jnp.pad(rpb.reshape(n_pairs, 2, n_dr, n_dc), ((0, 0), (0, 0), (0, 2), (0, 0)))
    rows = jnp.stack([rows[:, :, :2 * NA_KH], rows[:, :, 1:]], axis=3)
    rows = jnp.concatenate([rows.reshape(n_pairs, 2, 2 * NA_KH, 2 * n_dc),
                            jnp.ones((n_pairs, 2, 2 * NA_KH, 1), jnp.float32)], axis=-1)
    pair = jnp.einsum('prhe,cen->prhcn', jnp.transpose(rows, (0, 2, 1, 3)), jnp.asarray(select),
                      precision=lax.Precision.HIGHEST)

    key_row = np.arange(B_WIN_KEYS + META_PAD) // GRID_W
    is_meta = (np.arange(B_WIN_KEYS + META_PAD) >= B_WIN_KEYS) & (
        np.arange(B_WIN_KEYS + META_PAD) < B_WIN_KEYS + N_META)
    rows = np.zeros((2, B_WIN_KEYS + META_PAD, LANES), np.float32)
    for variant, starts in enumerate(((0, 1), (0, 0))):
        for u, s0 in enumerate(starts):
            seen = ((key_row >= s0) & (key_row < s0 + NA_KH)) | is_meta
            rows[variant, :, u] = np.where(seen, 0.0, NEG_INF)
    sel = np.zeros((2, 2, GRID_W, LANES), np.float32)
    sel[:, 0, :, 0] = 1.0
    sel[:, 1, :, 1] = 1.0
    return (pair, jnp.asarray(rows, jnp.bfloat16),
            jnp.asarray(sel.reshape(2 * BLOCK, LANES), jnp.bfloat16))


def kernel(x, meta_tokens, norm_gain, w_in, sink_logits, rel_pos_bias, w_proj_a, w_proj_b, w_out,
           final_norm_gain):
    bsz, seq, _ = x.shape
    assert seq % CHUNK == 0 and seq // GRID_W >= NA_KH and norm_gain.shape[0] == 1
    bf16 = jnp.bfloat16
    w = w_in[0].astype(bf16)

    rope_base, rope_off = _rope_tables(seq // CHUNK)
    gain = norm_gain[0][None]
    return _layer_call(sink_logits[0], x, meta_tokens, rope_base, rope_off, gain,
                       final_norm_gain[None], w, w_proj_a[0].astype(bf16),
                       w_proj_b[0].astype(bf16), w_out[0].astype(bf16), _band_masks(),
                       _na_bias(rel_pos_bias[0]))
```

```python
import functools

import numpy as np
import jax
import jax.numpy as jnp
from jax import lax
from jax.experimental import pallas as pl
from jax.experimental.pallas import tpu as pltpu

D_MODEL = 1024
N_META = 16
HEAD_DIM = 64
RMS_EPS = 1e-6
NEG_INF = -1e30
A_HEADS = 8
A_KV_HEADS = 2
A_GROUP = A_HEADS // A_KV_HEADS
A_WIDTH = A_HEADS * HEAD_DIM
A_KV_WIDTH = A_KV_HEADS * HEAD_DIM
WINDOW = 128
BLOCK = 128
ROT_DIM = HEAD_DIM // 4
ROT_HALF = ROT_DIM // 2
ROPE_THETA = 500000.0
B_HEADS = 8
B_WIDTH = B_HEADS * HEAD_DIM
GRID_W = 64
NA_KH = 8
NA_KW = 16
SPLIT_SIZES = (A_WIDTH, A_KV_WIDTH, A_KV_WIDTH, A_WIDTH, B_WIDTH, B_WIDTH, B_WIDTH, B_WIDTH,
               D_MODEL, D_MODEL)

LANES = 128
CHUNK = 512
ROWS_PER_CHUNK = CHUNK // GRID_W
BLOCKS_PER_CHUNK = CHUNK // BLOCK
HALO = CHUNK // 2
assert HALO >= BLOCK and HALO >= (NA_KH // 2) * GRID_W
META_PAD = LANES
B_WIN_ROWS = NA_KH + 2
B_WIN_KEYS = B_WIN_ROWS * GRID_W
B_PAIRS_PER_ITEM = 2
PROJ_PIECE = 256
FILL_PLAN = (2, 1, 1, 1, 2, 1, 1, 1, 1, 1, 1, 1, 1, 1, 1, 1)

KV_KA = 0
KV_VA = KV_KA + A_KV_HEADS * LANES
KV_KB = KV_VA + A_KV_HEADS * LANES
KV_VB = KV_KB + B_WIDTH
KV_COLS = KV_VB + B_WIDTH

(C_QA, C_KA, C_VA, C_ZA, C_QB, C_KB, C_VB, C_ZB, C_GA, C_GB, IN_COLS) = (
    int(c) for c in np.cumsum((0,) + SPLIT_SIZES))
assert A_KV_WIDTH == LANES and C_VA == C_KA + A_KV_WIDTH and C_VB == C_KB + B_WIDTH

P_ZA = 0
P_ZB = P_ZA + A_WIDTH
P_GA = P_ZB + B_WIDTH
P_GB = P_GA + D_MODEL
P_COLS = P_GB + D_MODEL
PROJ_SOURCES = tuple(
    src + off for src, width in ((C_ZA, A_WIDTH), (C_ZB, B_WIDTH), (C_GA, D_MODEL), (C_GB, D_MODEL))
    for off in range(0, width, PROJ_PIECE))

T_PA = 0
T_PB = T_PA + A_WIDTH
T_OUT = T_PB + B_WIDTH
T_ROWS = T_OUT + D_MODEL

VMEM_LIMIT_BYTES = 58 * 1024 * 1024

_NT = (((1,), (1,)), ((), ()))


def _rmsnorm(x, gain):
    return x * lax.rsqrt(jnp.mean(x * x, axis=-1, keepdims=True) + RMS_EPS) * gain


def _lane_patterns():
    lane = lax.broadcasted_iota(jnp.int32, (1, LANES), 1)
    return lane < HEAD_DIM, (lane & (HEAD_DIM - 1)) < ROT_HALF


def _rope_coeffs(cb_ref, sb_ref, off_ref, first_half):
    cb, sb = cb_ref[...], sb_ref[...]
    ca, sa = off_ref[0:1, :], off_ref[1:2, :]
    c = cb * ca - sb * sa
    s = (sb * ca + cb * sa) * jnp.where(first_half, -1.0, 1.0)
    return c, s


def _rope_tile(t, c, s, first_half):
    partner = jnp.where(first_half, pltpu.roll(t, LANES - ROT_HALF, 1), pltpu.roll(t, ROT_HALF, 1))
    return t * c + partner * s


def _layer_body(sink_ref, x_ref, xn_ref, meta_ref, cb_ref, sb_ref, off_ref, offn_ref,
                gain_ref, fgain_ref, wr_ref, wtail_ref, amask_ref, eye_ref, bpair_ref,
                brow_ref, bsel_ref,
                out_ref, n_ref, ring_ref, kvm_ref, win_ref, qa_ref, qb_ref, oa_ref, ob_ref, proj_ref,
                *, n_chunks, n_rows):
    step = pl.program_id(0)
    i = step % n_chunks
    cur, nxt = step % 2, (step + 1) % 2
    slot_prev, slot_cur, slot_nxt = (step + 2) % 3, step % 3, (step + 1) % 3
    f32, bf16 = jnp.float32, jnp.bfloat16
    scale = HEAD_DIM ** -0.5
    lo_half, first_half = _lane_patterns()

    def kv_a_piece(n, c, s, store):
        kv_a = jnp.dot(n, wr_ref[:, C_KA:C_KA + 2 * A_KV_WIDTH], preferred_element_type=f32)
        ka = _rope_tile(kv_a[:, :LANES], c, s, first_half)
        va = kv_a[:, LANES:]
        for src, col in ((ka, KV_KA), (va, KV_VA)):
            swapped = pltpu.roll(src, HEAD_DIM, 1)
            store(col, jnp.where(lo_half, src, swapped).astype(bf16))
            store(col + LANES, jnp.where(lo_half, swapped, src).astype(bf16))

    def kv_b_piece(n, k, store):
        src = C_KB + k * PROJ_PIECE
        store(KV_KB + k * PROJ_PIECE,
              jnp.dot(n, wr_ref[:, src:src + PROJ_PIECE], preferred_element_type=f32).astype(bf16))

    n_kv_b = 2 * B_WIDTH // PROJ_PIECE

    def ring_store(slot):
        def store(col, value):
            ring_ref[slot, :, col:col + value.shape[1]] = value
        return store

    def meta_store(col, value):
        kvm_ref[:, col:col + value.shape[1]] = value

    @pl.when(step == 0)
    def _():
        ring_ref[...] = jnp.zeros(ring_ref.shape, bf16)
        n_ref[0] = _rmsnorm(x_ref[0], gain_ref[...]).astype(bf16)
        c0, s0 = _rope_coeffs(cb_ref, sb_ref, off_ref, first_half)
        kv_a_piece(n_ref[0], c0, s0, ring_store(0))
        for k in range(n_kv_b):
            kv_b_piece(n_ref[0], k, ring_store(0))
        n_meta = _rmsnorm(meta_ref[...], gain_ref[...]).astype(bf16)
        sign = jnp.where(first_half, -1.0, 1.0)
        kv_a_piece(n_meta, cb_ref[0:N_META], sb_ref[0:N_META] * sign, meta_store)
        for k in range(n_kv_b):
            kv_b_piece(n_meta, k, meta_store)

    x = x_ref[0]
    n_ref[nxt] = _rmsnorm(xn_ref[0], gain_ref[...]).astype(bf16)

    win_ref[0:HALO] = ring_ref[slot_prev, CHUNK - HALO:CHUNK]
    win_ref[HALO:HALO + CHUNK] = ring_ref[slot_cur]

    def next_halo():
        win_ref[HALO + CHUNK:CHUNK + 2 * HALO] = ring_ref[slot_nxt, 0:HALO]

    c, s = _rope_coeffs(cb_ref, sb_ref, off_ref, first_half)
    qa = jnp.dot(n_ref[cur], wr_ref[:, C_QA:C_QA + A_WIDTH], preferred_element_type=f32)
    for t in range(A_WIDTH // LANES):
        sl = slice(t * LANES, (t + 1) * LANES)
        qa_ref[:, sl] = (_rope_tile(qa[:, sl], c, s, first_half) * scale).astype(bf16)
    qb = jnp.dot(n_ref[cur], wr_ref[:, C_QB:C_QB + B_WIDTH], preferred_element_type=f32)
    qb_ref[...] = (qb * scale).astype(bf16)

    zero_pad = jnp.zeros((META_PAD - N_META, LANES), bf16)

    def meta_tile(col):
        return jnp.concatenate([kvm_ref[:, col:col + LANES], zero_pad], axis=0)

    def split_heads(q):
        zeros = jnp.zeros_like(q)
        return jnp.concatenate([jnp.where(lo_half, q, zeros), jnp.where(lo_half, zeros, q)], axis=0)

    def with_ones(v):
        return jnp.concatenate([v, jnp.ones_like(v)], axis=1)

    def proj_piece(k):
        src = PROJ_SOURCES[k]
        proj_ref[:, k * PROJ_PIECE:(k + 1) * PROJ_PIECE] = jnp.dot(
            n_ref[cur], wr_ref[:, src:src + PROJ_PIECE], preferred_element_type=f32)

    def next_kv_a():
        cn, sn = _rope_coeffs(cb_ref, sb_ref, offn_ref, first_half)
        kv_a_piece(n_ref[nxt], cn, sn, ring_store(slot_nxt))

    def next_kv_b(k):
        kv_b_piece(n_ref[nxt], k, ring_store(slot_nxt))

    fills = ([next_kv_a] + [functools.partial(next_kv_b, k) for k in range(n_kv_b)] + [next_halo]
             + [functools.partial(proj_piece, k) for k in range(len(PROJ_SOURCES))])

    def pipelined(stages, fill_plan):
        pending = stages[0][0](stages[0][2])
        for t, (_, finish, item) in enumerate(stages):
            nxt_scores = stages[t + 1][0](stages[t + 1][2]) if t + 1 < len(stages) else None
            for _ in range(fill_plan[t]):
                fills.pop(0)()
            finish(item, pending)
            pending = nxt_scores

    a_tiles = A_GROUP // 2
    kmetas_a = [meta_tile(KV_KA + g * LANES) for g in range(A_KV_HEADS)]
    vmetas_a = [meta_tile(KV_VA + g * LANES) for g in range(A_KV_HEADS)]

    def a_window(col, tile, j):
        w0 = HALO + j * BLOCK - BLOCK
        return win_ref[w0:w0 + 3 * BLOCK, col + tile * LANES:col + (tile + 1) * LANES]

    def a_scores(item):
        g, j = item
        first = jnp.logical_and(i == 0, j == 0)
        last = jnp.logical_and(i == n_chunks - 1, j == BLOCKS_PER_CHUNK - 1)
        mask_t = amask_ref[jnp.where(first, 1, jnp.where(last, 2, 0))]
        kcat = jnp.concatenate([kmetas_a[g], a_window(KV_KA, g, j)], axis=0)
        k_aug = jnp.concatenate([kcat, mask_t], axis=1)
        q_tiles = [qa_ref[j * BLOCK:(j + 1) * BLOCK, (a_tiles * g + t) * LANES:(a_tiles * g + t + 1) * LANES]
                   for t in range(a_tiles)]
        zeros = jnp.zeros_like(q_tiles[0])
        qs = jnp.concatenate([jnp.where(lo_half, q, zeros) for q in q_tiles]
                             + [jnp.where(lo_half, zeros, q) for q in q_tiles], axis=0)
        q_aug = jnp.concatenate([qs, eye_ref[...]], axis=1)
        return lax.dot_general(q_aug, k_aug, _NT, preferred_element_type=f32)

    def a_finish(item, s):
        g, j = item
        ps, es = [], []
        for rb in range(A_GROUP):
            sink = sink_ref[A_GROUP * g + 2 * (rb % a_tiles) + rb // a_tiles]
            sh = s[rb * BLOCK:(rb + 1) * BLOCK]
            m = jnp.maximum(jnp.max(sh, axis=-1, keepdims=True), sink)
            ps.append(jnp.exp(sh - m).astype(bf16))
            es.append(jnp.exp(sink - m))
        vcat = jnp.concatenate([vmetas_a[g], a_window(KV_VA, g, j)], axis=0)
        o = jnp.dot(jnp.concatenate(ps, axis=0), with_ones(vcat), preferred_element_type=f32)
        for t in range(a_tiles):
            oe = o[t * BLOCK:(t + 1) * BLOCK]
            oo = o[(a_tiles + t) * BLOCK:(a_tiles + t + 1) * BLOCK]
            den = (jnp.where(lo_half, oe[:, LANES:], oo[:, LANES:])
                   + jnp.where(lo_half, es[t], es[a_tiles + t]))
            oa_ref[j * BLOCK:(j + 1) * BLOCK, (a_tiles * g + t) * LANES:(a_tiles * g + t + 1) * LANES] = (
                jnp.where(lo_half, oe[:, :LANES], oo[:, :LANES]) / den)

    a_stages = [(a_scores, a_finish, (g, j)) for last in (False, True) for g in range(A_KV_HEADS)
                for j in range(BLOCKS_PER_CHUNK) if (j == BLOCKS_PER_CHUNK - 1) == last]

    kmetas = [meta_tile(KV_KB + p * LANES) for p in range(B_HEADS // 2)]
    vmetas = [meta_tile(KV_VB + p * LANES) for p in range(B_HEADS // 2)]

    def unit_geometry(step):
        r0 = i * ROWS_PER_CHUNK + 2 * step
        rs0 = jnp.clip(r0 - NA_KH // 2, 0, n_rows - NA_KH)
        rs1 = jnp.clip(r0 + 1 - NA_KH // 2, 0, n_rows - NA_KH)
        shifts = (r0 - rs0, r0 + 1 - rs0)
        variant = 1 - (rs1 - rs0)
        woff = pl.multiple_of((rs0 - i * ROWS_PER_CHUNK) * GRID_W + HALO, BLOCK)
        return shifts, variant, woff

    geometry = [unit_geometry(step) for step in range(ROWS_PER_CHUNK // 2)]

    def b_scores(item):
        step, pairs = item
        _, variant, woff = geometry[step]
        out = []
        for p in pairs:
            kcat = jnp.concatenate(
                [win_ref[pl.ds(woff, B_WIN_KEYS), KV_KB + p * LANES:KV_KB + (p + 1) * LANES],
                 kmetas[p]], axis=0)
            k_aug = jnp.concatenate([kcat, brow_ref[variant]], axis=1)
            qs = split_heads(qb_ref[step * BLOCK:(step + 1) * BLOCK, p * LANES:(p + 1) * LANES])
            q_aug = jnp.concatenate([qs, bsel_ref[...]], axis=1)
            out.append(lax.dot_general(q_aug, k_aug, _NT, preferred_element_type=f32))
        return out

    def b_finish(item, scores):
        step, pairs = item
        shifts, _, woff = geometry[step]
        probs = []
        for p, s in zip(pairs, scores):
            bias = jnp.concatenate(
                [jnp.concatenate([bpair_ref[p, 2 * jj + NA_KH - 1 - shifts[u], hh]
                                  for jj in range(B_WIN_ROWS // 2)], axis=1)
                 for hh in range(2) for u in range(2)], axis=0)
            s_loc = s[:, :B_WIN_KEYS] + bias
            s_met = s[:, B_WIN_KEYS:]
            m = jnp.maximum(jnp.max(s_loc, axis=-1, keepdims=True),
                            jnp.max(s_met, axis=-1, keepdims=True))
            probs.append(jnp.concatenate([jnp.exp(s_loc - m), jnp.exp(s_met - m)],
                                         axis=1).astype(bf16))
        for p, pr in zip(pairs, probs):
            vcat = jnp.concatenate(
                [win_ref[pl.ds(woff, B_WIN_KEYS), KV_VB + p * LANES:KV_VB + (p + 1) * LANES],
                 vmetas[p]], axis=0)
            o = jnp.dot(pr, with_ones(vcat), preferred_element_type=f32)
            o = o[:, :LANES] / o[:, LANES:]
            ob_ref[step * BLOCK:(step + 1) * BLOCK, p * LANES:(p + 1) * LANES] = jnp.where(
                lo_half, o[:BLOCK], o[BLOCK:])

    b_stages = [(b_scores, b_finish, (step, tuple(range(p0, p0 + B_PAIRS_PER_ITEM))))
                for step in range(ROWS_PER_CHUNK // 2)
                for p0 in range(0, B_HEADS // 2, B_PAIRS_PER_ITEM)]
    first_reader = next(t for t, (_, _, (g, j)) in enumerate(a_stages) if j == BLOCKS_PER_CHUNK - 1)
    assert sum(FILL_PLAN[:first_reader - 1]) >= 1 + n_kv_b + 1
    assert len(a_stages) + len(b_stages) == len(FILL_PLAN) and sum(FILL_PLAN) == len(fills)
    pipelined(a_stages + b_stages, FILL_PLAN)


    def gated_branch(o_ref, z_col, w_row, g_col):
        z = proj_ref[:, z_col:z_col + A_WIDTH]
        a = (o_ref[...] * (z * jax.nn.sigmoid(z))).astype(bf16)
        y = jnp.dot(a, wtail_ref[w_row:w_row + A_WIDTH], preferred_element_type=f32)
        return jax.nn.sigmoid(proj_ref[:, g_col:g_col + D_MODEL]) * y

    merged = gated_branch(oa_ref, P_ZA, T_PA, P_GA) + gated_branch(ob_ref, P_ZB, T_PB, P_GB)
    h = x + jnp.dot(merged.astype(bf16), wtail_ref[T_OUT:T_OUT + D_MODEL], preferred_element_type=f32)
    out_ref[0] = _rmsnorm(h, fgain_ref[...])


def _layer_call(sink, x, meta, rope_base, rope_off, gain, fgain, w, w_tail,
                amasks, btables):
    bsz, seq, _ = x.shape
    n_chunks = seq // CHUNK
    n_steps = bsz * n_chunks
    const2 = lambda t: (0, 0)
    const3 = lambda t: (0, 0, 0)
    resident = dict(pipeline_mode=pl.Buffered(1))
    body = functools.partial(_layer_body, n_chunks=n_chunks, n_rows=seq // GRID_W)

    def chunk_of(t):
        return t // n_chunks, t % n_chunks

    def next_chunk(t):
        return chunk_of(jnp.minimum(t + 1, n_steps - 1))

    return pl.pallas_call(
        body,
        grid=(n_steps,),
        in_specs=[
            pl.BlockSpec(memory_space=pltpu.SMEM),
            pl.BlockSpec((1, CHUNK, D_MODEL), lambda t: (*chunk_of(t), 0)),
            pl.BlockSpec((1, CHUNK, D_MODEL), lambda t: (*next_chunk(t), 0)),
            pl.BlockSpec((N_META, D_MODEL), const2, **resident),
            pl.BlockSpec((CHUNK, LANES), const2, **resident),
            pl.BlockSpec((CHUNK, LANES), const2, **resident),
            pl.BlockSpec((None, 2, LANES), lambda t: (chunk_of(t)[1], 0, 0)),
            pl.BlockSpec((None, 2, LANES), lambda t: (next_chunk(t)[1], 0, 0)),
            pl.BlockSpec((1, D_MODEL), const2, **resident),
            pl.BlockSpec((1, D_MODEL), const2, **resident),
            pl.BlockSpec((D_MODEL, IN_COLS), const2, **resident),
            pl.BlockSpec((T_ROWS, D_MODEL), const2, **resident),
            pl.BlockSpec((3, META_PAD + 3 * BLOCK, BLOCK), const3, **resident),
            pl.BlockSpec((A_GROUP * BLOCK, BLOCK), const2, **resident),
            pl.BlockSpec((B_HEADS // 2, 2 * NA_KH, 2, GRID_W, LANES), lambda t: (0, 0, 0, 0, 0),
                         **resident),
            pl.BlockSpec((2, B_WIN_KEYS + META_PAD, LANES), const3, **resident),
            pl.BlockSpec((2 * BLOCK, LANES), const2, **resident),
        ],
        out_specs=pl.BlockSpec((1, CHUNK, D_MODEL), lambda t: (*chunk_of(t), 0)),
        out_shape=jax.ShapeDtypeStruct((bsz, seq, D_MODEL), jnp.float32),
        scratch_shapes=[
            pltpu.VMEM((2, CHUNK, D_MODEL), jnp.bfloat16),
            pltpu.VMEM((3, CHUNK, KV_COLS), jnp.bfloat16),
            pltpu.VMEM((N_META, KV_COLS), jnp.bfloat16),
            pltpu.VMEM((CHUNK + 2 * HALO, KV_COLS), jnp.bfloat16),
            pltpu.VMEM((CHUNK, A_WIDTH), jnp.bfloat16),
            pltpu.VMEM((CHUNK, B_WIDTH), jnp.bfloat16),
            pltpu.VMEM((CHUNK, A_WIDTH), jnp.float32),
            pltpu.VMEM((CHUNK, B_WIDTH), jnp.float32),
            pltpu.VMEM((CHUNK, P_COLS), jnp.float32),
        ],
        compiler_params=pltpu.CompilerParams(
            dimension_semantics=("arbitrary",), vmem_limit_bytes=VMEM_LIMIT_BYTES),
        name="hybrid_layer",
    )(sink, x, x, meta, *rope_base, rope_off, rope_off, gain, fgain, w, w_tail,
      *amasks, *btables)


def _rope_tables(n_chunks):
    d = np.arange(LANES) % HEAD_DIM
    inv_freq = ROPE_THETA ** (-jnp.asarray(d % ROT_HALF, jnp.float32) / ROT_HALF)
    inv_lane = jnp.where(d < ROT_DIM, inv_freq, 0.0)[None, :]
    base = jnp.arange(CHUNK, dtype=jnp.float32)[:, None] * inv_lane
    off = (N_META + CHUNK * jnp.arange(n_chunks, dtype=jnp.float32))[:, None] * inv_lane
    return (jnp.cos(base), jnp.sin(base)), jnp.stack([jnp.cos(off), jnp.sin(off)], axis=1)


def _band_masks():
    qi = np.arange(BLOCK)[:, None]
    col = np.arange(META_PAD + 3 * BLOCK)[None, :]
    kj = col - META_PAD
    band = (col >= META_PAD) & (np.abs(kj - BLOCK - qi) <= WINDOW)
    meta = np.broadcast_to(col < N_META, band.shape)
    variants = [meta | band, meta | (band & (kj >= BLOCK)), meta | (band & (kj < 2 * BLOCK))]
    masks = np.where(np.stack(variants), 0.0, NEG_INF).astype(np.float32).transpose(0, 2, 1)
    eye = np.tile(np.eye(BLOCK, dtype=np.float32), (A_GROUP, 1))
    return jnp.asarray(masks, jnp.bfloat16), jnp.asarray(eye, jnp.bfloat16)


def _na_bias(rpb):
    cq = np.arange(GRID_W)[:, None]
    kc = np.arange(GRID_W)[None, :]
    start = np.clip(cq - NA_KW // 2, 0, GRID_W - NA_KW)
    valid = (kc >= start) & (kc < start + NA_KW)
    n_pairs, n_dr, n_dc = B_HEADS // 2, 2 * NA_KH - 1, 2 * NA_KW - 1
    onehot = ((kc - cq + NA_KW - 1)[..., None] == np.arange(n_dc)) & valid[..., None]
    select = np.zeros((GRID_W, 2, n_dc, 2, GRID_W), np.float32)
    for t in range(2):
        select[:, t, :, t, :] = onehot.transpose(0, 2, 1)
    mask = np.where(np.tile(valid, (1, 2)), 0.0, NEG_INF).astype(np.float32)
    select = np.concatenate([select.reshape(GRID_W, 2 * n_dc, LANES), mask[:, None, :]], axis=1)
    rows = jnp.pad(rpb.reshape(n_pairs, 2, n_dr, n_dc), ((0, 0), (0, 0), (0, 2), (0, 0)))
    rows = jnp.stack([rows[:, :, :2 * NA_KH], rows[:, :, 1:]], axis=3)
    rows = jnp.concatenate([rows.reshape(n_pairs, 2, 2 * NA_KH, 2 * n_dc),
                            jnp.ones((n_pairs, 2, 2 * NA_KH, 1), jnp.float32)], axis=-1)
    pair = jnp.einsum('prhe,cen->prhcn', jnp.transpose(rows, (0, 2, 1, 3)), jnp.asarray(select),
                      precision=lax.Precision.HIGHEST)

    key_row = np.arange(B_WIN_KEYS + META_PAD) // GRID_W
    is_meta = (np.arange(B_WIN_KEYS + META_PAD) >= B_WIN_KEYS) & (
        np.arange(B_WIN_KEYS + META_PAD) < B_WIN_KEYS + N_META)
    rows = np.zeros((2, B_WIN_KEYS + META_PAD, LANES), np.float32)
    for variant, starts in enumerate(((0, 1), (0, 0))):
        for u, s0 in enumerate(starts):
            seen = ((key_row >= s0) & (key_row < s0 + NA_KH)) | is_meta
            rows[variant, :, u] = np.where(seen, 0.0, NEG_INF)
    sel = np.zeros((2, 2, GRID_W, LANES), np.float32)
    sel[:, 0, :, 0] = 1.0
    sel[:, 1, :, 1] = 1.0
    return (pair, jnp.asarray(rows, jnp.bfloat16),
            jnp.asarray(sel.reshape(2 * BLOCK, LANES), jnp.bfloat16))


def kernel(x, meta_tokens, norm_gain, w_in, sink_logits, rel_pos_bias, w_proj_a, w_proj_b, w_out,
           final_norm_gain):
    bsz, seq, _ = x.shape
    assert seq % CHUNK == 0 and seq // GRID_W >= NA_KH and norm_gain.shape[0] == 1
    bf16 = jnp.bfloat16
    w = w_in[0].astype(bf16)
    w_tail = jnp.concatenate([w_proj_a[0], w_proj_b[0], w_out[0]], axis=0).astype(bf16)

    rope_base, rope_off = _rope_tables(seq // CHUNK)
    gain = norm_gain[0][None]
    return _layer_call(sink_logits[0], x, meta_tokens, rope_base, rope_off, gain,
                       final_norm_gain[None], w, w_tail, _band_masks(), _na_bias(rel_pos_bias[0]))
```

```python
import functools

import numpy as np
import jax
import jax.numpy as jnp
from jax import lax
from jax.experimental import pallas as pl
from jax.experimental.pallas import tpu as pltpu

D_MODEL = 1024
N_META = 16
HEAD_DIM = 64
RMS_EPS = 1e-6
NEG_INF = -1e30
A_HEADS = 8
A_KV_HEADS = 2
A_GROUP = A_HEADS // A_KV_HEADS
A_WIDTH = A_HEADS * HEAD_DIM
A_KV_WIDTH = A_KV_HEADS * HEAD_DIM
WINDOW = 128
BLOCK = 128
ROT_DIM = HEAD_DIM // 4
ROT_HALF = ROT_DIM // 2
ROPE_THETA = 500000.0
B_HEADS = 8
B_WIDTH = B_HEADS * HEAD_DIM
GRID_W = 64
NA_KH = 8
NA_KW = 16
SPLIT_SIZES = (A_WIDTH, A_KV_WIDTH, A_KV_WIDTH, A_WIDTH, B_WIDTH, B_WIDTH, B_WIDTH, B_WIDTH,
               D_MODEL, D_MODEL)

LANES = 128
CHUNK = 512
ROWS_PER_CHUNK = CHUNK // GRID_W
BLOCKS_PER_CHUNK = CHUNK // BLOCK
HALO = CHUNK // 2
assert HALO >= BLOCK and HALO >= (NA_KH // 2) * GRID_W
META_PAD = LANES
B_WIN_ROWS = NA_KH + 2
B_WIN_KEYS = B_WIN_ROWS * GRID_W
B_PAIRS_PER_ITEM = 2
PROJ_PIECE = 256
FILL_PLAN = (2, 2, 1, 1, 2, 1, 1, 1, 1, 1, 1, 1, 1, 1, 1, 1)

KV_KA = 0
KV_VA = KV_KA + A_KV_HEADS * LANES
KV_KB = KV_VA + A_KV_HEADS * LANES
KV_VB = KV_KB + B_WIDTH
KV_COLS = KV_VB + B_WIDTH

(C_QA, C_KA, C_VA, C_ZA, C_QB, C_KB, C_VB, C_ZB, C_GA, C_GB, IN_COLS) = (
    int(c) for c in np.cumsum((0,) + SPLIT_SIZES))
assert A_KV_WIDTH == LANES and C_VA == C_KA + A_KV_WIDTH and C_VB == C_KB + B_WIDTH

P_ZA = 0
P_ZB = P_ZA + A_WIDTH
P_GA = P_ZB + B_WIDTH
P_GB = P_GA + D_MODEL
P_COLS = P_GB + D_MODEL
PROJ_SOURCES = tuple(
    src + off for src, width in ((C_ZA, A_WIDTH), (C_ZB, B_WIDTH), (C_GA, D_MODEL), (C_GB, D_MODEL))
    for off in range(0, width, PROJ_PIECE))

T_PA = 0
T_PB = T_PA + A_WIDTH
T_OUT = T_PB + B_WIDTH
T_ROWS = T_OUT + D_MODEL

VMEM_LIMIT_BYTES = 58 * 1024 * 1024

_NT = (((1,), (1,)), ((), ()))


def _rmsnorm(x, gain):
    return x * lax.rsqrt(jnp.mean(x * x, axis=-1, keepdims=True) + RMS_EPS) * gain


def _lane_patterns():
    lane = lax.broadcasted_iota(jnp.int32, (1, LANES), 1)
    return lane < HEAD_DIM, (lane & (HEAD_DIM - 1)) < ROT_HALF


def _rope_coeffs(cb_ref, sb_ref, off_ref, first_half):
    cb, sb = cb_ref[...], sb_ref[...]
    ca, sa = off_ref[0:1, :], off_ref[1:2, :]
    c = cb * ca - sb * sa
    s = (sb * ca + cb * sa) * jnp.where(first_half, -1.0, 1.0)
    return c, s


def _rope_tile(t, c, s, first_half):
    partner = jnp.where(first_half, pltpu.roll(t, LANES - ROT_HALF, 1), pltpu.roll(t, ROT_HALF, 1))
    return t * c + partner * s


def _layer_body(sink_ref, x_ref, xn_ref, meta_ref, cb_ref, sb_ref, off_ref, offn_ref,
                gain_ref, fgain_ref, wr_ref, wtail_ref, amask_ref, eye_ref, bpair_ref,
                brow_ref, bsel_ref,
                out_ref, n_ref, ring_ref, kvm_ref, win_ref, qa_ref, qb_ref, oa_ref, ob_ref, proj_ref,
                *, n_chunks, n_rows):
    step = pl.program_id(0)
    i = step % n_chunks
    cur, nxt = step % 2, (step + 1) % 2
    slot_prev, slot_cur, slot_nxt = (step + 2) % 3, step % 3, (step + 1) % 3
    f32, bf16 = jnp.float32, jnp.bfloat16
    scale = HEAD_DIM ** -0.5
    lo_half, first_half = _lane_patterns()

    def kv_a_piece(n, c, s, store):
        kv_a = jnp.dot(n, wr_ref[:, C_KA:C_KA + 2 * A_KV_WIDTH], preferred_element_type=f32)
        ka = _rope_tile(kv_a[:, :LANES], c, s, first_half)
        va = kv_a[:, LANES:]
        for src, col in ((ka, KV_KA), (va, KV_VA)):
            swapped = pltpu.roll(src, HEAD_DIM, 1)
            store(col, jnp.where(lo_half, src, swapped).astype(bf16))
            store(col + LANES, jnp.where(lo_half, swapped, src).astype(bf16))

    def kv_b_piece(n, k, store):
        src = C_KB + k * PROJ_PIECE
        store(KV_KB + k * PROJ_PIECE,
              jnp.dot(n, wr_ref[:, src:src + PROJ_PIECE], preferred_element_type=f32).astype(bf16))

    n_kv_b = 2 * B_WIDTH // PROJ_PIECE

    def ring_store(slot):
        def store(col, value):
            ring_ref[slot, :, col:col + value.shape[1]] = value
        return store

    def meta_store(col, value):
        kvm_ref[:, col:col + value.shape[1]] = value

    @pl.when(step == 0)
    def _():
        ring_ref[...] = jnp.zeros(ring_ref.shape, bf16)
        n_ref[0] = _rmsnorm(x_ref[0], gain_ref[...]).astype(bf16)
        c0, s0 = _rope_coeffs(cb_ref, sb_ref, off_ref, first_half)
        kv_a_piece(n_ref[0], c0, s0, ring_store(0))
        for k in range(n_kv_b):
            kv_b_piece(n_ref[0], k, ring_store(0))
        n_meta = _rmsnorm(meta_ref[...], gain_ref[...]).astype(bf16)
        sign = jnp.where(first_half, -1.0, 1.0)
        kv_a_piece(n_meta, cb_ref[0:N_META], sb_ref[0:N_META] * sign, meta_store)
        for k in range(n_kv_b):
            kv_b_piece(n_meta, k, meta_store)

    x = x_ref[0]

    def normalise_next():
        n_ref[nxt] = _rmsnorm(xn_ref[0], gain_ref[...]).astype(bf16)

    win_ref[0:HALO] = ring_ref[slot_prev, CHUNK - HALO:CHUNK]
    win_ref[HALO:HALO + CHUNK] = ring_ref[slot_cur]

    def next_halo():
        win_ref[HALO + CHUNK:CHUNK + 2 * HALO] = ring_ref[slot_nxt, 0:HALO]

    c, s = _rope_coeffs(cb_ref, sb_ref, off_ref, first_half)
    qa = jnp.dot(n_ref[cur], wr_ref[:, C_QA:C_QA + A_WIDTH], preferred_element_type=f32)
    for t in range(A_WIDTH // LANES):
        sl = slice(t * LANES, (t + 1) * LANES)
        qa_ref[:, sl] = (_rope_tile(qa[:, sl], c, s, first_half) * scale).astype(bf16)
    qb = jnp.dot(n_ref[cur], wr_ref[:, C_QB:C_QB + B_WIDTH], preferred_element_type=f32)
    qb_ref[...] = (qb * scale).astype(bf16)

    zero_pad = jnp.zeros((META_PAD - N_META, LANES), bf16)

    def meta_tile(col):
        return jnp.concatenate([kvm_ref[:, col:col + LANES], zero_pad], axis=0)

    def split_heads(q):
        zeros = jnp.zeros_like(q)
        return jnp.concatenate([jnp.where(lo_half, q, zeros), jnp.where(lo_half, zeros, q)], axis=0)

    def with_ones(v):
        return jnp.concatenate([v, jnp.ones_like(v)], axis=1)

    def proj_piece(k):
        src = PROJ_SOURCES[k]
        proj_ref[:, k * PROJ_PIECE:(k + 1) * PROJ_PIECE] = jnp.dot(
            n_ref[cur], wr_ref[:, src:src + PROJ_PIECE], preferred_element_type=f32)

    def next_kv_a():
        cn, sn = _rope_coeffs(cb_ref, sb_ref, offn_ref, first_half)
        kv_a_piece(n_ref[nxt], cn, sn, ring_store(slot_nxt))

    def next_kv_b(k):
        kv_b_piece(n_ref[nxt], k, ring_store(slot_nxt))

    fills = ([normalise_next, next_kv_a] + [functools.partial(next_kv_b, k) for k in range(n_kv_b)]
             + [next_halo]
             + [functools.partial(proj_piece, k) for k in range(len(PROJ_SOURCES))])

    def pipelined(stages, fill_plan):
        pending = stages[0][0](stages[0][2])
        for t, (_, finish, item) in enumerate(stages):
            nxt_scores = stages[t + 1][0](stages[t + 1][2]) if t + 1 < len(stages) else None
            for _ in range(fill_plan[t]):
                fills.pop(0)()
            finish(item, pending)
            pending = nxt_scores

    a_tiles = A_GROUP // 2
    kmetas_a = [meta_tile(KV_KA + g * LANES) for g in range(A_KV_HEADS)]
    vmetas_a = [meta_tile(KV_VA + g * LANES) for g in range(A_KV_HEADS)]

    def a_window(col, tile, j):
        w0 = HALO + j * BLOCK - BLOCK
        return win_ref[w0:w0 + 3 * BLOCK, col + tile * LANES:col + (tile + 1) * LANES]

    def a_scores(item):
        g, j = item
        first = jnp.logical_and(i == 0, j == 0)
        last = jnp.logical_and(i == n_chunks - 1, j == BLOCKS_PER_CHUNK - 1)
        mask_t = amask_ref[jnp.where(first, 1, jnp.where(last, 2, 0))]
        kcat = jnp.concatenate([kmetas_a[g], a_window(KV_KA, g, j)], axis=0)
        k_aug = jnp.concatenate([kcat, mask_t], axis=1)
        q_tiles = [qa_ref[j * BLOCK:(j + 1) * BLOCK, (a_tiles * g + t) * LANES:(a_tiles * g + t + 1) * LANES]
                   for t in range(a_tiles)]
        zeros = jnp.zeros_like(q_tiles[0])
        qs = jnp.concatenate([jnp.where(lo_half, q, zeros) for q in q_tiles]
                             + [jnp.where(lo_half, zeros, q) for q in q_tiles], axis=0)
        q_aug = jnp.concatenate([qs, eye_ref[...]], axis=1)
        return lax.dot_general(q_aug, k_aug, _NT, preferred_element_type=f32)

    def a_finish(item, s):
        g, j = item
        ps, es = [], []
        for rb in range(A_GROUP):
            sink = sink_ref[A_GROUP * g + 2 * (rb % a_tiles) + rb // a_tiles]
            sh = s[rb * BLOCK:(rb + 1) * BLOCK]
            m = jnp.maximum(jnp.max(sh, axis=-1, keepdims=True), sink)
            ps.append(jnp.exp(sh - m).astype(bf16))
            es.append(jnp.exp(sink - m))
        vcat = jnp.concatenate([vmetas_a[g], a_window(KV_VA, g, j)], axis=0)
        o = jnp.dot(jnp.concatenate(ps, axis=0), with_ones(vcat), preferred_element_type=f32)
        for t in range(a_tiles):
            oe = o[t * BLOCK:(t + 1) * BLOCK]
            oo = o[(a_tiles + t) * BLOCK:(a_tiles + t + 1) * BLOCK]
            den = (jnp.where(lo_half, oe[:, LANES:], oo[:, LANES:])
                   + jnp.where(lo_half, es[t], es[a_tiles + t]))
            oa_ref[j * BLOCK:(j + 1) * BLOCK, (a_tiles * g + t) * LANES:(a_tiles * g + t + 1) * LANES] = (
                jnp.where(lo_half, oe[:, :LANES], oo[:, :LANES]) / den)

    a_stages = [(a_scores, a_finish, (g, j)) for last in (False, True) for g in range(A_KV_HEADS)
                for j in range(BLOCKS_PER_CHUNK) if (j == BLOCKS_PER_CHUNK - 1) == last]

    kmetas = [meta_tile(KV_KB + p * LANES) for p in range(B_HEADS // 2)]
    vmetas = [meta_tile(KV_VB + p * LANES) for p in range(B_HEADS // 2)]

    def unit_geometry(step):
        r0 = i * ROWS_PER_CHUNK + 2 * step
        rs0 = jnp.clip(r0 - NA_KH // 2, 0, n_rows - NA_KH)
        rs1 = jnp.clip(r0 + 1 - NA_KH // 2, 0, n_rows - NA_KH)
        shifts = (r0 - rs0, r0 + 1 - rs0)
        variant = 1 - (rs1 - rs0)
        woff = pl.multiple_of((rs0 - i * ROWS_PER_CHUNK) * GRID_W + HALO, BLOCK)
        return shifts, variant, woff

    geometry = [unit_geometry(step) for step in range(ROWS_PER_CHUNK // 2)]

    def b_scores(item):
        step, pairs = item
        _, variant, woff = geometry[step]
        out = []
        for p in pairs:
            kcat = jnp.concatenate(
                [win_ref[pl.ds(woff, B_WIN_KEYS), KV_KB + p * LANES:KV_KB + (p + 1) * LANES],
                 kmetas[p]], axis=0)
            k_aug = jnp.concatenate([kcat, brow_ref[variant]], axis=1)
            qs = split_heads(qb_ref[step * BLOCK:(step + 1) * BLOCK, p * LANES:(p + 1) * LANES])
            q_aug = jnp.concatenate([qs, bsel_ref[...]], axis=1)
            out.append(lax.dot_general(q_aug, k_aug, _NT, preferred_element_type=f32))
        return out

    def b_finish(item, scores):
        step, pairs = item
        shifts, _, woff = geometry[step]
        probs = []
        for p, s in zip(pairs, scores):
            bias = jnp.concatenate(
                [jnp.concatenate([bpair_ref[p, 2 * jj + NA_KH - 1 - shifts[u], hh]
                                  for jj in range(B_WIN_ROWS // 2)], axis=1)
                 for hh in range(2) for u in range(2)], axis=0)
            s_loc = s[:, :B_WIN_KEYS] + bias
            s_met = s[:, B_WIN_KEYS:]
            m = jnp.maximum(jnp.max(s_loc, axis=-1, keepdims=True),
                            jnp.max(s_met, axis=-1, keepdims=True))
            probs.append(jnp.concatenate([jnp.exp(s_loc - m), jnp.exp(s_met - m)],
                                         axis=1).astype(bf16))
        for p, pr in zip(pairs, probs):
            vcat = jnp.concatenate(
                [win_ref[pl.ds(woff, B_WIN_KEYS), KV_VB + p * LANES:KV_VB + (p + 1) * LANES],
                 vmetas[p]], axis=0)
            o = jnp.dot(pr, with_ones(vcat), preferred_element_type=f32)
            o = o[:, :LANES] / o[:, LANES:]
            ob_ref[step * BLOCK:(step + 1) * BLOCK, p * LANES:(p + 1) * LANES] = jnp.where(
                lo_half, o[:BLOCK], o[BLOCK:])

    b_stages = [(b_scores, b_finish, (step, tuple(range(p0, p0 + B_PAIRS_PER_ITEM))))
                for step in range(ROWS_PER_CHUNK // 2)
                for p0 in range(0, B_HEADS // 2, B_PAIRS_PER_ITEM)]
    first_reader = next(t for t, (_, _, (g, j)) in enumerate(a_stages) if j == BLOCKS_PER_CHUNK - 1)
    assert sum(FILL_PLAN[:first_reader - 1]) >= 2 + n_kv_b + 1
    assert len(a_stages) + len(b_stages) == len(FILL_PLAN) and sum(FILL_PLAN) == len(fills)
    pipelined(a_stages + b_stages, FILL_PLAN)


    def gated_branch(o_ref, z_col, w_row, g_col):
        z = proj_ref[:, z_col:z_col + A_WIDTH]
        a = (o_ref[...] * (z * jax.nn.sigmoid(z))).astype(bf16)
        y = jnp.dot(a, wtail_ref[w_row:w_row + A_WIDTH], preferred_element_type=f32)
        return jax.nn.sigmoid(proj_ref[:, g_col:g_col + D_MODEL]) * y

    merged = gated_branch(oa_ref, P_ZA, T_PA, P_GA) + gated_branch(ob_ref, P_ZB, T_PB, P_GB)
    h = x + jnp.dot(merged.astype(bf16), wtail_ref[T_OUT:T_OUT + D_MODEL], preferred_element_type=f32)
    out_ref[0] = _rmsnorm(h, fgain_ref[...])


def _layer_call(sink, x, meta, rope_base, rope_off, gain, fgain, w, w_tail,
                amasks, btables):
    bsz, seq, _ = x.shape
    n_chunks = seq // CHUNK
    n_steps = bsz * n_chunks
    const2 = lambda t: (0, 0)
    const3 = lambda t: (0, 0, 0)
    resident = dict(pipeline_mode=pl.Buffered(1))
    body = functools.partial(_layer_body, n_chunks=n_chunks, n_rows=seq // GRID_W)

    def chunk_of(t):
        return t // n_chunks, t % n_chunks

    def next_chunk(t):
        return chunk_of(jnp.minimum(t + 1, n_steps - 1))

    return pl.pallas_call(
        body,
        grid=(n_steps,),
        in_specs=[
            pl.BlockSpec(memory_space=pltpu.SMEM),
            pl.BlockSpec((1, CHUNK, D_MODEL), lambda t: (*chunk_of(t), 0)),
            pl.BlockSpec((1, CHUNK, D_MODEL), lambda t: (*next_chunk(t), 0)),
            pl.BlockSpec((N_META, D_MODEL), const2, **resident),
            pl.BlockSpec((CHUNK, LANES), const2, **resident),
            pl.BlockSpec((CHUNK, LANES), const2, **resident),
            pl.BlockSpec((None, 2, LANES), lambda t: (chunk_of(t)[1], 0, 0)),
            pl.BlockSpec((None, 2, LANES), lambda t: (next_chunk(t)[1], 0, 0)),
            pl.BlockSpec((1, D_MODEL), const2, **resident),
            pl.BlockSpec((1, D_MODEL), const2, **resident),
            pl.BlockSpec((D_MODEL, IN_COLS), const2, **resident),
            pl.BlockSpec((T_ROWS, D_MODEL), const2, **resident),
            pl.BlockSpec((3, META_PAD + 3 * BLOCK, BLOCK), const3, **resident),
            pl.BlockSpec((A_GROUP * BLOCK, BLOCK), const2, **resident),
            pl.BlockSpec((B_HEADS // 2, 2 * NA_KH, 2, GRID_W, LANES), lambda t: (0, 0, 0, 0, 0),
                         **resident),
            pl.BlockSpec((2, B_WIN_KEYS + META_PAD, LANES), const3, **resident),
            pl.BlockSpec((2 * BLOCK, LANES), const2, **resident),
        ],
        out_specs=pl.BlockSpec((1, CHUNK, D_MODEL), lambda t: (*chunk_of(t), 0)),
        out_shape=jax.ShapeDtypeStruct((bsz, seq, D_MODEL), jnp.float32),
        scratch_shapes=[
            pltpu.VMEM((2, CHUNK, D_MODEL), jnp.bfloat16),
            pltpu.VMEM((3, CHUNK, KV_COLS), jnp.bfloat16),
            pltpu.VMEM((N_META, KV_COLS), jnp.bfloat16),
            pltpu.VMEM((CHUNK + 2 * HALO, KV_COLS), jnp.bfloat16),
            pltpu.VMEM((CHUNK, A_WIDTH), jnp.bfloat16),
            pltpu.VMEM((CHUNK, B_WIDTH), jnp.bfloat16),
            pltpu.VMEM((CHUNK, A_WIDTH), jnp.float32),
            pltpu.VMEM((CHUNK, B_WIDTH), jnp.float32),
            pltpu.VMEM((CHUNK, P_COLS), jnp.float32),
        ],
        compiler_params=pltpu.CompilerParams(
            dimension_semantics=("arbitrary",), vmem_limit_bytes=VMEM_LIMIT_BYTES),
        name="hybrid_layer",
    )(sink, x, x, meta, *rope_base, rope_off, rope_off, gain, fgain, w, w_tail,
      *amasks, *btables)


def _rope_tables(n_chunks):
    d = np.arange(LANES) % HEAD_DIM
    inv_freq = ROPE_THETA ** (-jnp.asarray(d % ROT_HALF, jnp.float32) / ROT_HALF)
    inv_lane = jnp.where(d < ROT_DIM, inv_freq, 0.0)[None, :]
    base = jnp.arange(CHUNK, dtype=jnp.float32)[:, None] * inv_lane
    off = (N_META + CHUNK * jnp.arange(n_chunks, dtype=jnp.float32))[:, None] * inv_lane
    return (jnp.cos(base), jnp.sin(base)), jnp.stack([jnp.cos(off), jnp.sin(off)], axis=1)


def _band_masks():
    qi = np.arange(BLOCK)[:, None]
    col = np.arange(META_PAD + 3 * BLOCK)[None, :]
    kj = col - META_PAD
    band = (col >= META_PAD) & (np.abs(kj - BLOCK - qi) <= WINDOW)
    meta = np.broadcast_to(col < N_META, band.shape)
    variants = [meta | band, meta | (band & (kj >= BLOCK)), meta | (band & (kj < 2 * BLOCK))]
    masks = np.where(np.stack(variants), 0.0, NEG_INF).astype(np.float32).transpose(0, 2, 1)
    eye = np.tile(np.eye(BLOCK, dtype=np.float32), (A_GROUP, 1))
    return jnp.asarray(masks, jnp.bfloat16), jnp.asarray(eye, jnp.bfloat16)


def _na_bias(rpb):
    cq = np.arange(GRID_W)[:, None]
    kc = np.arange(GRID_W)[None, :]
    start = np.clip(cq - NA_KW // 2, 0, GRID_W - NA_KW)
    valid = (kc >= start) & (kc < start + NA_KW)
    n_pairs, n_dr, n_dc = B_HEADS // 2, 2 * NA_KH - 1, 2 * NA_KW - 1
    onehot = ((kc - cq + NA_KW - 1)[..., None] == np.arange(n_dc)) & valid[..., None]
    select = np.zeros((GRID_W, 2, n_dc, 2, GRID_W), np.float32)
    for t in range(2):
        select[:, t, :, t, :] = onehot.transpose(0, 2, 1)
    mask = np.where(np.tile(valid, (1, 2)), 0.0, NEG_INF).astype(np.float32)
    select = np.concatenate([select.reshape(GRID_W, 2 * n_dc, LANES), mask[:, None, :]], axis=1)
    rows = jnp.pad(rpb.reshape(n_pairs, 2, n_dr, n_dc), ((0, 0), (0, 0), (0, 2), (0, 0)))
    rows = jnp.stack([rows[:, :, :2 * NA_KH], rows[:, :, 1:]], axis=3)
    rows = jnp.concatenate([rows.reshape(n_pairs, 2, 2 * NA_KH, 2 * n_dc),
                            jnp.ones((n_pairs, 2, 2 * NA_KH, 1), jnp.float32)], axis=-1)
    pair = jnp.einsum('prhe,cen->prhcn', jnp.transpose(rows, (0, 2, 1, 3)), jnp.asarray(select),
                      precision=lax.Precision.HIGHEST)

    key_row = np.arange(B_WIN_KEYS + META_PAD) // GRID_W
    is_meta = (np.arange(B_WIN_KEYS + META_PAD) >= B_WIN_KEYS) & (
        np.arange(B_WIN_KEYS + META_PAD) < B_WIN_KEYS + N_META)
    rows = np.zeros((2, B_WIN_KEYS + META_PAD, LANES), np.float32)
    for variant, starts in enumerate(((0, 1), (0, 0))):
        for u, s0 in enumerate(starts):
            seen = ((key_row >= s0) & (key_row < s0 + NA_KH)) | is_meta
            rows[variant, :, u] = np.where(seen, 0.0, NEG_INF)
    sel = np.zeros((2, 2, GRID_W, LANES), np.float32)
    sel[:, 0, :, 0] = 1.0
    sel[:, 1, :, 1] = 1.0
    return (pair, jnp.asarray(rows, jnp.bfloat16),
            jnp.asarray(sel.reshape(2 * BLOCK, LANES), jnp.bfloat16))


def kernel(x, meta_tokens, norm_gain, w_in, sink_logits, rel_pos_bias, w_proj_a, w_proj_b, w_out,
           final_norm_gain):
    bsz, seq, _ = x.shape
    assert seq % CHUNK == 0 and seq // GRID_W >= NA_KH and norm_gain.shape[0] == 1
    bf16 = jnp.bfloat16
    w = w_in[0].astype(bf16)
    w_tail = jnp.concatenate([w_proj_a[0], w_proj_b[0], w_out[0]], axis=0).astype(bf16)

    rope_base, rope_off = _rope_tables(seq // CHUNK)
    gain = norm_gain[0][None]
    return _layer_call(sink_logits[0], x, meta_tokens, rope_base, rope_off, gain,
                       final_norm_gain[None], w, w_tail, _band_masks(), _na_bias(rel_pos_bias[0]))
```

```python
import functools

import numpy as np
import jax
import jax.numpy as jnp
from jax import lax
from jax.experimental import pallas as pl
from jax.experimental.pallas import tpu as pltpu

D_MODEL = 1024
N_META = 16
HEAD_DIM = 64
RMS_EPS = 1e-6
NEG_INF = -1e30
A_HEADS = 8
A_KV_HEADS = 2
A_GROUP = A_HEADS // A_KV_HEADS
A_WIDTH = A_HEADS * HEAD_DIM
A_KV_WIDTH = A_KV_HEADS * HEAD_DIM
WINDOW = 128
BLOCK = 128
ROT_DIM = HEAD_DIM // 4
ROT_HALF = ROT_DIM // 2
ROPE_THETA = 500000.0
B_HEADS = 8
B_WIDTH = B_HEADS * HEAD_DIM
GRID_W = 64
NA_KH = 8
NA_KW = 16
SPLIT_SIZES = (A_WIDTH, A_KV_WIDTH, A_KV_WIDTH, A_WIDTH, B_WIDTH, B_WIDTH, B_WIDTH, B_WIDTH,
               D_MODEL, D_MODEL)

LANES = 128
CHUNK = 512
ROWS_PER_CHUNK = CHUNK // GRID_W
BLOCKS_PER_CHUNK = CHUNK // BLOCK
HALO = CHUNK // 2
assert HALO >= BLOCK and HALO >= (NA_KH // 2) * GRID_W
META_PAD = LANES
B_WIN_ROWS = NA_KH + 2
B_WIN_KEYS = B_WIN_ROWS * GRID_W
B_PAIRS_PER_ITEM = 2
PROJ_PIECE = 256
FILL_PLAN = (2, 1, 1, 1, 2, 1, 1, 1, 1, 1, 1, 1, 1, 1, 1, 1)

KV_KA = 0
KV_VA = KV_KA + A_KV_HEADS * LANES
KV_KB = KV_VA + A_KV_HEADS * LANES
KV_VB = KV_KB + B_WIDTH
KV_COLS = KV_VB + B_WIDTH

(C_QA, C_KA, C_VA, C_ZA, C_QB, C_KB, C_VB, C_ZB, C_GA, C_GB, IN_COLS) = (
    int(c) for c in np.cumsum((0,) + SPLIT_SIZES))
assert A_KV_WIDTH == LANES and C_VA == C_KA + A_KV_WIDTH and C_VB == C_KB + B_WIDTH

P_ZA = 0
P_ZB = P_ZA + A_WIDTH
P_GA = P_ZB + B_WIDTH
P_GB = P_GA + D_MODEL
P_COLS = P_GB + D_MODEL
PROJ_SOURCES = tuple(
    src + off for src, width in ((C_ZA, A_WIDTH), (C_ZB, B_WIDTH), (C_GA, D_MODEL), (C_GB, D_MODEL))
    for off in range(0, width, PROJ_PIECE))

T_PA = 0
T_PB = T_PA + A_WIDTH
T_OUT = T_PB + B_WIDTH
T_ROWS = T_OUT + D_MODEL

VMEM_LIMIT_BYTES = 58 * 1024 * 1024

_NT = (((1,), (1,)), ((), ()))


def _rmsnorm(x, gain):
    return x * lax.rsqrt(jnp.mean(x * x, axis=-1, keepdims=True) + RMS_EPS) * gain


def _lane_patterns():
    lane = lax.broadcasted_iota(jnp.int32, (1, LANES), 1)
    return lane < HEAD_DIM, (lane & (HEAD_DIM - 1)) < ROT_HALF


def _rope_coeffs(cb_ref, sb_ref, off_ref, first_half):
    cb, sb = cb_ref[...], sb_ref[...]
    ca, sa = off_ref[0:1, :], off_ref[1:2, :]
    c = cb * ca - sb * sa
    s = (sb * ca + cb * sa) * jnp.where(first_half, -1.0, 1.0)
    return c, s


def _rope_tile(t, c, s, first_half):
    partner = jnp.where(first_half, pltpu.roll(t, LANES - ROT_HALF, 1), pltpu.roll(t, ROT_HALF, 1))
    return t * c + partner * s


def _layer_body(sink_ref, x_ref, xn_ref, meta_ref, cb_ref, sb_ref, off_ref, offn_ref,
                gain_ref, fgain_ref, wr_ref, wtail_hbm, amask_ref, eye_ref, bpair_hbm,
                brow_ref, bsel_ref,
                out_ref, n_ref, ring_ref, kvm_ref, win_ref, qa_ref, qb_ref, oa_ref, ob_ref, proj_ref,
                wtail_ref, bpair_ref, late_sem, *, n_chunks, n_rows):
    step = pl.program_id(0)
    i = step % n_chunks
    cur, nxt = step % 2, (step + 1) % 2
    slot_prev, slot_cur, slot_nxt = (step + 2) % 3, step % 3, (step + 1) % 3
    f32, bf16 = jnp.float32, jnp.bfloat16
    scale = HEAD_DIM ** -0.5
    lo_half, first_half = _lane_patterns()

    def kv_a_piece(n, c, s, store):
        kv_a = jnp.dot(n, wr_ref[:, C_KA:C_KA + 2 * A_KV_WIDTH], preferred_element_type=f32)
        ka = _rope_tile(kv_a[:, :LANES], c, s, first_half)
        va = kv_a[:, LANES:]
        for src, col in ((ka, KV_KA), (va, KV_VA)):
            swapped = pltpu.roll(src, HEAD_DIM, 1)
            store(col, jnp.where(lo_half, src, swapped).astype(bf16))
            store(col + LANES, jnp.where(lo_half, swapped, src).astype(bf16))

    def kv_b_piece(n, k, store):
        src = C_KB + k * PROJ_PIECE
        store(KV_KB + k * PROJ_PIECE,
              jnp.dot(n, wr_ref[:, src:src + PROJ_PIECE], preferred_element_type=f32).astype(bf16))

    n_kv_b = 2 * B_WIDTH // PROJ_PIECE

    def ring_store(slot):
        def store(col, value):
            ring_ref[slot, :, col:col + value.shape[1]] = value
        return store

    def meta_store(col, value):
        kvm_ref[:, col:col + value.shape[1]] = value

    @pl.when(step == 0)
    def _():
        late_copies = [pltpu.make_async_copy(wtail_hbm, wtail_ref, late_sem.at[0]),
                       pltpu.make_async_copy(bpair_hbm, bpair_ref, late_sem.at[1])]
        for copy in late_copies:
            copy.start()
        ring_ref[...] = jnp.zeros(ring_ref.shape, bf16)
        n_ref[0] = _rmsnorm(x_ref[0], gain_ref[...]).astype(bf16)
        c0, s0 = _rope_coeffs(cb_ref, sb_ref, off_ref, first_half)
        kv_a_piece(n_ref[0], c0, s0, ring_store(0))
        for k in range(n_kv_b):
            kv_b_piece(n_ref[0], k, ring_store(0))
        n_meta = _rmsnorm(meta_ref[...], gain_ref[...]).astype(bf16)
        sign = jnp.where(first_half, -1.0, 1.0)
        kv_a_piece(n_meta, cb_ref[0:N_META], sb_ref[0:N_META] * sign, meta_store)
        for k in range(n_kv_b):
            kv_b_piece(n_meta, k, meta_store)
        for copy in late_copies:
            copy.wait()

    x = x_ref[0]
    n_ref[nxt] = _rmsnorm(xn_ref[0], gain_ref[...]).astype(bf16)

    win_ref[0:HALO] = ring_ref[slot_prev, CHUNK - HALO:CHUNK]
    win_ref[HALO:HALO + CHUNK] = ring_ref[slot_cur]

    def next_halo():
        win_ref[HALO + CHUNK:CHUNK + 2 * HALO] = ring_ref[slot_nxt, 0:HALO]

    c, s = _rope_coeffs(cb_ref, sb_ref, off_ref, first_half)
    qa = jnp.dot(n_ref[cur], wr_ref[:, C_QA:C_QA + A_WIDTH], preferred_element_type=f32)
    for t in range(A_WIDTH // LANES):
        sl = slice(t * LANES, (t + 1) * LANES)
        qa_ref[:, sl] = (_rope_tile(qa[:, sl], c, s, first_half) * scale).astype(bf16)
    qb = jnp.dot(n_ref[cur], wr_ref[:, C_QB:C_QB + B_WIDTH], preferred_element_type=f32)
    qb_ref[...] = (qb * scale).astype(bf16)

    zero_pad = jnp.zeros((META_PAD - N_META, LANES), bf16)

    def meta_tile(col):
        return jnp.concatenate([kvm_ref[:, col:col + LANES], zero_pad], axis=0)

    def split_heads(q):
        zeros = jnp.zeros_like(q)
        return jnp.concatenate([jnp.where(lo_half, q, zeros), jnp.where(lo_half, zeros, q)], axis=0)

    def with_ones(v):
        return jnp.concatenate([v, jnp.ones_like(v)], axis=1)

    def proj_piece(k):
        src = PROJ_SOURCES[k]
        proj_ref[:, k * PROJ_PIECE:(k + 1) * PROJ_PIECE] = jnp.dot(
            n_ref[cur], wr_ref[:, src:src + PROJ_PIECE], preferred_element_type=f32)

    def next_kv_a():
        cn, sn = _rope_coeffs(cb_ref, sb_ref, offn_ref, first_half)
        kv_a_piece(n_ref[nxt], cn, sn, ring_store(slot_nxt))

    def next_kv_b(k):
        kv_b_piece(n_ref[nxt], k, ring_store(slot_nxt))

    fills = ([next_kv_a] + [functools.partial(next_kv_b, k) for k in range(n_kv_b)] + [next_halo]
             + [functools.partial(proj_piece, k) for k in range(len(PROJ_SOURCES))])

    def pipelined(stages, fill_plan):
        pending = stages[0][0](stages[0][2])
        for t, (_, finish, item) in enumerate(stages):
            nxt_scores = stages[t + 1][0](stages[t + 1][2]) if t + 1 < len(stages) else None
            for _ in range(fill_plan[t]):
                fills.pop(0)()
            finish(item, pending)
            pending = nxt_scores

    a_tiles = A_GROUP // 2
    kmetas_a = [meta_tile(KV_KA + g * LANES) for g in range(A_KV_HEADS)]
    vmetas_a = [meta_tile(KV_VA + g * LANES) for g in range(A_KV_HEADS)]

    def a_window(col, tile, j):
        w0 = HALO + j * BLOCK - BLOCK
        return win_ref[w0:w0 + 3 * BLOCK, col + tile * LANES:col + (tile + 1) * LANES]

    def a_scores(item):
        g, j = item
        first = jnp.logical_and(i == 0, j == 0)
        last = jnp.logical_and(i == n_chunks - 1, j == BLOCKS_PER_CHUNK - 1)
        mask_t = amask_ref[jnp.where(first, 1, jnp.where(last, 2, 0))]
        kcat = jnp.concatenate([kmetas_a[g], a_window(KV_KA, g, j)], axis=0)
        k_aug = jnp.concatenate([kcat, mask_t], axis=1)
        q_tiles = [qa_ref[j * BLOCK:(j + 1) * BLOCK, (a_tiles * g + t) * LANES:(a_tiles * g + t + 1) * LANES]
                   for t in range(a_tiles)]
        zeros = jnp.zeros_like(q_tiles[0])
        qs = jnp.concatenate([jnp.where(lo_half, q, zeros) for q in q_tiles]
                             + [jnp.where(lo_half, zeros, q) for q in q_tiles], axis=0)
        q_aug = jnp.concatenate([qs, eye_ref[...]], axis=1)
        return lax.dot_general(q_aug, k_aug, _NT, preferred_element_type=f32)

    def a_finish(item, s):
        g, j = item
        ps, es = [], []
        for rb in range(A_GROUP):
            sink = sink_ref[A_GROUP * g + 2 * (rb % a_tiles) + rb // a_tiles]
            sh = s[rb * BLOCK:(rb + 1) * BLOCK]
            m = jnp.maximum(jnp.max(sh, axis=-1, keepdims=True), sink)
            ps.append(jnp.exp(sh - m).astype(bf16))
            es.append(jnp.exp(sink - m))
        vcat = jnp.concatenate([vmetas_a[g], a_window(KV_VA, g, j)], axis=0)
        o = jnp.dot(jnp.concatenate(ps, axis=0), with_ones(vcat), preferred_element_type=f32)
        for t in range(a_tiles):
            oe = o[t * BLOCK:(t + 1) * BLOCK]
            oo = o[(a_tiles + t) * BLOCK:(a_tiles + t + 1) * BLOCK]
            den = (jnp.where(lo_half, oe[:, LANES:], oo[:, LANES:])
                   + jnp.where(lo_half, es[t], es[a_tiles + t]))
            oa_ref[j * BLOCK:(j + 1) * BLOCK, (a_tiles * g + t) * LANES:(a_tiles * g + t + 1) * LANES] = (
                jnp.where(lo_half, oe[:, :LANES], oo[:, :LANES]) / den)

    a_stages = [(a_scores, a_finish, (g, j)) for last in (False, True) for g in range(A_KV_HEADS)
                for j in range(BLOCKS_PER_CHUNK) if (j == BLOCKS_PER_CHUNK - 1) == last]

    kmetas = [meta_tile(KV_KB + p * LANES) for p in range(B_HEADS // 2)]
    vmetas = [meta_tile(KV_VB + p * LANES) for p in range(B_HEADS // 2)]

    def unit_geometry(step):
        r0 = i * ROWS_PER_CHUNK + 2 * step
        rs0 = jnp.clip(r0 - NA_KH // 2, 0, n_rows - NA_KH)
        rs1 = jnp.clip(r0 + 1 - NA_KH // 2, 0, n_rows - NA_KH)
        shifts = (r0 - rs0, r0 + 1 - rs0)
        variant = 1 - (rs1 - rs0)
        woff = pl.multiple_of((rs0 - i * ROWS_PER_CHUNK) * GRID_W + HALO, BLOCK)
        return shifts, variant, woff

    geometry = [unit_geometry(step) for step in range(ROWS_PER_CHUNK // 2)]

    def b_scores(item):
        step, pairs = item
        _, variant, woff = geometry[step]
        out = []
        for p in pairs:
            kcat = jnp.concatenate(
                [win_ref[pl.ds(woff, B_WIN_KEYS), KV_KB + p * LANES:KV_KB + (p + 1) * LANES],
                 kmetas[p]], axis=0)
            k_aug = jnp.concatenate([kcat, brow_ref[variant]], axis=1)
            qs = split_heads(qb_ref[step * BLOCK:(step + 1) * BLOCK, p * LANES:(p + 1) * LANES])
            q_aug = jnp.concatenate([qs, bsel_ref[...]], axis=1)
            out.append(lax.dot_general(q_aug, k_aug, _NT, preferred_element_type=f32))
        return out

    def b_finish(item, scores):
        step, pairs = item
        shifts, _, woff = geometry[step]
        probs = []
        for p, s in zip(pairs, scores):
            bias = jnp.concatenate(
                [jnp.concatenate([bpair_ref[p, 2 * jj + NA_KH - 1 - shifts[u], hh]
                                  for jj in range(B_WIN_ROWS // 2)], axis=1)
                 for hh in range(2) for u in range(2)], axis=0)
            s_loc = s[:, :B_WIN_KEYS] + bias
            s_met = s[:, B_WIN_KEYS:]
            m = jnp.maximum(jnp.max(s_loc, axis=-1, keepdims=True),
                            jnp.max(s_met, axis=-1, keepdims=True))
            probs.append(jnp.concatenate([jnp.exp(s_loc - m), jnp.exp(s_met - m)],
                                         axis=1).astype(bf16))
        for p, pr in zip(pairs, probs):
            vcat = jnp.concatenate(
                [win_ref[pl.ds(woff, B_WIN_KEYS), KV_VB + p * LANES:KV_VB + (p + 1) * LANES],
                 vmetas[p]], axis=0)
            o = jnp.dot(pr, with_ones(vcat), preferred_element_type=f32)
            o = o[:, :LANES] / o[:, LANES:]
            ob_ref[step * BLOCK:(step + 1) * BLOCK, p * LANES:(p + 1) * LANES] = jnp.where(
                lo_half, o[:BLOCK], o[BLOCK:])

    b_stages = [(b_scores, b_finish, (step, tuple(range(p0, p0 + B_PAIRS_PER_ITEM))))
                for step in range(ROWS_PER_CHUNK // 2)
                for p0 in range(0, B_HEADS // 2, B_PAIRS_PER_ITEM)]
    first_reader = next(t for t, (_, _, (g, j)) in enumerate(a_stages) if j == BLOCKS_PER_CHUNK - 1)
    assert sum(FILL_PLAN[:first_reader - 1]) >= 1 + n_kv_b + 1
    assert len(a_stages) + len(b_stages) == len(FILL_PLAN) and sum(FILL_PLAN) == len(fills)
    pipelined(a_stages + b_stages, FILL_PLAN)


    def gated_branch(o_ref, z_col, w_row, g_col):
        z = proj_ref[:, z_col:z_col + A_WIDTH]
        a = (o_ref[...] * (z * jax.nn.sigmoid(z))).astype(bf16)
        y = jnp.dot(a, wtail_ref[w_row:w_row + A_WIDTH], preferred_element_type=f32)
        return jax.nn.sigmoid(proj_ref[:, g_col:g_col + D_MODEL]) * y

    merged = gated_branch(oa_ref, P_ZA, T_PA, P_GA) + gated_branch(ob_ref, P_ZB, T_PB, P_GB)
    h = x + jnp.dot(merged.astype(bf16), wtail_ref[T_OUT:T_OUT + D_MODEL], preferred_element_type=f32)
    out_ref[0] = _rmsnorm(h, fgain_ref[...])


def _layer_call(sink, x, meta, rope_base, rope_off, gain, fgain, w, w_tail,
                amasks, btables):
    bsz, seq, _ = x.shape
    n_chunks = seq // CHUNK
    n_steps = bsz * n_chunks
    const2 = lambda t: (0, 0)
    const3 = lambda t: (0, 0, 0)
    resident = dict(pipeline_mode=pl.Buffered(1))
    body = functools.partial(_layer_body, n_chunks=n_chunks, n_rows=seq // GRID_W)

    def chunk_of(t):
        return t // n_chunks, t % n_chunks

    def next_chunk(t):
        return chunk_of(jnp.minimum(t + 1, n_steps - 1))

    return pl.pallas_call(
        body,
        grid=(n_steps,),
        in_specs=[
            pl.BlockSpec(memory_space=pltpu.SMEM),
            pl.BlockSpec((1, CHUNK, D_MODEL), lambda t: (*chunk_of(t), 0)),
            pl.BlockSpec((1, CHUNK, D_MODEL), lambda t: (*next_chunk(t), 0)),
            pl.BlockSpec((N_META, D_MODEL), const2, **resident),
            pl.BlockSpec((CHUNK, LANES), const2, **resident),
            pl.BlockSpec((CHUNK, LANES), const2, **resident),
            pl.BlockSpec((None, 2, LANES), lambda t: (chunk_of(t)[1], 0, 0)),
            pl.BlockSpec((None, 2, LANES), lambda t: (next_chunk(t)[1], 0, 0)),
            pl.BlockSpec((1, D_MODEL), const2, **resident),
            pl.BlockSpec((1, D_MODEL), const2, **resident),
            pl.BlockSpec((D_MODEL, IN_COLS), const2, **resident),
            pl.BlockSpec(memory_space=pl.ANY),
            pl.BlockSpec((3, META_PAD + 3 * BLOCK, BLOCK), const3, **resident),
            pl.BlockSpec((A_GROUP * BLOCK, BLOCK), const2, **resident),
            pl.BlockSpec(memory_space=pl.ANY),
            pl.BlockSpec((2, B_WIN_KEYS + META_PAD, LANES), const3, **resident),
            pl.BlockSpec((2 * BLOCK, LANES), const2, **resident),
        ],
        out_specs=pl.BlockSpec((1, CHUNK, D_MODEL), lambda t: (*chunk_of(t), 0)),
        out_shape=jax.ShapeDtypeStruct((bsz, seq, D_MODEL), jnp.float32),
        scratch_shapes=[
            pltpu.VMEM((2, CHUNK, D_MODEL), jnp.bfloat16),
            pltpu.VMEM((3, CHUNK, KV_COLS), jnp.bfloat16),
            pltpu.VMEM((N_META, KV_COLS), jnp.bfloat16),
            pltpu.VMEM((CHUNK + 2 * HALO, KV_COLS), jnp.bfloat16),
            pltpu.VMEM((CHUNK, A_WIDTH), jnp.bfloat16),
            pltpu.VMEM((CHUNK, B_WIDTH), jnp.bfloat16),
            pltpu.VMEM((CHUNK, A_WIDTH), jnp.float32),
            pltpu.VMEM((CHUNK, B_WIDTH), jnp.float32),
            pltpu.VMEM((CHUNK, P_COLS), jnp.float32),
            pltpu.VMEM((T_ROWS, D_MODEL), jnp.bfloat16),
            pltpu.VMEM((B_HEADS // 2, 2 * NA_KH, 2, GRID_W, LANES), jnp.float32),
            pltpu.SemaphoreType.DMA((2,)),
        ],
        compiler_params=pltpu.CompilerParams(
            dimension_semantics=("arbitrary",), vmem_limit_bytes=VMEM_LIMIT_BYTES),
        name="hybrid_layer",
    )(sink, x, x, meta, *rope_base, rope_off, rope_off, gain, fgain, w, w_tail,
      *amasks, *btables)


def _rope_tables(n_chunks):
    d = np.arange(LANES) % HEAD_DIM
    inv_freq = ROPE_THETA ** (-jnp.asarray(d % ROT_HALF, jnp.float32) / ROT_HALF)
    inv_lane = jnp.where(d < ROT_DIM, inv_freq, 0.0)[None, :]
    base = jnp.arange(CHUNK, dtype=jnp.float32)[:, None] * inv_lane
    off = (N_META + CHUNK * jnp.arange(n_chunks, dtype=jnp.float32))[:, None] * inv_lane
    return (jnp.cos(base), jnp.sin(base)), jnp.stack([jnp.cos(off), jnp.sin(off)], axis=1)


def _band_masks():
    qi = np.arange(BLOCK)[:, None]
    col = np.arange(META_PAD + 3 * BLOCK)[None, :]
    kj = col - META_PAD
    band = (col >= META_PAD) & (np.abs(kj - BLOCK - qi) <= WINDOW)
    meta = np.broadcast_to(col < N_META, band.shape)
    variants = [meta | band, meta | (band & (kj >= BLOCK)), meta | (band & (kj < 2 * BLOCK))]
    masks = np.where(np.stack(variants), 0.0, NEG_INF).astype(np.float32).transpose(0, 2, 1)
    eye = np.tile(np.eye(BLOCK, dtype=np.float32), (A_GROUP, 1))
    return jnp.asarray(masks, jnp.bfloat16), jnp.asarray(eye, jnp.bfloat16)


def _na_bias(rpb):
    cq = np.arange(GRID_W)[:, None]
    kc = np.arange(GRID_W)[None, :]
    start = np.clip(cq - NA_KW // 2, 0, GRID_W - NA_KW)
    valid = (kc >= start) & (kc < start + NA_KW)
    n_pairs, n_dr, n_dc = B_HEADS // 2, 2 * NA_KH - 1, 2 * NA_KW - 1
    onehot = ((kc - cq + NA_KW - 1)[..., None] == np.arange(n_dc)) & valid[..., None]
    select = np.zeros((GRID_W, 2, n_dc, 2, GRID_W), np.float32)
    for t in range(2):
        select[:, t, :, t, :] = onehot.transpose(0, 2, 1)
    mask = np.where(np.tile(valid, (1, 2)), 0.0, NEG_INF).astype(np.float32)
    select = np.concatenate([select.reshape(GRID_W, 2 * n_dc, LANES), mask[:, None, :]], axis=1)
    rows = jnp.pad(rpb.reshape(n_pairs, 2, n_dr, n_dc), ((0, 0), (0, 0), (0, 2), (0, 0)))
    rows = jnp.stack([rows[:, :, :2 * NA_KH], rows[:, :, 1:]], axis=3)
    rows = jnp.concatenate([rows.reshape(n_pairs, 2, 2 * NA_KH, 2 * n_dc),
                            jnp.ones((n_pairs, 2, 2 * NA_KH, 1), jnp.float32)], axis=-1)
    pair = jnp.einsum('prhe,cen->prhcn', jnp.transpose(rows, (0, 2, 1, 3)), jnp.asarray(select),
                      precision=lax.Precision.HIGHEST)

    key_row = np.arange(B_WIN_KEYS + META_PAD) // GRID_W
    is_meta = (np.arange(B_WIN_KEYS + META_PAD) >= B_WIN_KEYS) & (
        np.arange(B_WIN_KEYS + META_PAD) < B_WIN_KEYS + N_META)
    rows = np.zeros((2, B_WIN_KEYS + META_PAD, LANES), np.float32)
    for variant, starts in enumerate(((0, 1), (0, 0))):
        for u, s0 in enumerate(starts):
            seen = ((key_row >= s0) & (key_row < s0 + NA_KH)) | is_meta
            rows[variant, :, u] = np.where(seen, 0.0, NEG_INF)
    sel = np.zeros((2, 2, GRID_W, LANES), np.float32)
    sel[:, 0, :, 0] = 1.0
    sel[:, 1, :, 1] = 1.0
    return (pair, jnp.asarray(rows, jnp.bfloat16),
            jnp.asarray(sel.reshape(2 * BLOCK, LANES), jnp.bfloat16))


def kernel(x, meta_tokens, norm_gain, w_in, sink_logits, rel_pos_bias, w_proj_a, w_proj_b, w_out,
           final_norm_gain):
    bsz, seq, _ = x.shape
    assert seq % CHUNK == 0 and seq // GRID_W >= NA_KH and norm_gain.shape[0] == 1
    bf16 = jnp.bfloat16
    w = w_in[0].astype(bf16)
    w_tail = jnp.concatenate([w_proj_a[0], w_proj_b[0], w_out[0]], axis=0).astype(bf16)

    rope_base, rope_off = _rope_tables(seq // CHUNK)
    gain = norm_gain[0][None]
    return _layer_call(sink_logits[0], x, meta_tokens, rope_base, rope_off, gain,
                       final_norm_gain[None], w, w_tail, _band_masks(), _na_bias(rel_pos_bias[0]))
```

```python
import functools

import numpy as np
import jax
import jax.numpy as jnp
from jax import lax
from jax.experimental import pallas as pl
from jax.experimental.pallas import tpu as pltpu

D_MODEL = 1024
N_META = 16
HEAD_DIM = 64
RMS_EPS = 1e-6
NEG_INF = -1e30
A_HEADS = 8
A_KV_HEADS = 2
A_GROUP = A_HEADS // A_KV_HEADS
A_WIDTH = A_HEADS * HEAD_DIM
A_KV_WIDTH = A_KV_HEADS * HEAD_DIM
WINDOW = 128
BLOCK = 128
ROT_DIM = HEAD_DIM // 4
ROT_HALF = ROT_DIM // 2
ROPE_THETA = 500000.0
B_HEADS = 8
B_WIDTH = B_HEADS * HEAD_DIM
GRID_W = 64
NA_KH = 8
NA_KW = 16
SPLIT_SIZES = (A_WIDTH, A_KV_WIDTH, A_KV_WIDTH, A_WIDTH, B_WIDTH, B_WIDTH, B_WIDTH, B_WIDTH,
               D_MODEL, D_MODEL)

LANES = 128
CHUNK = 512
ROWS_PER_CHUNK = CHUNK // GRID_W
BLOCKS_PER_CHUNK = CHUNK // BLOCK
HALO = CHUNK // 2
assert HALO >= BLOCK and HALO >= (NA_KH // 2) * GRID_W
META_PAD = LANES
B_WIN_ROWS = NA_KH + 2
B_WIN_KEYS = B_WIN_ROWS * GRID_W
B_PAIRS_PER_ITEM = 2
PROJ_PIECE = 256
FILL_PLAN = (2, 1, 1, 1, 2, 1, 1, 1, 1, 1, 1, 1, 1, 1, 1, 1)

KV_KA = 0
KV_VA = KV_KA + A_KV_HEADS * LANES
KV_KB = KV_VA + A_KV_HEADS * LANES
KV_VB = KV_KB + B_WIDTH
KV_COLS = KV_VB + B_WIDTH

(C_QA, C_KA, C_VA, C_ZA, C_QB, C_KB, C_VB, C_ZB, C_GA, C_GB, IN_COLS) = (
    int(c) for c in np.cumsum((0,) + SPLIT_SIZES))
assert A_KV_WIDTH == LANES and C_VA == C_KA + A_KV_WIDTH and C_VB == C_KB + B_WIDTH

P_ZA = 0
P_ZB = P_ZA + A_WIDTH
P_GA = P_ZB + B_WIDTH
P_GB = P_GA + D_MODEL
P_COLS = P_GB + D_MODEL
PROJ_SOURCES = tuple(
    src + off for src, width in ((C_ZA, A_WIDTH), (C_ZB, B_WIDTH), (C_GA, D_MODEL), (C_GB, D_MODEL))
    for off in range(0, width, PROJ_PIECE))

T_PA = 0
T_PB = T_PA + A_WIDTH
T_OUT = T_PB + B_WIDTH
T_ROWS = T_OUT + D_MODEL

VMEM_LIMIT_BYTES = 58 * 1024 * 1024

_NT = (((1,), (1,)), ((), ()))


def _rmsnorm(x, gain):
    return x * lax.rsqrt(jnp.mean(x * x, axis=-1, keepdims=True) + RMS_EPS) * gain


def _lane_patterns():
    lane = lax.broadcasted_iota(jnp.int32, (1, LANES), 1)
    return lane < HEAD_DIM, (lane & (HEAD_DIM - 1)) < ROT_HALF


def _rope_coeffs(cb_ref, sb_ref, off_ref, first_half):
    cb, sb = cb_ref[...], sb_ref[...]
    ca, sa = off_ref[0:1, :], off_ref[1:2, :]
    c = cb * ca - sb * sa
    s = (sb * ca + cb * sa) * jnp.where(first_half, -1.0, 1.0)
    return c, s


def _rope_tile(t, c, s, first_half):
    partner = jnp.where(first_half, pltpu.roll(t, LANES - ROT_HALF, 1), pltpu.roll(t, ROT_HALF, 1))
    return t * c + partner * s


def _layer_body(sink_ref, x_ref, xn_ref, meta_ref, cb_ref, sb_ref, off_ref, offn_ref,
                gain_ref, fgain_ref, wr_hbm, wtail_hbm, amask_hbm, eye_hbm, bpair_hbm,
                brow_hbm, bsel_hbm,
                out_ref, n_ref, ring_ref, kvm_ref, win_ref, qa_ref, qb_ref, oa_ref, ob_ref, proj_ref,
                wtail_ref, bpair_ref, wr_ref, amask_ref, eye_ref, brow_ref, bsel_ref, late_sem,
                *, n_chunks, n_rows):
    step = pl.program_id(0)
    i = step % n_chunks
    cur, nxt = step % 2, (step + 1) % 2
    slot_prev, slot_cur, slot_nxt = (step + 2) % 3, step % 3, (step + 1) % 3
    f32, bf16 = jnp.float32, jnp.bfloat16
    scale = HEAD_DIM ** -0.5
    lo_half, first_half = _lane_patterns()

    def kv_a_piece(n, c, s, store):
        kv_a = jnp.dot(n, wr_ref[:, C_KA:C_KA + 2 * A_KV_WIDTH], preferred_element_type=f32)
        ka = _rope_tile(kv_a[:, :LANES], c, s, first_half)
        va = kv_a[:, LANES:]
        for src, col in ((ka, KV_KA), (va, KV_VA)):
            swapped = pltpu.roll(src, HEAD_DIM, 1)
            store(col, jnp.where(lo_half, src, swapped).astype(bf16))
            store(col + LANES, jnp.where(lo_half, swapped, src).astype(bf16))

    def kv_b_piece(n, k, store):
        src = C_KB + k * PROJ_PIECE
        store(KV_KB + k * PROJ_PIECE,
              jnp.dot(n, wr_ref[:, src:src + PROJ_PIECE], preferred_element_type=f32).astype(bf16))

    n_kv_b = 2 * B_WIDTH // PROJ_PIECE

    def ring_store(slot):
        def store(col, value):
            ring_ref[slot, :, col:col + value.shape[1]] = value
        return store

    def meta_store(col, value):
        kvm_ref[:, col:col + value.shape[1]] = value

    @pl.when(step == 0)
    def _():
        pairs = ((wr_hbm, wr_ref), (wtail_hbm, wtail_ref), (bpair_hbm, bpair_ref), (amask_hbm, amask_ref),
                 (eye_hbm, eye_ref), (brow_hbm, brow_ref), (bsel_hbm, bsel_ref))
        copies = [pltpu.make_async_copy(src, dst, late_sem.at[k]) for k, (src, dst) in enumerate(pairs)]
        for copy in copies:
            copy.start()
        late_copies = copies[1:]
        ring_ref[...] = jnp.zeros(ring_ref.shape, bf16)
        copies[0].wait()
        n_ref[0] = _rmsnorm(x_ref[0], gain_ref[...]).astype(bf16)
        c0, s0 = _rope_coeffs(cb_ref, sb_ref, off_ref, first_half)
        kv_a_piece(n_ref[0], c0, s0, ring_store(0))
        for k in range(n_kv_b):
            kv_b_piece(n_ref[0], k, ring_store(0))
        n_meta = _rmsnorm(meta_ref[...], gain_ref[...]).astype(bf16)
        sign = jnp.where(first_half, -1.0, 1.0)
        kv_a_piece(n_meta, cb_ref[0:N_META], sb_ref[0:N_META] * sign, meta_store)
        for k in range(n_kv_b):
            kv_b_piece(n_meta, k, meta_store)
        for copy in late_copies:
            copy.wait()

    x = x_ref[0]
    n_ref[nxt] = _rmsnorm(xn_ref[0], gain_ref[...]).astype(bf16)

    win_ref[0:HALO] = ring_ref[slot_prev, CHUNK - HALO:CHUNK]
    win_ref[HALO:HALO + CHUNK] = ring_ref[slot_cur]

    def next_halo():
        win_ref[HALO + CHUNK:CHUNK + 2 * HALO] = ring_ref[slot_nxt, 0:HALO]

    c, s = _rope_coeffs(cb_ref, sb_ref, off_ref, first_half)
    qa = jnp.dot(n_ref[cur], wr_ref[:, C_QA:C_QA + A_WIDTH], preferred_element_type=f32)
    for t in range(A_WIDTH // LANES):
        sl = slice(t * LANES, (t + 1) * LANES)
        qa_ref[:, sl] = (_rope_tile(qa[:, sl], c, s, first_half) * scale).astype(bf16)
    qb = jnp.dot(n_ref[cur], wr_ref[:, C_QB:C_QB + B_WIDTH], preferred_element_type=f32)
    qb_ref[...] = (qb * scale).astype(bf16)

    zero_pad = jnp.zeros((META_PAD - N_META, LANES), bf16)

    def meta_tile(col):
        return jnp.concatenate([kvm_ref[:, col:col + LANES], zero_pad], axis=0)

    def split_heads(q):
        zeros = jnp.zeros_like(q)
        return jnp.concatenate([jnp.where(lo_half, q, zeros), jnp.where(lo_half, zeros, q)], axis=0)

    def with_ones(v):
        return jnp.concatenate([v, jnp.ones_like(v)], axis=1)

    def proj_piece(k):
        src = PROJ_SOURCES[k]
        proj_ref[:, k * PROJ_PIECE:(k + 1) * PROJ_PIECE] = jnp.dot(
            n_ref[cur], wr_ref[:, src:src + PROJ_PIECE], preferred_element_type=f32)

    def next_kv_a():
        cn, sn = _rope_coeffs(cb_ref, sb_ref, offn_ref, first_half)
        kv_a_piece(n_ref[nxt], cn, sn, ring_store(slot_nxt))

    def next_kv_b(k):
        kv_b_piece(n_ref[nxt], k, ring_store(slot_nxt))

    fills = ([next_kv_a] + [functools.partial(next_kv_b, k) for k in range(n_kv_b)] + [next_halo]
             + [functools.partial(proj_piece, k) for k in range(len(PROJ_SOURCES))])

    def pipelined(stages, fill_plan):
        pending = stages[0][0](stages[0][2])
        for t, (_, finish, item) in enumerate(stages):
            nxt_scores = stages[t + 1][0](stages[t + 1][2]) if t + 1 < len(stages) else None
            for _ in range(fill_plan[t]):
                fills.pop(0)()
            finish(item, pending)
            pending = nxt_scores

    a_tiles = A_GROUP // 2
    kmetas_a = [meta_tile(KV_KA + g * LANES) for g in range(A_KV_HEADS)]
    vmetas_a = [meta_tile(KV_VA + g * LANES) for g in range(A_KV_HEADS)]

    def a_window(col, tile, j):
        w0 = HALO + j * BLOCK - BLOCK
        return win_ref[w0:w0 + 3 * BLOCK, col + tile * LANES:col + (tile + 1) * LANES]

    def a_scores(item):
        g, j = item
        first = jnp.logical_and(i == 0, j == 0)
        last = jnp.logical_and(i == n_chunks - 1, j == BLOCKS_PER_CHUNK - 1)
        mask_t = amask_ref[jnp.where(first, 1, jnp.where(last, 2, 0))]
        kcat = jnp.concatenate([kmetas_a[g], a_window(KV_KA, g, j)], axis=0)
        k_aug = jnp.concatenate([kcat, mask_t], axis=1)
        q_tiles = [qa_ref[j * BLOCK:(j + 1) * BLOCK, (a_tiles * g + t) * LANES:(a_tiles * g + t + 1) * LANES]
                   for t in range(a_tiles)]
        zeros = jnp.zeros_like(q_tiles[0])
        qs = jnp.concatenate([jnp.where(lo_half, q, zeros) for q in q_tiles]
                             + [jnp.where(lo_half, zeros, q) for q in q_tiles], axis=0)
        q_aug = jnp.concatenate([qs, eye_ref[...]], axis=1)
        return lax.dot_general(q_aug, k_aug, _NT, preferred_element_type=f32)

    def a_finish(item, s):
        g, j = item
        ps, es = [], []
        for rb in range(A_GROUP):
            sink = sink_ref[A_GROUP * g + 2 * (rb % a_tiles) + rb // a_tiles]
            sh = s[rb * BLOCK:(rb + 1) * BLOCK]
            m = jnp.maximum(jnp.max(sh, axis=-1, keepdims=True), sink)
            ps.append(jnp.exp(sh - m).astype(bf16))
            es.append(jnp.exp(sink - m))
        vcat = jnp.concatenate([vmetas_a[g], a_window(KV_VA, g, j)], axis=0)
        o = jnp.dot(jnp.concatenate(ps, axis=0), with_ones(vcat), preferred_element_type=f32)
        for t in range(a_tiles):
            oe = o[t * BLOCK:(t + 1) * BLOCK]
            oo = o[(a_tiles + t) * BLOCK:(a_tiles + t + 1) * BLOCK]
            den = (jnp.where(lo_half, oe[:, LANES:], oo[:, LANES:])
                   + jnp.where(lo_half, es[t], es[a_tiles + t]))
            oa_ref[j * BLOCK:(j + 1) * BLOCK, (a_tiles * g + t) * LANES:(a_tiles * g + t + 1) * LANES] = (
                jnp.where(lo_half, oe[:, :LANES], oo[:, :LANES]) / den)

    a_stages = [(a_scores, a_finish, (g, j)) for last in (False, True) for g in range(A_KV_HEADS)
                for j in range(BLOCKS_PER_CHUNK) if (j == BLOCKS_PER_CHUNK - 1) == last]

    kmetas = [meta_tile(KV_KB + p * LANES) for p in range(B_HEADS // 2)]
    vmetas = [meta_tile(KV_VB + p * LANES) for p in range(B_HEADS // 2)]

    def unit_geometry(step):
        r0 = i * ROWS_PER_CHUNK + 2 * step
        rs0 = jnp.clip(r0 - NA_KH // 2, 0, n_rows - NA_KH)
        rs1 = jnp.clip(r0 + 1 - NA_KH // 2, 0, n_rows - NA_KH)
        shifts = (r0 - rs0, r0 + 1 - rs0)
        variant = 1 - (rs1 - rs0)
        woff = pl.multiple_of((rs0 - i * ROWS_PER_CHUNK) * GRID_W + HALO, BLOCK)
        return shifts, variant, woff

    geometry = [unit_geometry(step) for step in range(ROWS_PER_CHUNK // 2)]

    def b_scores(item):
        step, pairs = item
        _, variant, woff = geometry[step]
        out = []
        for p in pairs:
            kcat = jnp.concatenate(
                [win_ref[pl.ds(woff, B_WIN_KEYS), KV_KB + p * LANES:KV_KB + (p + 1) * LANES],
                 kmetas[p]], axis=0)
            k_aug = jnp.concatenate([kcat, brow_ref[variant]], axis=1)
            qs = split_heads(qb_ref[step * BLOCK:(step + 1) * BLOCK, p * LANES:(p + 1) * LANES])
            q_aug = jnp.concatenate([qs, bsel_ref[...]], axis=1)
            out.append(lax.dot_general(q_aug, k_aug, _NT, preferred_element_type=f32))
        return out

    def b_finish(item, scores):
        step, pairs = item
        shifts, _, woff = geometry[step]
        probs = []
        for p, s in zip(pairs, scores):
            bias = jnp.concatenate(
                [jnp.concatenate([bpair_ref[p, 2 * jj + NA_KH - 1 - shifts[u], hh]
                                  for jj in range(B_WIN_ROWS // 2)], axis=1)
                 for hh in range(2) for u in range(2)], axis=0)
            s_loc = s[:, :B_WIN_KEYS] + bias
            s_met = s[:, B_WIN_KEYS:]
            m = jnp.maximum(jnp.max(s_loc, axis=-1, keepdims=True),
                            jnp.max(s_met, axis=-1, keepdims=True))
            probs.append(jnp.concatenate([jnp.exp(s_loc - m), jnp.exp(s_met - m)],
                                         axis=1).astype(bf16))
        for p, pr in zip(pairs, probs):
            vcat = jnp.concatenate(
                [win_ref[pl.ds(woff, B_WIN_KEYS), KV_VB + p * LANES:KV_VB + (p + 1) * LANES],
                 vmetas[p]], axis=0)
            o = jnp.dot(pr, with_ones(vcat), preferred_element_type=f32)
            o = o[:, :LANES] / o[:, LANES:]
            ob_ref[step * BLOCK:(step + 1) * BLOCK, p * LANES:(p + 1) * LANES] = jnp.where(
                lo_half, o[:BLOCK], o[BLOCK:])

    b_stages = [(b_scores, b_finish, (step, tuple(range(p0, p0 + B_PAIRS_PER_ITEM))))
                for step in range(ROWS_PER_CHUNK // 2)
                for p0 in range(0, B_HEADS // 2, B_PAIRS_PER_ITEM)]
    first_reader = next(t for t, (_, _, (g, j)) in enumerate(a_stages) if j == BLOCKS_PER_CHUNK - 1)
    assert sum(FILL_PLAN[:first_reader - 1]) >= 1 + n_kv_b + 1
    assert len(a_stages) + len(b_stages) == len(FILL_PLAN) and sum(FILL_PLAN) == len(fills)
    pipelined(a_stages + b_stages, FILL_PLAN)


    def gated_branch(o_ref, z_col, w_row, g_col):
        z = proj_ref[:, z_col:z_col + A_WIDTH]
        a = (o_ref[...] * (z * jax.nn.sigmoid(z))).astype(bf16)
        y = jnp.dot(a, wtail_ref[w_row:w_row + A_WIDTH], preferred_element_type=f32)
        return jax.nn.sigmoid(proj_ref[:, g_col:g_col + D_MODEL]) * y

    merged = gated_branch(oa_ref, P_ZA, T_PA, P_GA) + gated_branch(ob_ref, P_ZB, T_PB, P_GB)
    h = x + jnp.dot(merged.astype(bf16), wtail_ref[T_OUT:T_OUT + D_MODEL], preferred_element_type=f32)
    out_ref[0] = _rmsnorm(h, fgain_ref[...])


def _layer_call(sink, x, meta, rope_base, rope_off, gain, fgain, w, w_tail,
                amasks, btables):
    bsz, seq, _ = x.shape
    n_chunks = seq // CHUNK
    n_steps = bsz * n_chunks
    const2 = lambda t: (0, 0)
    const3 = lambda t: (0, 0, 0)
    resident = dict(pipeline_mode=pl.Buffered(1))
    body = functools.partial(_layer_body, n_chunks=n_chunks, n_rows=seq // GRID_W)

    def chunk_of(t):
        return t // n_chunks, t % n_chunks

    def next_chunk(t):
        return chunk_of(jnp.minimum(t + 1, n_steps - 1))

    return pl.pallas_call(
        body,
        grid=(n_steps,),
        in_specs=[
            pl.BlockSpec(memory_space=pltpu.SMEM),
            pl.BlockSpec((1, CHUNK, D_MODEL), lambda t: (*chunk_of(t), 0)),
            pl.BlockSpec((1, CHUNK, D_MODEL), lambda t: (*next_chunk(t), 0)),
            pl.BlockSpec((N_META, D_MODEL), const2, **resident),
            pl.BlockSpec((CHUNK, LANES), const2, **resident),
            pl.BlockSpec((CHUNK, LANES), const2, **resident),
            pl.BlockSpec((None, 2, LANES), lambda t: (chunk_of(t)[1], 0, 0)),
            pl.BlockSpec((None, 2, LANES), lambda t: (next_chunk(t)[1], 0, 0)),
            pl.BlockSpec((1, D_MODEL), const2, **resident),
            pl.BlockSpec((1, D_MODEL), const2, **resident),
            pl.BlockSpec(memory_space=pl.ANY),
            pl.BlockSpec(memory_space=pl.ANY),
            pl.BlockSpec(memory_space=pl.ANY),
            pl.BlockSpec(memory_space=pl.ANY),
            pl.BlockSpec(memory_space=pl.ANY),
            pl.BlockSpec(memory_space=pl.ANY),
            pl.BlockSpec(memory_space=pl.ANY),
        ],
        out_specs=pl.BlockSpec((1, CHUNK, D_MODEL), lambda t: (*chunk_of(t), 0)),
        out_shape=jax.ShapeDtypeStruct((bsz, seq, D_MODEL), jnp.float32),
        scratch_shapes=[
            pltpu.VMEM((2, CHUNK, D_MODEL), jnp.bfloat16),
            pltpu.VMEM((3, CHUNK, KV_COLS), jnp.bfloat16),
            pltpu.VMEM((N_META, KV_COLS), jnp.bfloat16),
            pltpu.VMEM((CHUNK + 2 * HALO, KV_COLS), jnp.bfloat16),
            pltpu.VMEM((CHUNK, A_WIDTH), jnp.bfloat16),
            pltpu.VMEM((CHUNK, B_WIDTH), jnp.bfloat16),
            pltpu.VMEM((CHUNK, A_WIDTH), jnp.float32),
            pltpu.VMEM((CHUNK, B_WIDTH), jnp.float32),
            pltpu.VMEM((CHUNK, P_COLS), jnp.float32),
            pltpu.VMEM((T_ROWS, D_MODEL), jnp.bfloat16),
            pltpu.VMEM((B_HEADS // 2, 2 * NA_KH, 2, GRID_W, LANES), jnp.float32),
            pltpu.VMEM((D_MODEL, IN_COLS), jnp.bfloat16),
            pltpu.VMEM((3, META_PAD + 3 * BLOCK, BLOCK), jnp.bfloat16),
            pltpu.VMEM((A_GROUP * BLOCK, BLOCK), jnp.bfloat16),
            pltpu.VMEM((2, B_WIN_KEYS + META_PAD, LANES), jnp.bfloat16),
            pltpu.VMEM((2 * BLOCK, LANES), jnp.bfloat16),
            pltpu.SemaphoreType.DMA((7,)),
        ],
        compiler_params=pltpu.CompilerParams(
            dimension_semantics=("arbitrary",), vmem_limit_bytes=VMEM_LIMIT_BYTES),
        name="hybrid_layer",
    )(sink, x, x, meta, *rope_base, rope_off, rope_off, gain, fgain, w, w_tail,
      *amasks, *btables)


def _rope_tables(n_chunks):
    d = np.arange(LANES) % HEAD_DIM
    inv_freq = ROPE_THETA ** (-jnp.asarray(d % ROT_HALF, jnp.float32) / ROT_HALF)
    inv_lane = jnp.where(d < ROT_DIM, inv_freq, 0.0)[None, :]
    base = jnp.arange(CHUNK, dtype=jnp.float32)[:, None] * inv_lane
    off = (N_META + CHUNK * jnp.arange(n_chunks, dtype=jnp.float32))[:, None] * inv_lane
    return (jnp.cos(base), jnp.sin(base)), jnp.stack([jnp.cos(off), jnp.sin(off)], axis=1)


def _band_masks():
    qi = np.arange(BLOCK)[:, None]
    col = np.arange(META_PAD + 3 * BLOCK)[None, :]
    kj = col - META_PAD
    band = (col >= META_PAD) & (np.abs(kj - BLOCK - qi) <= WINDOW)
    meta = np.broadcast_to(col < N_META, band.shape)
    variants = [meta | band, meta | (band & (kj >= BLOCK)), meta | (band & (kj < 2 * BLOCK))]
    masks = np.where(np.stack(variants), 0.0, NEG_INF).astype(np.float32).transpose(0, 2, 1)
    eye = np.tile(np.eye(BLOCK, dtype=np.float32), (A_GROUP, 1))
    return jnp.asarray(masks, jnp.bfloat16), jnp.asarray(eye, jnp.bfloat16)


def _na_bias(rpb):
    cq = np.arange(GRID_W)[:, None]
    kc = np.arange(GRID_W)[None, :]
    start = np.clip(cq - NA_KW // 2, 0, GRID_W - NA_KW)
    valid = (kc >= start) & (kc < start + NA_KW)
    n_pairs, n_dr, n_dc = B_HEADS // 2, 2 * NA_KH - 1, 2 * NA_KW - 1
    onehot = ((kc - cq + NA_KW - 1)[..., None] == np.arange(n_dc)) & valid[..., None]
    select = np.zeros((GRID_W, 2, n_dc, 2, GRID_W), np.float32)
    for t in range(2):
        select[:, t, :, t, :] = onehot.transpose(0, 2, 1)
    mask = np.where(np.tile(valid, (1, 2)), 0.0, NEG_INF).astype(np.float32)
    select = np.concatenate([select.reshape(GRID_W, 2 * n_dc, LANES), mask[:, None, :]], axis=1)
    rows = jnp.pad(rpb.reshape(n_pairs, 2, n_dr, n_dc), ((0, 0), (0, 0), (0, 2), (0, 0)))
    rows = jnp.stack([rows[:, :, :2 * NA_KH], rows[:, :, 1:]], axis=3)
    rows = jnp.concatenate([rows.reshape(n_pairs, 2, 2 * NA_KH, 2 * n_dc),
                            jnp.ones((n_pairs, 2, 2 * NA_KH, 1), jnp.float32)], axis=-1)
    pair = jnp.einsum('prhe,cen->prhcn', jnp.transpose(rows, (0, 2, 1, 3)), jnp.asarray(select),
                      precision=lax.Precision.HIGHEST)

    key_row = np.arange(B_WIN_KEYS + META_PAD) // GRID_W
    is_meta = (np.arange(B_WIN_KEYS + META_PAD) >= B_WIN_KEYS) & (
        np.arange(B_WIN_KEYS + META_PAD) < B_WIN_KEYS + N_META)
    rows = np.zeros((2, B_WIN_KEYS + META_PAD, LANES), np.float32)
    for variant, starts in enumerate(((0, 1), (0, 0))):
        for u, s0 in enumerate(starts):
            seen = ((key_row >= s0) & (key_row < s0 + NA_KH)) | is_meta
            rows[variant, :, u] = np.where(seen, 0.0, NEG_INF)
    sel = np.zeros((2, 2, GRID_W, LANES), np.float32)
    sel[:, 0, :, 0] = 1.0
    sel[:, 1, :, 1] = 1.0
    return (pair, jnp.asarray(rows, jnp.bfloat16),
            jnp.asarray(sel.reshape(2 * BLOCK, LANES), jnp.bfloat16))


def kernel(x, meta_tokens, norm_gain, w_in, sink_logits, rel_pos_bias, w_proj_a, w_proj_b, w_out,
           final_norm_gain):
    bsz, seq, _ = x.shape
    assert seq % CHUNK == 0 and seq // GRID_W >= NA_KH and norm_gain.shape[0] == 1
    bf16 = jnp.bfloat16
    w = w_in[0].astype(bf16)
    w_tail = jnp.concatenate([w_proj_a[0], w_proj_b[0], w_out[0]], axis=0).astype(bf16)

    rope_base, rope_off = _rope_tables(seq // CHUNK)
    gain = norm_gain[0][None]
    return _layer_call(sink_logits[0], x, meta_tokens, rope_base, rope_off, gain,
                       final_norm_gain[None], w, w_tail, _band_masks(), _na_bias(rel_pos_bias[0]))
```
